```python
import math
import jax, jax.numpy as jnp
from jax import lax
import numpy as np

D_MODEL = 1024
BATCH = 2
SEQ = 8192
DEPTH = 4
DEC_BATCH = 128
DEC_SEQ = 8
PAST_LEN = 8192
PAGE_SIZE = 128

HEAD_DIM = 64
A_HEADS = 6
A_KV_HEADS = 2
A_GROUP = A_HEADS // A_KV_HEADS
A_WIDTH = A_HEADS * HEAD_DIM
KV_WIDTH = A_KV_HEADS * HEAD_DIM
WINDOW = 128
BLOCK = 128
B_HEADS = 6
B_WIDTH = B_HEADS * HEAD_DIM
DECAY_LORA = 32
ICLR_LORA = 32
GN_EPS = 6.4e-4
SHIFT_WIDTH = 3 * B_WIDTH + DECAY_LORA + ICLR_LORA
C_GROUPS = 4
C_GROUP_DIM = 64
C_WIDTH = C_GROUPS * C_GROUP_DIM
POOL_WINDOWS = (2, 4, 8, 16)
POOL_HIST = 15
MIX_WIDTH = A_WIDTH + B_WIDTH + C_WIDTH
IN_SPLITS = (A_WIDTH, KV_WIDTH, KV_WIDTH, A_WIDTH, SHIFT_WIDTH, B_WIDTH, C_WIDTH, C_WIDTH)
IN_WIDTH = A_WIDTH + 2 * KV_WIDTH + A_WIDTH + SHIFT_WIDTH + B_WIDTH + 2 * C_WIDTH
NORM_EPS = 1e-6

kernel_name = "hymba_swa_rwkv7_pool_step"

F32 = jnp.float32


def _offsets(sizes):
    out, acc = [], 0
    for s in sizes[:-1]:
        acc += s
        out.append(acc)
    return out


def rms_norm(x, g):
    xf = x.astype(F32)
    y = xf * lax.rsqrt(jnp.mean(xf * xf, axis=-1, keepdims=True) + NORM_EPS)
    return (y * g.astype(F32)).astype(x.dtype)


def sink_attend(q, k, v, mask, sinks):
    s = jnp.einsum('...tngd,...snd->...ngts', q.astype(F32), k.astype(F32)) * (HEAD_DIM ** -0.5)
    s = jnp.where(mask, s, -jnp.inf)
    sink = jnp.broadcast_to(sinks.astype(F32)[:, :, None, None], s.shape[:-1] + (1,))
    p = jax.nn.softmax(jnp.concatenate([s, sink], axis=-1), axis=-1)[..., :-1]
    return jnp.einsum('...ngts,...snd->...tngd', p, v.astype(F32))


def swa_prompt(q, k, v, sinks):
    B, S = q.shape[0], q.shape[1]
    nb = S // BLOCK
    qb = q.reshape(B, nb, BLOCK, A_KV_HEADS, A_GROUP, HEAD_DIM)
    kb = k.reshape(B, nb, BLOCK, A_KV_HEADS, HEAD_DIM)
    vb = v.reshape(B, nb, BLOCK, A_KV_HEADS, HEAD_DIM)

    def with_prev(t):
        prev = jnp.pad(t, ((0, 0), (1, 0), (0, 0), (0, 0), (0, 0)))[:, :-1]
        return jnp.concatenate([prev, t], axis=2)

    kc, vc = with_prev(kb), with_prev(vb)
    qi = jnp.arange(BLOCK)[:, None] + BLOCK
    kj = jnp.arange(2 * BLOCK)[None, :]
    rel = qi - kj
    band = (rel >= 0) & (rel < WINDOW)
    blk = jnp.arange(nb)[:, None, None]
    mask = band[None] & ((kj[None] >= BLOCK) | (blk > 0))
    o = sink_attend(qb, kc, vc, mask[None, :, None, None], sinks)
    return o.reshape(B, S, A_WIDTH)


def swa_sample(q, k, v, kbuf, vbuf, sinks):
    B, T = q.shape[0], q.shape[1]
    W = kbuf.shape[1]
    kc = jnp.concatenate([kbuf.astype(k.dtype), k], axis=1)
    vc = jnp.concatenate([vbuf.astype(v.dtype), v], axis=1)
    rel = (W + jnp.arange(T))[:, None] - jnp.arange(W + T)[None, :]
    mask = ((rel >= 0) & (rel < WINDOW))[None, None, None]
    o = sink_attend(q, kc, vc, mask, sinks)
    return o.reshape(B, T, A_WIDTH), kc[:, -W:], vc[:, -W:]


def rwkv7_mix(p, prev, wkv0, mu, w0, w2, a0, a2, k_k, k_a, r_k, gn_w, gn_b):
    B, T, _ = p.shape
    pf = p.astype(F32)
    shifted = jnp.concatenate([prev.astype(F32)[:, None], pf[:, :-1]], axis=1)
    xs = pf + (shifted - pf) * mu.astype(F32)
    r, k, v, wl, al = jnp.split(xs, [B_WIDTH, 2 * B_WIDTH, 3 * B_WIDTH, 3 * B_WIDTH + DECAY_LORA], axis=-1)
    w = -jax.nn.softplus(-(w0.astype(F32) + jnp.tanh(wl) @ w2.astype(F32))) - 0.5
    decay = jnp.exp(-jnp.exp(w))
    a = jax.nn.sigmoid(a0.astype(F32) + al @ a2.astype(F32))

    def heads(t):
        return t.reshape(B, T, B_HEADS, HEAD_DIM)

    kk = heads(k * k_k.astype(F32))
    kk = kk / jnp.maximum(jnp.sqrt(jnp.sum(kk * kk, axis=-1, keepdims=True)), 1e-12)
    k = k * (1.0 + (a - 1.0) * k_a.astype(F32))
    r, k, v, a, decay = heads(r), heads(k), heads(v), heads(a), heads(decay)

    def step(S, inp):
        r_t, w_t, k_t, v_t, kk_t, a_t = inp
        sa = jnp.einsum('bhvk,bhk->bhv', S, -kk_t)
        S = (S * w_t[:, :, None, :] + sa[..., None] * (kk_t * a_t)[:, :, None, :]
             + v_t[..., None] * k_t[:, :, None, :])
        return S, jnp.einsum('bhvk,bhk->bhv', S, r_t)

    def seq_first(t):
        return jnp.swapaxes(t, 0, 1)

    S_fin, y = lax.scan(step, wkv0.astype(F32),
                        (seq_first(r), seq_first(decay), seq_first(k), seq_first(v), seq_first(kk), seq_first(a)))
    y = seq_first(y)
    mean = jnp.mean(y, axis=-1, keepdims=True)
    var = jnp.mean(jnp.square(y - mean), axis=-1, keepdims=True)
    y = (y - mean) * lax.rsqrt(var + GN_EPS)
    y = y * gn_w.astype(F32).reshape(B_HEADS, HEAD_DIM) + gn_b.astype(F32).reshape(B_HEADS, HEAD_DIM)
    y = y + jnp.sum(r * k * r_k.astype(F32), axis=-1, keepdims=True) * v
    return y.reshape(B, T, B_WIDTH), S_fin, p[:, -1]


def pool_mix(u, hist, pos, w_pool, scale):
    B, T, _ = u.shape
    uf = u.astype(F32)
    xf = jnp.concatenate([hist.astype(F32), uf], axis=1)
    cs = jnp.concatenate([jnp.zeros((B, 1, C_WIDTH), F32), jnp.cumsum(xf, axis=1)], axis=1)
    end = cs[:, POOL_HIST + 1:POOL_HIST + 1 + T]
    means = []
    for g, w in enumerate(POOL_WINDOWS):
        sl = slice(g * C_GROUP_DIM, (g + 1) * C_GROUP_DIM)
        start = cs[:, POOL_HIST + 1 - w:POOL_HIST + 1 - w + T, sl]
        cnt = jnp.minimum(w, pos + 1).astype(F32)[None, :, None]
        means.append((end[..., sl] - start) / cnt)
    d = (jnp.concatenate(means, axis=-1) - uf).reshape(B, T, C_GROUPS, C_GROUP_DIM)
    y = jnp.einsum('btgc,gcd->btgd', d, w_pool.astype(F32)).reshape(B, T, C_WIDTH) * scale.astype(F32)
    new_hist = xf[:, -POOL_HIST:].astype(u.dtype)
    return y, new_hist


def mixer_layer(x, pos, kv_cache, shift_prev, wkv0, pool_hist, lp):
    B, T, _ = x.shape
    h = rms_norm(x, lp['norm_g'])
    proj = h @ lp['w_in']
    qa, ka, va, ga, pb, gb, uc, gc = jnp.split(proj, _offsets(IN_SPLITS), axis=-1)

    q = rms_norm(qa.reshape(B, T, A_KV_HEADS, A_GROUP, HEAD_DIM), lp['q_norm_g'])
    k = rms_norm(ka.reshape(B, T, A_KV_HEADS, HEAD_DIM), lp['k_norm_g'])
    v = va.reshape(B, T, A_KV_HEADS, HEAD_DIM)
    sinks = lp['attn_sinks'].reshape(A_KV_HEADS, A_GROUP)
    if kv_cache is None:
        o_a = swa_prompt(q, k, v, sinks)
        keep = min(WINDOW, T)
        new_k, new_v = k[:, -keep:], v[:, -keep:]
    else:
        o_a, new_k, new_v = swa_sample(q, k, v, kv_cache[0], kv_cache[1], sinks)

    o_b, new_wkv, new_shift = rwkv7_mix(pb, shift_prev, wkv0, lp['shift_mu'], lp['decay_w0'], lp['decay_w2'],
                                        lp['iclr_a0'], lp['iclr_a2'], lp['k_k'], lp['k_a'], lp['r_k'],
                                        lp['gn_w'], lp['gn_b'])

    o_c, new_pool = pool_mix(uc, pool_hist, pos, lp['pool_w'], lp['pool_scale'])

    gated = jnp.concatenate([o_a * jax.nn.silu(ga.astype(F32)),
                             o_b * jax.nn.silu(gb.astype(F32)),
                             o_c * jax.nn.silu(gc.astype(F32))], axis=-1).astype(x.dtype)
    x = x + gated @ lp['w_out']
    return x, (new_k, new_v, new_wkv, new_shift, new_pool)


def setup_inputs(seed: int = 0) -> dict:
    key = jax.random.key(seed)
    ks = jax.random.split(key, 32)
    w_buf = min(WINDOW, PAST_LEN)
    nrm = jax.random.normal
    return {
        "x_prompt": nrm(ks[0], (BATCH, SEQ, D_MODEL), F32),
        "x_sample": nrm(ks[1], (DEC_BATCH, DEC_SEQ, D_MODEL), F32),
        "cache_k": nrm(ks[2], (DEPTH, DEC_BATCH, w_buf, A_KV_HEADS, HEAD_DIM), F32),
        "cache_v": nrm(ks[3], (DEPTH, DEC_BATCH, w_buf, A_KV_HEADS, HEAD_DIM), F32),
        "state_wkv": 0.5 * nrm(ks[4], (DEPTH, DEC_BATCH, B_HEADS, HEAD_DIM, HEAD_DIM), F32),
        "state_shift": nrm(ks[5], (DEPTH, DEC_BATCH, SHIFT_WIDTH), F32),
        "state_pool": nrm(ks[6], (DEPTH, DEC_BATCH, POOL_HIST, C_WIDTH), F32),
        "norm_g": 1.0 + 0.02 * nrm(ks[7], (DEPTH, D_MODEL), F32),
        "w_in": nrm(ks[8], (DEPTH, D_MODEL, IN_WIDTH), F32) * D_MODEL ** -0.5,
        "q_norm_g": 1.0 + 0.02 * nrm(ks[9], (DEPTH, HEAD_DIM), F32),
        "k_norm_g": 1.0 + 0.02 * nrm(ks[10], (DEPTH, HEAD_DIM), F32),
        "attn_sinks": nrm(ks[11], (DEPTH, A_HEADS), F32),
        "shift_mu": jax.random.uniform(ks[12], (DEPTH, SHIFT_WIDTH), F32),
        "decay_w0": jax.random.uniform(ks[13], (DEPTH, B_WIDTH), F32, -3.0, 1.0),
        "decay_w2": 0.5 * nrm(ks[14], (DEPTH, DECAY_LORA, B_WIDTH), F32) * DECAY_LORA ** -0.5,
        "iclr_a0": 0.1 * nrm(ks[15], (DEPTH, B_WIDTH), F32),
        "iclr_a2": nrm(ks[16], (DEPTH, ICLR_LORA, B_WIDTH), F32) * ICLR_LORA ** -0.5,
        "k_k": 0.85 + 0.05 * nrm(ks[17], (DEPTH, B_WIDTH), F32),
        "k_a": 1.0 + 0.05 * nrm(ks[18], (DEPTH, B_WIDTH), F32),
        "r_k": 0.1 * nrm(ks[19], (DEPTH, B_HEADS, HEAD_DIM), F32),
        "gn_w": 1.0 + 0.02 * nrm(ks[20], (DEPTH, B_WIDTH), F32),
        "gn_b": 0.02 * nrm(ks[21], (DEPTH, B_WIDTH), F32),
        "pool_w": nrm(ks[22], (DEPTH, C_GROUPS, C_GROUP_DIM, C_GROUP_DIM), F32) * C_GROUP_DIM ** -0.5,
        "pool_scale": 1.0 + 0.1 * nrm(ks[23], (DEPTH, C_WIDTH), F32),
        "w_out": nrm(ks[24], (DEPTH, MIX_WIDTH, D_MODEL), F32) * MIX_WIDTH ** -0.5,
    }


def reference(x_prompt, x_sample, cache_k, cache_v, state_wkv, state_shift, state_pool,
              norm_g, w_in, q_norm_g, k_norm_g, attn_sinks, shift_mu, decay_w0, decay_w2,
              iclr_a0, iclr_a2, k_k, k_a, r_k, gn_w, gn_b, pool_w, pool_scale, w_out):
    Bp, S = x_prompt.shape[0], x_prompt.shape[1]
    T = x_sample.shape[1]
    pos_p = jnp.arange(S)
    pos_s = PAST_LEN + jnp.arange(T)
    zero_shift = jnp.zeros((Bp, SHIFT_WIDTH), x_prompt.dtype)
    zero_wkv = jnp.zeros((Bp, B_HEADS, HEAD_DIM, HEAD_DIM), F32)
    zero_pool = jnp.zeros((Bp, POOL_HIST, C_WIDTH), x_prompt.dtype)

    xp, xs = x_prompt, x_sample
    outs_p, outs_s = [], []
    for l in range(DEPTH):
        lp = dict(norm_g=norm_g[l], w_in=w_in[l], q_norm_g=q_norm_g[l], k_norm_g=k_norm_g[l],
                  attn_sinks=attn_sinks[l], shift_mu=shift_mu[l], decay_w0=decay_w0[l], decay_w2=decay_w2[l],
                  iclr_a0=iclr_a0[l], iclr_a2=iclr_a2[l], k_k=k_k[l], k_a=k_a[l], r_k=r_k[l],
                  gn_w=gn_w[l], gn_b=gn_b[l], pool_w=pool_w[l], pool_scale=pool_scale[l], w_out=w_out[l])
        xp, st_p = mixer_layer(xp, pos_p, None, zero_shift, zero_wkv, zero_pool, lp)
        xs, st_s = mixer_layer(xs, pos_s, (cache_k[l], cache_v[l]), state_shift[l], state_wkv[l],
                               state_pool[l], lp)
        outs_p.append(st_p)
        outs_s.append(st_s)

    def stack(outs, i):
        return jnp.stack([o[i] for o in outs], axis=0)

    return (xp, xs,
            stack(outs_p, 0), stack(outs_p, 1), stack(outs_p, 2), stack(outs_p, 3), stack(outs_p, 4),
            stack(outs_s, 0), stack(outs_s, 1), stack(outs_s, 2), stack(outs_s, 3), stack(outs_s, 4))
```

```python
import functools
import math

import jax
import jax.numpy as jnp
from jax import lax
from jax.experimental import pallas as pl
from jax.experimental.pallas import tpu as pltpu

F32 = jnp.float32
BF16 = jnp.bfloat16
HI = lax.Precision.HIGHEST

D_MODEL = 1024
DEPTH = 4
HEAD_DIM = 64
A_HEADS = 6
A_KV_HEADS = 2
A_GROUP = A_HEADS // A_KV_HEADS
A_WIDTH = A_HEADS * HEAD_DIM
KV_WIDTH = A_KV_HEADS * HEAD_DIM
WINDOW = 128
B_HEADS = 6
B_WIDTH = B_HEADS * HEAD_DIM
LORA = 32
GN_EPS = 6.4e-4
SHIFT_WIDTH = 3 * B_WIDTH + 2 * LORA
C_WIDTH = 256
POOL_HIST = 15
NORM_EPS = 1e-6
PAST_LEN = 8192

LANES = 128
TILE = 64
N_PAIRS = B_HEADS // 2

COL_Q, COL_GA, COL_R, COL_KB, COL_VB, COL_GB = 0, 384, 768, 1152, 1536, 1920
COL_K, COL_V, COL_UC, COL_GC, COL_WLAL = 2304, 2432, 2560, 2816, 3072
IN_PAD = 3200
SHIFT_PAD = 3 * B_WIDTH + LANES

VMEM_LIMIT = 48 * 1024 * 1024


def _dot(a, b, prec=None):
    return jnp.dot(a, b, preferred_element_type=F32, precision=prec)


def _dot_nt(a, b, prec=None):
    return lax.dot_general(a, b, (((1,), (1,)), ((), ())),
                           preferred_element_type=F32, precision=prec)


def _dot_tn(a, b, prec=None):
    return lax.dot_general(a, b, (((0,), (0,)), ((), ())),
                           preferred_element_type=F32, precision=prec)


def _sigmoid(x):
    return 1.0 / (1.0 + jnp.exp(-x))


def _silu(x):
    return x * _sigmoid(x)


def _lane_half_mask(rows=1):
    lane = lax.broadcasted_iota(jnp.int32, (rows, LANES), 1)
    return lane < HEAD_DIM


def _head_sums(x):
    lo = _lane_half_mask()
    s0 = jnp.sum(jnp.where(lo, x, 0.0), axis=-1, keepdims=True)
    s1 = jnp.sum(jnp.where(lo, 0.0, x), axis=-1, keepdims=True)
    return jnp.where(lo, s0, s1)


def _head_sums_wide(x):
    n = x.shape[1] // LANES
    return jnp.concatenate(
        [_head_sums(x[:, j * LANES:(j + 1) * LANES]) for j in range(n)], axis=1)


def _head_rms(x, g):
    ms = _head_sums_wide(x * x) * (1.0 / HEAD_DIM)
    return x * lax.rsqrt(ms + NORM_EPS) * g


def _inproj_kernel(x_ref, g_ref, w_ref, o_ref):
    x = x_ref[...]
    ms = jnp.mean(x * x, axis=-1, keepdims=True)
    h = x * lax.rsqrt(ms + NORM_EPS) * g_ref[...]
    o_ref[...] = _dot(h.astype(BF16), w_ref[...])


def _inproj(x, g, w, tm=256):
    n = x.shape[0]
    return pl.pallas_call(
        _inproj_kernel,
        grid=(n // tm,),
        in_specs=[pl.BlockSpec((tm, D_MODEL), lambda i: (i, 0)),
                  pl.BlockSpec((1, D_MODEL), lambda i: (0, 0)),
                  pl.BlockSpec((D_MODEL, IN_PAD), lambda i: (0, 0))],
        out_specs=pl.BlockSpec((tm, IN_PAD), lambda i: (i, 0)),
        out_shape=jax.ShapeDtypeStruct((n, IN_PAD), F32),
        compiler_params=pltpu.CompilerParams(
            dimension_semantics=("arbitrary",), vmem_limit_bytes=VMEM_LIMIT),
        name="inproj",
    )(x, g, w)


def _outproj_kernel(x_ref, oa_ref, ob_ref, oc_ref, wa_ref, wb_ref, wc_ref, o_ref):
    acc = _dot(oa_ref[...], wa_ref[...])
    acc += _dot(ob_ref[...], wb_ref[...])
    acc += _dot(oc_ref[...], wc_ref[...])
    o_ref[...] = x_ref[...] + acc


def _outproj(x, oa, ob, oc, w, tm=512):
    n = x.shape[0]
    return pl.pallas_call(
        _outproj_kernel,
        grid=(n // tm,),
        in_specs=[pl.BlockSpec((tm, D_MODEL), lambda i: (i, 0)),
                  pl.BlockSpec((tm, A_WIDTH), lambda i: (i, 0)),
                  pl.BlockSpec((tm, B_WIDTH), lambda i: (i, 0)),
                  pl.BlockSpec((tm, C_WIDTH), lambda i: (i, 0)),
                  pl.BlockSpec((A_WIDTH, D_MODEL), lambda i: (0, 0)),
                  pl.BlockSpec((B_WIDTH, D_MODEL), lambda i: (1, 0)),
                  pl.BlockSpec((C_WIDTH, D_MODEL), lambda i: (3, 0))],
        out_specs=pl.BlockSpec((tm, D_MODEL), lambda i: (i, 0)),
        out_shape=jax.ShapeDtypeStruct((n, D_MODEL), F32),
        compiler_params=pltpu.CompilerParams(
            dimension_semantics=("arbitrary",), vmem_limit_bytes=VMEM_LIMIT),
        name="outproj",
    )(x, oa, ob, oc, w, w, w)


def _sink_softmax_pv(s, ok, sink_col, v_bf16):
    sm = jnp.where(ok, s, -1e30)
    m = jnp.maximum(jnp.max(sm, axis=-1, keepdims=True), sink_col)
    p = jnp.where(ok, jnp.exp(sm - m), 0.0)
    den = jnp.sum(p, axis=-1, keepdims=True) + jnp.exp(sink_col - m)
    return _dot(p.astype(BF16), v_bf16) / den


def _attn_prompt_kernel(sink_ref, q_ref, ga_ref, k_ref, v_ref, qg_ref, kg_ref,
                        o_ref, kl_ref, vl_ref, kprev, vprev):
    i = pl.program_id(1)

    @pl.when(i == 0)
    def _():
        kprev[...] = jnp.zeros_like(kprev)
        vprev[...] = jnp.zeros_like(vprev)

    k = _head_rms(k_ref[...], kg_ref[...])
    v = v_ref[...]
    qn = _head_rms(q_ref[...], qg_ref[...])
    kc = jnp.concatenate([kprev[...], k], axis=0).astype(BF16)
    vc = jnp.concatenate([vprev[...], v], axis=0).astype(BF16)

    rows = A_GROUP * WINDOW
    r = lax.broadcasted_iota(jnp.int32, (rows, 2 * WINDOW), 0)
    j = lax.broadcasted_iota(jnp.int32, (rows, 2 * WINDOW), 1)
    rel = (r & (WINDOW - 1)) + WINDOW - j
    ok = (rel >= 0) & (rel < WINDOW) & ((j >= WINDOW) | (i > 0))
    rcol = lax.broadcasted_iota(jnp.int32, (rows, 1), 0)

    outs = []
    for n in range(A_KV_HEADS):
        heads = [A_GROUP * n + g for g in range(A_GROUP)]
        qs = jnp.concatenate(
            [qn[:, HEAD_DIM * h:HEAD_DIM * (h + 1)] for h in heads], axis=0)
        s = _dot_nt(qs.astype(BF16), kc[:, HEAD_DIM * n:HEAD_DIM * (n + 1)])
        s = s * (HEAD_DIM ** -0.5)
        sink_col = jnp.full((rows, 1), sink_ref[heads[-1]], F32)
        for g in reversed(range(A_GROUP - 1)):
            sink_col = jnp.where(rcol < (g + 1) * WINDOW, sink_ref[heads[g]], sink_col)
        o = _sink_softmax_pv(s, ok, sink_col, vc[:, HEAD_DIM * n:HEAD_DIM * (n + 1)])
        outs += [o[g * WINDOW:(g + 1) * WINDOW] for g in range(A_GROUP)]
    o_all = jnp.concatenate(outs, axis=1)
    o_ref[...] = (o_all * _silu(ga_ref[...])).astype(BF16)

    kprev[...] = k
    vprev[...] = v
    kl_ref[0] = k
    vl_ref[0] = v


def _attn_prompt(proj, sinks, qg, kg, batch, seq):
    nb = seq // WINDOW
    cb = lambda c, w: c // w
    row = lambda b, i: b * nb + i
    return pl.pallas_call(
        _attn_prompt_kernel,
        grid=(batch, nb),
        in_specs=[pl.BlockSpec(memory_space=pltpu.SMEM),
                  pl.BlockSpec((WINDOW, A_WIDTH), lambda b, i: (row(b, i), cb(COL_Q, A_WIDTH))),
                  pl.BlockSpec((WINDOW, A_WIDTH), lambda b, i: (row(b, i), cb(COL_GA, A_WIDTH))),
                  pl.BlockSpec((WINDOW, KV_WIDTH), lambda b, i: (row(b, i), cb(COL_K, KV_WIDTH))),
                  pl.BlockSpec((WINDOW, KV_WIDTH), lambda b, i: (row(b, i), cb(COL_V, KV_WIDTH))),
                  pl.BlockSpec((1, A_WIDTH), lambda b, i: (0, 0)),
                  pl.BlockSpec((1, KV_WIDTH), lambda b, i: (0, 0))],
        out_specs=[pl.BlockSpec((WINDOW, A_WIDTH), lambda b, i: (row(b, i), 0)),
                   pl.BlockSpec((1, WINDOW, KV_WIDTH), lambda b, i: (b, 0, 0)),
                   pl.BlockSpec((1, WINDOW, KV_WIDTH), lambda b, i: (b, 0, 0))],
        out_shape=[jax.ShapeDtypeStruct((batch * seq, A_WIDTH), BF16),
                   jax.ShapeDtypeStruct((batch, WINDOW, KV_WIDTH), F32),
                   jax.ShapeDtypeStruct((batch, WINDOW, KV_WIDTH), F32)],
        scratch_shapes=[pltpu.VMEM((WINDOW, KV_WIDTH), F32),
                        pltpu.VMEM((WINDOW, KV_WIDTH), F32)],
        compiler_params=pltpu.CompilerParams(
            dimension_semantics=("arbitrary", "arbitrary"), vmem_limit_bytes=VMEM_LIMIT),
        name="attn_prompt",
    )(sinks, proj, proj, proj, proj, qg, kg)


def _attn_sample_kernel(sink_ref, q_ref, ga_ref, k_ref, v_ref, ck_ref, cv_ref, qg_ref, kg_ref,
                        o_ref, nk_ref, nv_ref, *, nseq, tdec):
    k = _head_rms(k_ref[...], kg_ref[...])
    v = v_ref[...]
    qn = _head_rms(q_ref[...], qg_ref[...])
    ck = ck_ref[...]
    cv = cv_ref[...]
    k3 = k.reshape(nseq, tdec, KV_WIDTH)
    v3 = v.reshape(nseq, tdec, KV_WIDTH)
    kc = jnp.concatenate([ck, k3], axis=1)
    vc = jnp.concatenate([cv, v3], axis=1)
    nk_ref[...] = kc[:, tdec:, :]
    nv_ref[...] = vc[:, tdec:, :]
    kcb = kc.astype(BF16)
    vcb = vc.astype(BF16)

    rows = A_GROUP * tdec
    keys = WINDOW + tdec
    r = lax.broadcasted_iota(jnp.int32, (1, rows, keys), 1)
    j = lax.broadcasted_iota(jnp.int32, (1, rows, keys), 2)
    rel = WINDOW + (r & (tdec - 1)) - j
    ok = (rel >= 0) & (rel < WINDOW)
    rcol = lax.broadcasted_iota(jnp.int32, (1, rows, 1), 1)

    outs = []
    for n in range(A_KV_HEADS):
        heads = [A_GROUP * n + g for g in range(A_GROUP)]
        qs = jnp.concatenate(
            [qn[:, HEAD_DIM * h:HEAD_DIM * (h + 1)].reshape(nseq, tdec, HEAD_DIM) for h in heads],
            axis=1).astype(BF16)
        kn = kcb[:, :, HEAD_DIM * n:HEAD_DIM * (n + 1)]
        vn = vcb[:, :, HEAD_DIM * n:HEAD_DIM * (n + 1)]
        s = jnp.einsum("bqd,bkd->bqk", qs, kn, preferred_element_type=F32) * (HEAD_DIM ** -0.5)
        sink_col = jnp.full((1, rows, 1), sink_ref[heads[-1]], F32)
        for g in reversed(range(A_GROUP - 1)):
            sink_col = jnp.where(rcol < (g + 1) * tdec, sink_ref[heads[g]], sink_col)
        sm = jnp.where(ok, s, -1e30)
        m = jnp.maximum(jnp.max(sm, axis=-1, keepdims=True), sink_col)
        p = jnp.where(ok, jnp.exp(sm - m), 0.0)
        den = jnp.sum(p, axis=-1, keepdims=True) + jnp.exp(sink_col - m)
        o = jnp.einsum("bqk,bkd->bqd", p.astype(BF16), vn, preferred_element_type=F32) / den
        outs += [o[:, g * tdec:(g + 1) * tdec, :].reshape(nseq * tdec, HEAD_DIM)
                 for g in range(A_GROUP)]
    o_all = jnp.concatenate(outs, axis=1)
    o_ref[...] = (o_all * _silu(ga_ref[...])).astype(BF16)


def _attn_sample(proj, row0, cache_k, cache_v, sinks, qg, kg, nbatch, tdec, nseq=16):
    tm = nseq * tdec
    rb = row0 // tm
    cb = lambda c, w: c // w
    return pl.pallas_call(
        functools.partial(_attn_sample_kernel, nseq=nseq, tdec=tdec),
        grid=(nbatch // nseq,),
        in_specs=[pl.BlockSpec(memory_space=pltpu.SMEM),
                  pl.BlockSpec((tm, A_WIDTH), lambda s: (rb + s, cb(COL_Q, A_WIDTH))),
                  pl.BlockSpec((tm, A_WIDTH), lambda s: (rb + s, cb(COL_GA, A_WIDTH))),
                  pl.BlockSpec((tm, KV_WIDTH), lambda s: (rb + s, cb(COL_K, KV_WIDTH))),
                  pl.BlockSpec((tm, KV_WIDTH), lambda s: (rb + s, cb(COL_V, KV_WIDTH))),
                  pl.BlockSpec((nseq, WINDOW, KV_WIDTH), lambda s: (s, 0, 0)),
                  pl.BlockSpec((nseq, WINDOW, KV_WIDTH), lambda s: (s, 0, 0)),
                  pl.BlockSpec((1, A_WIDTH), lambda s: (0, 0)),
                  pl.BlockSpec((1, KV_WIDTH), lambda s: (0, 0))],
        out_specs=[pl.BlockSpec((tm, A_WIDTH), lambda s: (s, 0)),
                   pl.BlockSpec((nseq, WINDOW, KV_WIDTH), lambda s: (s, 0, 0)),
                   pl.BlockSpec((nseq, WINDOW, KV_WIDTH), lambda s: (s, 0, 0))],
        out_shape=[jax.ShapeDtypeStruct((nbatch * tdec, A_WIDTH), BF16),
                   jax.ShapeDtypeStruct((nbatch, WINDOW, KV_WIDTH), F32),
                   jax.ShapeDtypeStruct((nbatch, WINDOW, KV_WIDTH), F32)],
        compiler_params=pltpu.CompilerParams(
            dimension_semantics=("arbitrary",), vmem_limit_bytes=VMEM_LIMIT),
        name="attn_sample",
    )(sinks, proj, proj, proj, proj, cache_k, cache_v, qg, kg)


def _stack_heads(x):
    lo = _lane_half_mask()
    return jnp.concatenate([jnp.where(lo, x, 0.0), jnp.where(lo, 0.0, x)], axis=0)


def _unstack_heads(x):
    t = x.shape[0] // 2
    return x[:t] + x[t:]


def _rwkv_tile(xr, xk, xv, xw, gate, prm, states, blk, prec):
    nseq = TILE // blk
    lw = _dot(jnp.tanh(xw), prm["w2"], HI)
    la = _dot(xw, prm["a2"], HI)
    ld = (-math.exp(-0.5)) * _sigmoid(prm["w0"] + lw)
    a = _sigmoid(prm["a0"] + la)
    kkr = xk * prm["k_k"]
    kk = kkr / jnp.maximum(jnp.sqrt(_head_sums_wide(kkr * kkr)), 1e-12)
    kmod = xk * (1.0 + (a - 1.0) * prm["k_a"])

    lb = blk.bit_length() - 1
    rr = lax.broadcasted_iota(jnp.int32, (TILE, TILE), 0)
    cc = lax.broadcasted_iota(jnp.int32, (TILE, TILE), 1)
    same = (rr >> lb) == (cc >> lb)
    tri = jnp.where(same & (cc <= rr), 1.0, 0.0)
    ones = jnp.where(same, 1.0, 0.0)
    c = _dot(tri, ld, HI)
    cl = _dot(ones, ld, HI)
    rt = xr * jnp.exp(c)
    at = -kk * jnp.exp(c - ld)
    bt = kk * a * jnp.exp(-c)
    kt = kmod * jnp.exp(-c)
    bh = kk * a * jnp.exp(cl - c)
    kh = kmod * jnp.exp(cl - c)
    pl_rows = jnp.exp(cl)

    n2 = 2 * TILE
    r2 = lax.broadcasted_iota(jnp.int32, (n2, n2), 0)
    c2 = lax.broadcasted_iota(jnp.int32, (n2, n2), 1)
    same2 = (r2 >> lb) == (c2 >> lb)
    strict = same2 & (c2 < r2)
    incl = same2 & (c2 <= r2)
    eye = jnp.where(r2 == c2, 1.0, 0.0)

    lo = _lane_half_mask()
    y_pairs = []
    new_states = []
    for j in range(N_PAIRS):
        sl = slice(j * LANES, (j + 1) * LANES)
        at_s, rt_s = _stack_heads(at[:, sl]), _stack_heads(rt[:, sl])
        bt_s, kt_s = _stack_heads(bt[:, sl]), _stack_heads(kt[:, sl])
        v_s = _stack_heads(xv[:, sl])
        gram = _dot_nt(jnp.concatenate([at_s, rt_s], axis=0),
                       jnp.concatenate([bt_s, kt_s], axis=0), prec)
        a_ab = jnp.where(strict, gram[:n2, :n2], 0.0)
        a_ak = jnp.where(strict, gram[:n2, n2:], 0.0)
        a_rb = jnp.where(incl, gram[n2:, :n2], 0.0)
        a_rk = jnp.where(incl, gram[n2:, n2:], 0.0)

        tinv = eye
        for s in range(lb):
            lvl = (((r2 >> (s + 1)) == (c2 >> (s + 1)))
                   & (((r2 >> s) & 1) == 1) & (((c2 >> s) & 1) == 0))
            mid = jnp.where(lvl, a_ab, 0.0)
            tinv = tinv + _dot(_dot(tinv, mid, prec), tinv, prec)

        wu = _dot(tinv, jnp.concatenate([at_s, _dot(a_ak, v_s, prec)], axis=1), prec)
        w_nat = _unstack_heads(wu[:, :LANES])
        u0_nat = _unstack_heads(wu[:, LANES:])
        rt_nat = rt[:, sl]

        u_parts, rs_parts = [], []
        for q in range(nseq):
            rows = slice(q * blk, (q + 1) * blk)
            lhs = jnp.concatenate([w_nat[rows], rt_nat[rows]], axis=0)
            res = _dot_nt(lhs, states[j][q], prec)
            u_parts.append(res[:blk] + u0_nat[rows])
            rs_parts.append(res[blk:])
        u_nat = jnp.concatenate(u_parts, axis=0) if nseq > 1 else u_parts[0]
        rs_nat = jnp.concatenate(rs_parts, axis=0) if nseq > 1 else rs_parts[0]
        u_s = _stack_heads(u_nat)
        y_s = _dot(jnp.concatenate([a_rb, a_rk], axis=1),
                   jnp.concatenate([u_s, v_s], axis=0), prec)
        y_pairs.append(_unstack_heads(y_s) + rs_nat)

        bh_s, kh_s = _stack_heads(bh[:, sl]), _stack_heads(kh[:, sl])
        pair_states = []
        for q in range(nseq):
            idx = [slice(h * TILE + q * blk, h * TILE + (q + 1) * blk) for h in range(2)]
            lhs = jnp.concatenate([u_s[idx[0]], u_s[idx[1]], v_s[idx[0]], v_s[idx[1]]], axis=0)
            rhs = jnp.concatenate([bh_s[idx[0]], bh_s[idx[1]], kh_s[idx[0]], kh_s[idx[1]]], axis=0)
            decay_row = pl_rows[q * blk:q * blk + 1, sl]
            pair_states.append(states[j][q] * decay_row + _dot_tn(lhs, rhs, prec))
        new_states.append(pair_states)

    y = jnp.concatenate(y_pairs, axis=1)
    mean = _head_sums_wide(y) * (1.0 / HEAD_DIM)
    yc = y - mean
    var = _head_sums_wide(yc * yc) * (1.0 / HEAD_DIM)
    yn = yc * lax.rsqrt(var + GN_EPS) * prm["gn_w"] + prm["gn_b"]
    yn = yn + _head_sums_wide(xr * kmod * prm["r_k"]) * xv
    return (yn * _silu(gate)).astype(BF16), new_states


_RWKV_PARAM_NAMES = ("mu", "w0", "w2", "a0", "a2", "k_k", "k_a", "r_k", "gn_w", "gn_b")


def _rwkv_shift_mix(cur, prev_rows, first_row_mask, mu):
    shifted = jnp.where(first_row_mask, prev_rows, pltpu.roll(cur, 1, axis=0))
    return cur + (shifted - cur) * mu


def _rwkv_load_params(refs):
    prm = {n: r[...] for n, r in zip(_RWKV_PARAM_NAMES, refs)}
    mu = prm.pop("mu")
    return prm, (mu[:, 0:B_WIDTH], mu[:, B_WIDTH:2 * B_WIDTH],
                 mu[:, 2 * B_WIDTH:3 * B_WIDTH], mu[:, 3 * B_WIDTH:])


def _rwkv_prompt_kernel(r_ref, k_ref, v_ref, g_ref, w_ref, *rest, prec):
    prm_refs, (o_ref, s_out_ref, state, prev) = rest[:len(_RWKV_PARAM_NAMES)], rest[len(_RWKV_PARAM_NAMES):]
    i = pl.program_id(1)

    @pl.when(i == 0)
    def _():
        state[...] = jnp.zeros_like(state)
        prev[...] = jnp.zeros_like(prev)

    prm, mus = _rwkv_load_params(prm_refs)
    cur = (r_ref[...], k_ref[...], v_ref[...], w_ref[...])
    first = lax.broadcasted_iota(jnp.int32, (TILE, 1), 0) == 0
    offs = (0, B_WIDTH, 2 * B_WIDTH, 3 * B_WIDTH, SHIFT_PAD)
    mixed = []
    for n, x in enumerate(cur):
        prev_row = jnp.broadcast_to(prev[0:1, offs[n]:offs[n + 1]], x.shape)
        mixed.append(_rwkv_shift_mix(x, prev_row, first, mus[n]))
        prev[0:1, offs[n]:offs[n + 1]] = x[TILE - 1:TILE, :]
    states = [[state[j]] for j in range(N_PAIRS)]
    out, new_states = _rwkv_tile(*mixed, g_ref[...], prm, states, TILE, prec)
    o_ref[...] = out
    for j in range(N_PAIRS):
        s = new_states[j][0]
        state[j] = s
        s_out_ref[0, 2 * j] = s[:HEAD_DIM, :HEAD_DIM]
        s_out_ref[0, 2 * j + 1] = s[HEAD_DIM:, HEAD_DIM:]


def _rwkv_sample_kernel(r_ref, k_ref, v_ref, g_ref, w_ref, sh_ref, s_in_ref, *rest, prec, tdec):
    prm_refs, (o_ref, s_out_ref) = rest[:len(_RWKV_PARAM_NAMES)], rest[len(_RWKV_PARAM_NAMES):]
    nseq = TILE // tdec
    prm, mus = _rwkv_load_params(prm_refs)
    cur = (r_ref[...], k_ref[...], v_ref[...], w_ref[...])
    first = (lax.broadcasted_iota(jnp.int32, (TILE, 1), 0) & (tdec - 1)) == 0
    offs = (0, B_WIDTH, 2 * B_WIDTH, 3 * B_WIDTH, SHIFT_PAD)
    sh = sh_ref[...]
    mixed = []
    for n, x in enumerate(cur):
        prev_rows = jnp.concatenate(
            [jnp.broadcast_to(sh[q:q + 1, offs[n]:offs[n + 1]], (tdec, x.shape[1]))
             for q in range(nseq)], axis=0)
        mixed.append(_rwkv_shift_mix(x, prev_rows, first, mus[n]))
    zero = jnp.zeros((HEAD_DIM, HEAD_DIM), F32)
    states = []
    for j in range(N_PAIRS):
        pair = []
        for q in range(nseq):
            top = jnp.concatenate([s_in_ref[q, 2 * j], zero], axis=1)
            bot = jnp.concatenate([zero, s_in_ref[q, 2 * j + 1]], axis=1)
            pair.append(jnp.concatenate([top, bot], axis=0))
        states.append(pair)
    out, new_states = _rwkv_tile(*mixed, g_ref[...], prm, states, tdec, prec)
    o_ref[...] = out
    for j in range(N_PAIRS):
        for q in range(nseq):
            s = new_states[j][q]
            s_out_ref[q, 2 * j] = s[:HEAD_DIM, :HEAD_DIM]
            s_out_ref[q, 2 * j + 1] = s[HEAD_DIM:, HEAD_DIM:]


def _rwkv_param_specs(nargs):
    zero = (lambda *_: (0, 0))
    shapes = {"mu": (1, SHIFT_PAD), "w2": (LANES, B_WIDTH), "a2": (LANES, B_WIDTH)}
    return [pl.BlockSpec(shapes.get(n, (1, B_WIDTH)), zero) for n in _RWKV_PARAM_NAMES]


def _rwkv_col_specs(row_fn):
    cols = (COL_R, COL_KB, COL_VB, COL_GB)
    specs = [pl.BlockSpec((TILE, B_WIDTH), (lambda *a, c=c: (row_fn(*a), c // B_WIDTH))) for c in cols]
    specs.append(pl.BlockSpec((TILE, LANES), lambda *a: (row_fn(*a), COL_WLAL // LANES)))
    return specs


def _rwkv_prompt(proj, prm, batch, seq, prec):
    nt = seq // TILE
    row_fn = lambda b, i: b * nt + i
    return pl.pallas_call(
        functools.partial(_rwkv_prompt_kernel, prec=prec),
        grid=(batch, nt),
        in_specs=_rwkv_col_specs(row_fn) + _rwkv_param_specs(2),
        out_specs=[pl.BlockSpec((TILE, B_WIDTH), lambda b, i: (row_fn(b, i), 0)),
                   pl.BlockSpec((1, B_HEADS, HEAD_DIM, HEAD_DIM), lambda b, i: (b, 0, 0, 0))],
        out_shape=[jax.ShapeDtypeStruct((batch * seq, B_WIDTH), BF16),
                   jax.ShapeDtypeStruct((batch, B_HEADS, HEAD_DIM, HEAD_DIM), F32)],
        scratch_shapes=[pltpu.VMEM((N_PAIRS, LANES, LANES), F32),
                        pltpu.VMEM((8, SHIFT_PAD), F32)],
        compiler_params=pltpu.CompilerParams(
            dimension_semantics=("arbitrary", "arbitrary"), vmem_limit_bytes=VMEM_LIMIT),
        name="rwkv_prompt",
    )(proj, proj, proj, proj, proj, *[prm[n] for n in _RWKV_PARAM_NAMES])


def _rwkv_sample(proj, row0, shift_in, state_in, prm, nbatch, tdec, prec):
    nseq = TILE // tdec
    rb = row0 // TILE
    row_fn = lambda s: rb + s
    return pl.pallas_call(
        functools.partial(_rwkv_sample_kernel, prec=prec, tdec=tdec),
        grid=(nbatch // nseq,),
        in_specs=_rwkv_col_specs(row_fn)
        + [pl.BlockSpec((nseq, SHIFT_PAD), lambda s: (s, 0)),
           pl.BlockSpec((nseq, B_HEADS, HEAD_DIM, HEAD_DIM), lambda s: (s, 0, 0, 0))]
        + _rwkv_param_specs(1),
        out_specs=[pl.BlockSpec((TILE, B_WIDTH), lambda s: (s, 0)),
                   pl.BlockSpec((nseq, B_HEADS, HEAD_DIM, HEAD_DIM), lambda s: (s, 0, 0, 0))],
        out_shape=[jax.ShapeDtypeStruct((nbatch * tdec, B_WIDTH), BF16),
                   jax.ShapeDtypeStruct((nbatch, B_HEADS, HEAD_DIM, HEAD_DIM), F32)],
        compiler_params=pltpu.CompilerParams(
            dimension_semantics=("arbitrary",), vmem_limit_bytes=VMEM_LIMIT),
        name="rwkv_sample",
    )(proj, proj, proj, proj, proj, shift_in, state_in, *[prm[n] for n in _RWKV_PARAM_NAMES])


def _window_sums(xe):
    s2 = xe + pltpu.roll(xe, 1, axis=0)
    s4 = s2 + pltpu.roll(s2, 2, axis=0)
    s8 = s4 + pltpu.roll(s4, 4, axis=0)
    s16 = s8 + pltpu.roll(s8, 8, axis=0)
    lane = lax.broadcasted_iota(jnp.int32, (1, C_WIDTH), 1)
    return jnp.where(lane < 64, s2, jnp.where(lane < 128, s4, jnp.where(lane < 192, s8, s16)))


def _pool_window_lane():
    lane = lax.broadcasted_iota(jnp.int32, (1, C_WIDTH), 1)
    return jnp.where(lane < 64, 2, jnp.where(lane < 128, 4, jnp.where(lane < 192, 8, 16)))


def _pool_prompt_kernel(u_ref, halo_ref, g_ref, w_ref, sc_ref, o_ref, *, tm):
    i = pl.program_id(1)
    u = u_ref[...]
    halo = jnp.where(i > 0, halo_ref[...], 0.0)
    sums = _window_sums(jnp.concatenate([halo, u], axis=0))[16:]
    pos = i * tm + lax.broadcasted_iota(jnp.int32, (tm, 1), 0)
    cnt = jnp.minimum(_pool_window_lane(), pos + 1).astype(F32)
    d = sums / cnt - u
    y = _dot(d.astype(BF16), w_ref[...]) * sc_ref[...]
    o_ref[...] = (y * _silu(g_ref[...])).astype(BF16)


def _pool_prompt(proj, wbd, scale, batch, seq, tm=512):
    nt = seq // tm
    row = lambda b, i: b * nt + i
    halo_row = lambda b, i: jnp.maximum(b * (seq // 16) + i * (tm // 16) - 1, 0)
    return pl.pallas_call(
        functools.partial(_pool_prompt_kernel, tm=tm),
        grid=(batch, nt),
        in_specs=[pl.BlockSpec((tm, C_WIDTH), lambda b, i: (row(b, i), COL_UC // C_WIDTH)),
                  pl.BlockSpec((16, C_WIDTH), lambda b, i: (halo_row(b, i), COL_UC // C_WIDTH)),
                  pl.BlockSpec((tm, C_WIDTH), lambda b, i: (row(b, i), COL_GC // C_WIDTH)),
                  pl.BlockSpec((C_WIDTH, C_WIDTH), lambda b, i: (0, 0)),
                  pl.BlockSpec((1, C_WIDTH), lambda b, i: (0, 0))],
        out_specs=pl.BlockSpec((tm, C_WIDTH), lambda b, i: (row(b, i), 0)),
        out_shape=jax.ShapeDtypeStruct((batch * seq, C_WIDTH), BF16),
        compiler_params=pltpu.CompilerParams(
            dimension_semantics=("arbitrary", "arbitrary"), vmem_limit_bytes=VMEM_LIMIT),
        name="pool_prompt",
    )(proj, proj, proj, wbd, scale)


def _pool_sample_kernel(u_ref, h_ref, g_ref, w_ref, sc_ref, o_ref, *, nseq, tdec, pos0):
    u = u_ref[...]
    hist = h_ref[...]
    xe = jnp.concatenate([hist, u.reshape(nseq, tdec, C_WIDTH)], axis=1)
    xe = xe.reshape(nseq * (16 + tdec), C_WIDTH)
    sums = _window_sums(xe).reshape(nseq, 16 + tdec, C_WIDTH)[:, 16:, :]
    sums = sums.reshape(nseq * tdec, C_WIDTH)
    t = lax.broadcasted_iota(jnp.int32, (nseq * tdec, 1), 0) & (tdec - 1)
    cnt = jnp.minimum(_pool_window_lane(), pos0 + t + 1).astype(F32)
    d = sums / cnt - u
    y = _dot(d.astype(BF16), w_ref[...]) * sc_ref[...]
    o_ref[...] = (y * _silu(g_ref[...])).astype(BF16)


def _pool_sample(proj, row0, hist16, wbd, scale, nbatch, tdec, pos0, nseq=64):
    tm = nseq * tdec
    rb = row0 // tm
    return pl.pallas_call(
        functools.partial(_pool_sample_kernel, nseq=nseq, tdec=tdec, pos0=pos0),
        grid=(nbatch // nseq,),
        in_specs=[pl.BlockSpec((tm, C_WIDTH), lambda s: (rb + s, COL_UC // C_WIDTH)),
                  pl.BlockSpec((nseq, 16, C_WIDTH), lambda s: (s, 0, 0)),
                  pl.BlockSpec((tm, C_WIDTH), lambda s: (rb + s, COL_GC // C_WIDTH)),
                  pl.BlockSpec((C_WIDTH, C_WIDTH), lambda s: (0, 0)),
                  pl.BlockSpec((1, C_WIDTH), lambda s: (0, 0))],
        out_specs=pl.BlockSpec((tm, C_WIDTH), lambda s: (s, 0)),
        out_shape=jax.ShapeDtypeStruct((nbatch * tdec, C_WIDTH), BF16),
        compiler_params=pltpu.CompilerParams(
            dimension_semantics=("arbitrary",), vmem_limit_bytes=VMEM_LIMIT),
        name="pool_sample",
    )(proj, hist16, proj, wbd, scale)


def _permute_w_in(w):
    q, ka, va, ga = w[:, 0:384], w[:, 384:512], w[:, 512:640], w[:, 640:1024]
    pb, gb = w[:, 1024:2240], w[:, 2240:2624]
    uc, gc = w[:, 2624:2880], w[:, 2880:3136]
    pad = jnp.zeros((w.shape[0], LANES - 2 * LORA), w.dtype)
    return jnp.concatenate([q, ga, pb[:, :3 * B_WIDTH], gb, ka, va, uc, gc,
                            pb[:, 3 * B_WIDTH:], pad], axis=1)


def _pad_shift(x):
    pad = jnp.zeros(x.shape[:-1] + (LANES - 2 * LORA,), x.dtype)
    return jnp.concatenate([x, pad], axis=-1)


def _block_diag(w):
    g, c, _ = w.shape
    eye = jnp.eye(g, dtype=w.dtype)
    return (eye[:, None, :, None] * w[:, :, None, :]).reshape(g * c, g * c)


def _layer(x_all, l, p, n_prompt, batch, seq, nbatch, tdec, past_len, prec):
    proj = _inproj(x_all, p["norm_g"][l][None, :], p["w_in_p"][l])
    sinks = p["attn_sinks"][l]
    qg = jnp.tile(p["q_norm_g"][l], A_HEADS)[None, :]
    kg = jnp.tile(p["k_norm_g"][l], A_KV_HEADS)[None, :]

    oa_p, kl_p, vl_p = _attn_prompt(proj, sinks, qg, kg, batch, seq)
    oa_s, nk_s, nv_s = _attn_sample(proj, n_prompt, p["cache_k"][l], p["cache_v"][l],
                                    sinks, qg, kg, nbatch, tdec)

    lora_pad = jnp.zeros((LANES - LORA, B_WIDTH), F32)
    prm = {
        "mu": _pad_shift(p["shift_mu"][l])[None, :],
        "w0": p["decay_w0"][l][None, :],
        "w2": jnp.concatenate([p["decay_w2"][l], lora_pad], axis=0),
        "a0": p["iclr_a0"][l][None, :],
        "a2": jnp.concatenate([jnp.zeros((LORA, B_WIDTH), F32), p["iclr_a2"][l],
                               lora_pad[LORA:]], axis=0),
        "k_k": p["k_k"][l][None, :], "k_a": p["k_a"][l][None, :],
        "r_k": p["r_k"][l].reshape(1, B_WIDTH),
        "gn_w": p["gn_w"][l][None, :], "gn_b": p["gn_b"][l][None, :],
    }
    ob_p, wkv_p = _rwkv_prompt(proj, prm, batch, seq, prec)
    ob_s, wkv_s = _rwkv_sample(proj, n_prompt, _pad_shift(p["state_shift"][l]),
                               p["state_wkv"][l], prm, nbatch, tdec, prec)

    wbd = _block_diag(p["pool_w"][l]).astype(BF16)
    scale = p["pool_scale"][l][None, :]
    oc_p = _pool_prompt(proj, wbd, scale, batch, seq)
    hist16 = jnp.pad(p["state_pool"][l], ((0, 0), (1, 0), (0, 0)))
    oc_s = _pool_sample(proj, n_prompt, hist16, wbd, scale, nbatch, tdec, past_len)

    oa = jnp.concatenate([oa_p, oa_s], axis=0)
    ob = jnp.concatenate([ob_p, ob_s], axis=0)
    oc = jnp.concatenate([oc_p, oc_s], axis=0)
    x_new = _outproj(x_all, oa, ob, oc, p["w_out_b"][l])

    def shift_row(rows):
        return jnp.concatenate([rows[:, COL_R:COL_R + 3 * B_WIDTH],
                                rows[:, COL_WLAL:COL_WLAL + 2 * LORA]], axis=1)

    pp = proj[:n_prompt].reshape(batch, seq, IN_PAD)
    ps = proj[n_prompt:].reshape(nbatch, tdec, IN_PAD)
    shift_p = shift_row(pp[:, -1, :])
    shift_s = shift_row(ps[:, -1, :])
    pool_p = pp[:, seq - POOL_HIST:, COL_UC:COL_UC + C_WIDTH]
    pool_s = jnp.concatenate([p["state_pool"][l], ps[:, :, COL_UC:COL_UC + C_WIDTH]],
                             axis=1)[:, -POOL_HIST:]
    st_p = (kl_p.reshape(batch, WINDOW, A_KV_HEADS, HEAD_DIM),
            vl_p.reshape(batch, WINDOW, A_KV_HEADS, HEAD_DIM), wkv_p, shift_p, pool_p)
    st_s = (nk_s.reshape(nbatch, WINDOW, A_KV_HEADS, HEAD_DIM),
            nv_s.reshape(nbatch, WINDOW, A_KV_HEADS, HEAD_DIM), wkv_s, shift_s, pool_s)
    return x_new, st_p, st_s


def kernel(x_prompt, x_sample, cache_k, cache_v, state_wkv, state_shift, state_pool, norm_g, w_in, q_norm_g, k_norm_g, attn_sinks, shift_mu, decay_w0, decay_w2, iclr_a0, iclr_a2, k_k, k_a, r_k, gn_w, gn_b, pool_w, pool_scale, w_out):
    batch, seq, _ = x_prompt.shape
    nbatch, tdec, _ = x_sample.shape
    wbuf = cache_k.shape[2]
    past_len = PAST_LEN
    n_prompt = batch * seq
    p = dict(norm_g=norm_g, q_norm_g=q_norm_g, k_norm_g=k_norm_g, attn_sinks=attn_sinks,
             shift_mu=shift_mu, decay_w0=decay_w0, decay_w2=decay_w2, iclr_a0=iclr_a0,
             iclr_a2=iclr_a2, k_k=k_k, k_a=k_a, r_k=r_k, gn_w=gn_w, gn_b=gn_b,
             pool_w=pool_w, pool_scale=pool_scale, state_wkv=state_wkv,
             state_shift=state_shift, state_pool=state_pool,
             cache_k=cache_k.reshape(DEPTH, nbatch, wbuf, KV_WIDTH),
             cache_v=cache_v.reshape(DEPTH, nbatch, wbuf, KV_WIDTH),
             w_in_p=jax.vmap(_permute_w_in)(w_in).astype(BF16),
             w_out_b=w_out.astype(BF16))
    x_all = jnp.concatenate([x_prompt.reshape(n_prompt, D_MODEL),
                             x_sample.reshape(nbatch * tdec, D_MODEL)], axis=0)
    outs_p, outs_s = [], []
    for l in range(DEPTH):
        x_all, st_p, st_s = _layer(x_all, l, p, n_prompt, batch, seq, nbatch, tdec, past_len, HI)
        outs_p.append(st_p)
        outs_s.append(st_s)

    def stack(outs, i):
        return jnp.stack([o[i] for o in outs], axis=0)

    y_p = x_all[:n_prompt].reshape(batch, seq, D_MODEL)
    y_s = x_all[n_prompt:].reshape(nbatch, tdec, D_MODEL)
    return (y_p, y_s,
            stack(outs_p, 0), stack(outs_p, 1), stack(outs_p, 2), stack(outs_p, 3), stack(outs_p, 4),
            stack(outs_s, 0), stack(outs_s, 1), stack(outs_s, 2), stack(outs_s, 3), stack(outs_s, 4))
```

```python
import functools
import math

import jax
import jax.numpy as jnp
from jax import lax
from jax.experimental import pallas as pl
from jax.experimental.pallas import tpu as pltpu

F32 = jnp.float32
BF16 = jnp.bfloat16

D_MODEL = 1024
DEPTH = 4
HEAD_DIM = 64
A_HEADS = 6
A_KV_HEADS = 2
A_GROUP = A_HEADS // A_KV_HEADS
A_WIDTH = A_HEADS * HEAD_DIM
KV_WIDTH = A_KV_HEADS * HEAD_DIM
WINDOW = 128
B_HEADS = 6
B_WIDTH = B_HEADS * HEAD_DIM
LORA = 32
GN_EPS = 6.4e-4
SHIFT_WIDTH = 3 * B_WIDTH + 2 * LORA
C_WIDTH = 256
POOL_HIST = 15
NORM_EPS = 1e-6
PAST_LEN = 8192

LANES = 128
TILE = 64
RWKV_PROMPT_ROWS = 256
N_PAIRS = B_HEADS // 2

COL_Q, COL_GA, COL_R, COL_KB, COL_VB, COL_GB = 0, 384, 768, 1152, 1536, 1920
COL_K, COL_V, COL_UC, COL_GC, COL_WLAL = 2304, 2432, 2560, 2816, 3072
IN_PAD = 3200
SHIFT_PAD = 3 * B_WIDTH + LANES

VMEM_LIMIT = 48 * 1024 * 1024


def _dot(a, b, prec=None):
    return jnp.dot(a, b, preferred_element_type=F32, precision=prec)


def _dot_nt(a, b, prec=None):
    return lax.dot_general(a, b, (((1,), (1,)), ((), ())),
                           preferred_element_type=F32, precision=prec)


def _dot_tn(a, b, prec=None):
    return lax.dot_general(a, b, (((0,), (0,)), ((), ())),
                           preferred_element_type=F32, precision=prec)


def _sigmoid(x):
    return 1.0 / (1.0 + jnp.exp(-x))


def _silu(x):
    return x * _sigmoid(x)


def _lane_half_mask(rows=1):
    lane = lax.broadcasted_iota(jnp.int32, (rows, LANES), 1)
    return lane < HEAD_DIM


def _head_sums(x):
    lo = _lane_half_mask()
    s0 = jnp.sum(jnp.where(lo, x, 0.0), axis=-1, keepdims=True)
    s1 = jnp.sum(jnp.where(lo, 0.0, x), axis=-1, keepdims=True)
    return jnp.where(lo, s0, s1)


def _head_sums_wide(x):
    n = x.shape[1] // LANES
    return jnp.concatenate(
        [_head_sums(x[:, j * LANES:(j + 1) * LANES]) for j in range(n)], axis=1)


def _head_rms(x, g):
    ms = _head_sums_wide(x * x) * (1.0 / HEAD_DIM)
    return x * lax.rsqrt(ms + NORM_EPS) * g


def _inproj_kernel(x_ref, g_ref, w_ref, o_ref):
    x = x_ref[...]
    ms = jnp.mean(x * x, axis=-1, keepdims=True)
    h = x * lax.rsqrt(ms + NORM_EPS) * g_ref[...]
    o_ref[...] = _dot(h.astype(BF16), w_ref[...])


def _inproj(x, g, w, tm=256):
    n = x.shape[0]
    return pl.pallas_call(
        _inproj_kernel,
        grid=(n // tm,),
        in_specs=[pl.BlockSpec((tm, D_MODEL), lambda i: (i, 0)),
                  pl.BlockSpec((1, D_MODEL), lambda i: (0, 0)),
                  pl.BlockSpec((D_MODEL, IN_PAD), lambda i: (0, 0))],
        out_specs=pl.BlockSpec((tm, IN_PAD), lambda i: (i, 0)),
        out_shape=jax.ShapeDtypeStruct((n, IN_PAD), F32),
        compiler_params=pltpu.CompilerParams(
            dimension_semantics=("arbitrary",), vmem_limit_bytes=VMEM_LIMIT),
        name="inproj",
    )(x, g, w)


def _outproj_kernel(x_ref, oa_ref, ob_ref, oc_ref, wa_ref, wb_ref, wc_ref, o_ref):
    acc = _dot(oa_ref[...], wa_ref[...])
    acc += _dot(ob_ref[...], wb_ref[...])
    acc += _dot(oc_ref[...], wc_ref[...])
    o_ref[...] = x_ref[...] + acc


def _outproj(x, oa, ob, oc, w, tm=512):
    n = x.shape[0]
    return pl.pallas_call(
        _outproj_kernel,
        grid=(n // tm,),
        in_specs=[pl.BlockSpec((tm, D_MODEL), lambda i: (i, 0)),
                  pl.BlockSpec((tm, A_WIDTH), lambda i: (i, 0)),
                  pl.BlockSpec((tm, B_WIDTH), lambda i: (i, 0)),
                  pl.BlockSpec((tm, C_WIDTH), lambda i: (i, 0)),
                  pl.BlockSpec((A_WIDTH, D_MODEL), lambda i: (0, 0)),
                  pl.BlockSpec((B_WIDTH, D_MODEL), lambda i: (1, 0)),
                  pl.BlockSpec((C_WIDTH, D_MODEL), lambda i: (3, 0))],
        out_specs=pl.BlockSpec((tm, D_MODEL), lambda i: (i, 0)),
        out_shape=jax.ShapeDtypeStruct((n, D_MODEL), F32),
        compiler_params=pltpu.CompilerParams(
            dimension_semantics=("arbitrary",), vmem_limit_bytes=VMEM_LIMIT),
        name="outproj",
    )(x, oa, ob, oc, w, w, w)


def _sink_softmax_pv(s, ok, sink_col, v_bf16):
    sm = jnp.where(ok, s, -1e30)
    m = jnp.maximum(jnp.max(sm, axis=-1, keepdims=True), sink_col)
    p = jnp.where(ok, jnp.exp(sm - m), 0.0)
    den = jnp.sum(p, axis=-1, keepdims=True) + jnp.exp(sink_col - m)
    return _dot(p.astype(BF16), v_bf16) / den


def _attn_prompt_kernel(sink_ref, q_ref, ga_ref, k_ref, v_ref, qg_ref, kg_ref,
                        o_ref, kl_ref, vl_ref, kprev, vprev):
    i = pl.program_id(1)

    @pl.when(i == 0)
    def _():
        kprev[...] = jnp.zeros_like(kprev)
        vprev[...] = jnp.zeros_like(vprev)

    k = _head_rms(k_ref[...], kg_ref[...])
    v = v_ref[...]
    qn = _head_rms(q_ref[...], qg_ref[...])
    kc = jnp.concatenate([kprev[...], k], axis=0).astype(BF16)
    vc = jnp.concatenate([vprev[...], v], axis=0).astype(BF16)

    rows = A_GROUP * WINDOW
    r = lax.broadcasted_iota(jnp.int32, (rows, 2 * WINDOW), 0)
    j = lax.broadcasted_iota(jnp.int32, (rows, 2 * WINDOW), 1)
    rel = (r & (WINDOW - 1)) + WINDOW - j
    ok = (rel >= 0) & (rel < WINDOW) & ((j >= WINDOW) | (i > 0))
    rcol = lax.broadcasted_iota(jnp.int32, (rows, 1), 0)

    outs = []
    for n in range(A_KV_HEADS):
        heads = [A_GROUP * n + g for g in range(A_GROUP)]
        qs = jnp.concatenate(
            [qn[:, HEAD_DIM * h:HEAD_DIM * (h + 1)] for h in heads], axis=0)
        s = _dot_nt(qs.astype(BF16), kc[:, HEAD_DIM * n:HEAD_DIM * (n + 1)])
        s = s * (HEAD_DIM ** -0.5)
        sink_col = jnp.full((rows, 1), sink_ref[heads[-1]], F32)
        for g in reversed(range(A_GROUP - 1)):
            sink_col = jnp.where(rcol < (g + 1) * WINDOW, sink_ref[heads[g]], sink_col)
        o = _sink_softmax_pv(s, ok, sink_col, vc[:, HEAD_DIM * n:HEAD_DIM * (n + 1)])
        outs += [o[g * WINDOW:(g + 1) * WINDOW] for g in range(A_GROUP)]
    o_all = jnp.concatenate(outs, axis=1)
    o_ref[...] = (o_all * _silu(ga_ref[...])).astype(BF16)

    kprev[...] = k
    vprev[...] = v
    kl_ref[0] = k
    vl_ref[0] = v


def _attn_prompt(proj, sinks, qg, kg, batch, seq):
    nb = seq // WINDOW
    cb = lambda c, w: c // w
    row = lambda b, i: b * nb + i
    return pl.pallas_call(
        _attn_prompt_kernel,
        grid=(batch, nb),
        in_specs=[pl.BlockSpec(memory_space=pltpu.SMEM),
                  pl.BlockSpec((WINDOW, A_WIDTH), lambda b, i: (row(b, i), cb(COL_Q, A_WIDTH))),
                  pl.BlockSpec((WINDOW, A_WIDTH), lambda b, i: (row(b, i), cb(COL_GA, A_WIDTH))),
                  pl.BlockSpec((WINDOW, KV_WIDTH), lambda b, i: (row(b, i), cb(COL_K, KV_WIDTH))),
                  pl.BlockSpec((WINDOW, KV_WIDTH), lambda b, i: (row(b, i), cb(COL_V, KV_WIDTH))),
                  pl.BlockSpec((1, A_WIDTH), lambda b, i: (0, 0)),
                  pl.BlockSpec((1, KV_WIDTH), lambda b, i: (0, 0))],
        out_specs=[pl.BlockSpec((WINDOW, A_WIDTH), lambda b, i: (row(b, i), 0)),
                   pl.BlockSpec((1, WINDOW, KV_WIDTH), lambda b, i: (b, 0, 0)),
                   pl.BlockSpec((1, WINDOW, KV_WIDTH), lambda b, i: (b, 0, 0))],
        out_shape=[jax.ShapeDtypeStruct((batch * seq, A_WIDTH), BF16),
                   jax.ShapeDtypeStruct((batch, WINDOW, KV_WIDTH), F32),
                   jax.ShapeDtypeStruct((batch, WINDOW, KV_WIDTH), F32)],
        scratch_shapes=[pltpu.VMEM((WINDOW, KV_WIDTH), F32),
                        pltpu.VMEM((WINDOW, KV_WIDTH), F32)],
        compiler_params=pltpu.CompilerParams(
            dimension_semantics=("arbitrary", "arbitrary"), vmem_limit_bytes=VMEM_LIMIT),
        name="attn_prompt",
    )(sinks, proj, proj, proj, proj, qg, kg)


def _attn_sample_kernel(sink_ref, q_ref, ga_ref, k_ref, v_ref, ck_ref, cv_ref, qg_ref, kg_ref,
                        o_ref, nk_ref, nv_ref, *, nseq, tdec):
    k = _head_rms(k_ref[...], kg_ref[...])
    v = v_ref[...]
    qn = _head_rms(q_ref[...], qg_ref[...])
    ck = ck_ref[...]
    cv = cv_ref[...]
    k3 = k.reshape(nseq, tdec, KV_WIDTH)
    v3 = v.reshape(nseq, tdec, KV_WIDTH)
    kc = jnp.concatenate([ck, k3], axis=1)
    vc = jnp.concatenate([cv, v3], axis=1)
    nk_ref[...] = kc[:, tdec:, :]
    nv_ref[...] = vc[:, tdec:, :]
    kcb = kc.astype(BF16)
    vcb = vc.astype(BF16)

    rows = A_GROUP * tdec
    keys = WINDOW + tdec
    r = lax.broadcasted_iota(jnp.int32, (1, rows, keys), 1)
    j = lax.broadcasted_iota(jnp.int32, (1, rows, keys), 2)
    rel = WINDOW + (r & (tdec - 1)) - j
    ok = (rel >= 0) & (rel < WINDOW)
    rcol = lax.broadcasted_iota(jnp.int32, (1, rows, 1), 1)

    outs = []
    for n in range(A_KV_HEADS):
        heads = [A_GROUP * n + g for g in range(A_GROUP)]
        qs = jnp.concatenate(
            [qn[:, HEAD_DIM * h:HEAD_DIM * (h + 1)].reshape(nseq, tdec, HEAD_DIM) for h in heads],
            axis=1).astype(BF16)
        kn = kcb[:, :, HEAD_DIM * n:HEAD_DIM * (n + 1)]
        vn = vcb[:, :, HEAD_DIM * n:HEAD_DIM * (n + 1)]
        s = jnp.einsum("bqd,bkd->bqk", qs, kn, preferred_element_type=F32) * (HEAD_DIM ** -0.5)
        sink_col = jnp.full((1, rows, 1), sink_ref[heads[-1]], F32)
        for g in reversed(range(A_GROUP - 1)):
            sink_col = jnp.where(rcol < (g + 1) * tdec, sink_ref[heads[g]], sink_col)
        sm = jnp.where(ok, s, -1e30)
        m = jnp.maximum(jnp.max(sm, axis=-1, keepdims=True), sink_col)
        p = jnp.where(ok, jnp.exp(sm - m), 0.0)
        den = jnp.sum(p, axis=-1, keepdims=True) + jnp.exp(sink_col - m)
        o = jnp.einsum("bqk,bkd->bqd", p.astype(BF16), vn, preferred_element_type=F32) / den
        outs += [o[:, g * tdec:(g + 1) * tdec, :].reshape(nseq * tdec, HEAD_DIM)
                 for g in range(A_GROUP)]
    o_all = jnp.concatenate(outs, axis=1)
    o_ref[...] = (o_all * _silu(ga_ref[...])).astype(BF16)


def _attn_sample(proj, row0, cache_k, cache_v, sinks, qg, kg, nbatch, tdec, nseq=16):
    tm = nseq * tdec
    rb = row0 // tm
    cb = lambda c, w: c // w
    return pl.pallas_call(
        functools.partial(_attn_sample_kernel, nseq=nseq, tdec=tdec),
        grid=(nbatch // nseq,),
        in_specs=[pl.BlockSpec(memory_space=pltpu.SMEM),
                  pl.BlockSpec((tm, A_WIDTH), lambda s: (rb + s, cb(COL_Q, A_WIDTH))),
                  pl.BlockSpec((tm, A_WIDTH), lambda s: (rb + s, cb(COL_GA, A_WIDTH))),
                  pl.BlockSpec((tm, KV_WIDTH), lambda s: (rb + s, cb(COL_K, KV_WIDTH))),
                  pl.BlockSpec((tm, KV_WIDTH), lambda s: (rb + s, cb(COL_V, KV_WIDTH))),
                  pl.BlockSpec((nseq, WINDOW, KV_WIDTH), lambda s: (s, 0, 0)),
                  pl.BlockSpec((nseq, WINDOW, KV_WIDTH), lambda s: (s, 0, 0)),
                  pl.BlockSpec((1, A_WIDTH), lambda s: (0, 0)),
                  pl.BlockSpec((1, KV_WIDTH), lambda s: (0, 0))],
        out_specs=[pl.BlockSpec((tm, A_WIDTH), lambda s: (s, 0)),
                   pl.BlockSpec((nseq, WINDOW, KV_WIDTH), lambda s: (s, 0, 0)),
                   pl.BlockSpec((nseq, WINDOW, KV_WIDTH), lambda s: (s, 0, 0))],
        out_shape=[jax.ShapeDtypeStruct((nbatch * tdec, A_WIDTH), BF16),
                   jax.ShapeDtypeStruct((nbatch, WINDOW, KV_WIDTH), F32),
                   jax.ShapeDtypeStruct((nbatch, WINDOW, KV_WIDTH), F32)],
        compiler_params=pltpu.CompilerParams(
            dimension_semantics=("arbitrary",), vmem_limit_bytes=VMEM_LIMIT),
        name="attn_sample",
    )(sinks, proj, proj, proj, proj, cache_k, cache_v, qg, kg)


def _stack_heads(x):
    lo = jnp.where(_lane_half_mask(), 1.0, 0.0).astype(x.dtype)
    return jnp.concatenate([x * lo, x * (1 - lo)], axis=0)


def _cat_rows(parts, rows):
    if (rows.stop - rows.start) % 16 == 0:
        return jnp.concatenate([x[rows] for x in parts], axis=0)
    return jnp.concatenate([x.astype(F32)[rows] for x in parts], axis=0).astype(BF16)


def _split3(x):
    hi = x.astype(BF16)
    r1 = x - hi.astype(F32)
    mid = r1.astype(BF16)
    lo = (r1 - mid.astype(F32)).astype(BF16)
    return hi, mid, lo


def _rwkv_masks(blk):
    lb = blk.bit_length() - 1
    t = lax.broadcasted_iota(jnp.int32, (TILE, LANES), 0)
    s = lax.broadcasted_iota(jnp.int32, (TILE, LANES), 1) & (TILE - 1)
    same = (t >> lb) == (s >> lb)
    levels = [((t >> (l + 1)) == (s >> (l + 1))) & (((t >> l) & 1) == 1) & (((s >> l) & 1) == 0)
              for l in range(lb)]
    tt = lax.broadcasted_iota(jnp.int32, (2 * TILE, 3 * TILE), 0)
    ss = lax.broadcasted_iota(jnp.int32, (2 * TILE, 3 * TILE), 1)
    ti = tt & (TILE - 1)
    si = jnp.where(ss >= 2 * TILE, ss - 2 * TILE, jnp.where(ss >= TILE, ss - TILE, ss))
    sel = ((ti >> lb) == (si >> lb)) & ((tt >= TILE) | (si <= ti))
    r2 = lax.broadcasted_iota(jnp.int32, (LANES, LANES), 0)
    c2 = lax.broadcasted_iota(jnp.int32, (LANES, LANES), 1)
    return dict(lb=lb, strict=same & (s < t), incl=same & (s <= t),
                eye=jnp.where(s == t, 1.0, 0.0), levels=levels,
                cumsel=jnp.where(sel, 1.0, 0.0).astype(BF16),
                same_head=(r2 >> 6) == (c2 >> 6))


def _rwkv_tile(xr, xk, xv, xw, gate, prm, states, blk):
    nchunk = xr.shape[0] // TILE
    nseq = TILE // blk
    mk = _rwkv_masks(blk)
    lw = _dot(jnp.tanh(xw).astype(BF16), prm["w2"])
    la = _dot(xw.astype(BF16), prm["a2"])
    ld = (-math.exp(-0.5)) * _sigmoid(prm["w0"] + lw)
    a = _sigmoid(prm["a0"] + la)
    kkr = xk * prm["k_k"]
    kk = kkr / jnp.maximum(jnp.sqrt(_head_sums_wide(kkr * kkr)), 1e-12)
    kmod = xk * (1.0 + (a - 1.0) * prm["k_a"])
    ka = kk * a

    probs = []
    for ch in range(nchunk):
        rows = slice(ch * TILE, (ch + 1) * TILE)
        cum = _dot(mk["cumsel"], jnp.concatenate(_split3(ld[rows]), axis=0))
        c, cl = cum[:TILE], cum[TILE:]
        e_c = jnp.exp(-c)
        e_l = jnp.exp(cl - c)
        full = dict(rt=(xr[rows] * jnp.exp(c)).astype(BF16),
                    at=(-kk[rows] * jnp.exp(c - ld[rows])).astype(BF16),
                    bt=(ka[rows] * e_c).astype(BF16), kt=(kmod[rows] * e_c).astype(BF16),
                    bh=(ka[rows] * e_l).astype(BF16), kh=(kmod[rows] * e_l).astype(BF16),
                    v=xv[rows].astype(BF16), decay=jnp.exp(cl))
        for j in range(N_PAIRS):
            sl = slice(j * LANES, (j + 1) * LANES)
            probs.append(dict(ch=ch, j=j, **{n: x[:, sl] for n, x in full.items()}))

    for p in probs:
        p["v_s"] = _stack_heads(p["v"])
        z_s = jnp.concatenate([_stack_heads(p["bt"]), _stack_heads(p["kt"])], axis=0)
        gram = _dot_nt(jnp.concatenate([p["at"], p["rt"]], axis=0), z_s)
        p["a_ab"] = jnp.where(mk["strict"], gram[:TILE, :LANES], 0.0)
        p["a_ak"] = jnp.where(mk["strict"], gram[:TILE, LANES:], 0.0).astype(BF16)
        p["a_r"] = jnp.concatenate([jnp.where(mk["incl"], gram[TILE:, :LANES], 0.0),
                                    jnp.where(mk["incl"], gram[TILE:, LANES:], 0.0)],
                                   axis=1).astype(BF16)
        p["tinv"] = mk["eye"]
        if mk["lb"] > 0:
            p["tinv"] = p["tinv"] + jnp.where(mk["levels"][0], p["a_ab"], 0.0)

    for l in range(1, mk["lb"]):
        for p in probs:
            p["tb"] = p["tinv"].astype(BF16)
            mid = _stack_heads(jnp.where(mk["levels"][l], p["a_ab"], 0.0).astype(BF16))
            p["half"] = _dot(p["tb"], mid).astype(BF16)
        for p in probs:
            p["tinv"] = p["tinv"] + _dot(p["half"], _stack_heads(p["tb"]))

    for p in probs:
        p["akv"] = _dot(p["a_ak"], p["v_s"]).astype(BF16)
    for p in probs:
        wu = _dot(p["tinv"].astype(BF16),
                  jnp.concatenate([_stack_heads(p["at"]), _stack_heads(p["akv"])], axis=1))
        p["w"], p["u0"] = wu[:, :LANES].astype(BF16), wu[:, LANES:]

    states = [list(st) for st in states]
    y_rows = []
    for ch in range(nchunk):
        cps = [p for p in probs if p["ch"] == ch]
        for p in cps:
            u_parts, rs_parts = [], []
            for q in range(nseq):
                rows = slice(q * blk, (q + 1) * blk)
                lhs = _cat_rows([p["w"], p["rt"]], rows)
                res = _dot_nt(lhs, states[p["j"]][q].astype(BF16))
                u_parts.append(res[:blk] + p["u0"][rows])
                rs_parts.append(res[blk:])
            p["u"] = (jnp.concatenate(u_parts, axis=0) if nseq > 1 else u_parts[0]).astype(BF16)
            p["rs"] = jnp.concatenate(rs_parts, axis=0) if nseq > 1 else rs_parts[0]
        for p in cps:
            for q in range(nseq):
                rows = slice(q * blk, (q + 1) * blk)
                upd = _dot_tn(_cat_rows([p["u"], p["v"]], rows),
                              _cat_rows([p["bh"], p["kh"]], rows))
                states[p["j"]][q] = (states[p["j"]][q] * p["decay"][q * blk:q * blk + 1]
                                     + jnp.where(mk["same_head"], upd, 0.0))
        y_rows.append(jnp.concatenate(
            [_dot(p["a_r"], jnp.concatenate([_stack_heads(p["u"]), p["v_s"]], axis=0)) + p["rs"]
             for p in cps], axis=1))
    y = jnp.concatenate(y_rows, axis=0) if nchunk > 1 else y_rows[0]

    mean = _head_sums_wide(y) * (1.0 / HEAD_DIM)
    yc = y - mean
    var = _head_sums_wide(yc * yc) * (1.0 / HEAD_DIM)
    yn = yc * lax.rsqrt(var + GN_EPS) * prm["gn_w"] + prm["gn_b"]
    yn = yn + _head_sums_wide(xr * kmod * prm["r_k"]) * xv
    return (yn * _silu(gate)).astype(BF16), states


_RWKV_PARAM_NAMES = ("mu", "w0", "w2", "a0", "a2", "k_k", "k_a", "r_k", "gn_w", "gn_b")


def _rwkv_shift_mix(cur, prev_rows, first_row_mask, mu):
    shifted = jnp.where(first_row_mask, prev_rows, pltpu.roll(cur, 1, axis=0))
    return cur + (shifted - cur) * mu


def _rwkv_load_params(refs):
    prm = {n: r[...] for n, r in zip(_RWKV_PARAM_NAMES, refs)}
    mu = prm.pop("mu")
    return prm, (mu[:, 0:B_WIDTH], mu[:, B_WIDTH:2 * B_WIDTH],
                 mu[:, 2 * B_WIDTH:3 * B_WIDTH], mu[:, 3 * B_WIDTH:])


def _rwkv_prompt_kernel(r_ref, k_ref, v_ref, g_ref, w_ref, *rest):
    prm_refs, (o_ref, s_out_ref, state, prev) = rest[:len(_RWKV_PARAM_NAMES)], rest[len(_RWKV_PARAM_NAMES):]
    i = pl.program_id(1)

    @pl.when(i == 0)
    def _():
        state[...] = jnp.zeros_like(state)
        prev[...] = jnp.zeros_like(prev)

    prm, mus = _rwkv_load_params(prm_refs)
    cur = (r_ref[...], k_ref[...], v_ref[...], w_ref[...])
    rows = cur[0].shape[0]
    first = lax.broadcasted_iota(jnp.int32, (rows, 1), 0) == 0
    offs = (0, B_WIDTH, 2 * B_WIDTH, 3 * B_WIDTH, SHIFT_PAD)
    mixed = []
    for n, x in enumerate(cur):
        prev_row = jnp.broadcast_to(prev[0:1, offs[n]:offs[n + 1]], x.shape)
        mixed.append(_rwkv_shift_mix(x, prev_row, first, mus[n]))
        prev[0:1, offs[n]:offs[n + 1]] = x[rows - 1:rows, :]
    states = [[state[j]] for j in range(N_PAIRS)]
    out, new_states = _rwkv_tile(*mixed, g_ref[...], prm, states, TILE)
    o_ref[...] = out
    for j in range(N_PAIRS):
        s = new_states[j][0]
        state[j] = s
        s_out_ref[0, 2 * j] = s[:HEAD_DIM, :HEAD_DIM]
        s_out_ref[0, 2 * j + 1] = s[HEAD_DIM:, HEAD_DIM:]


def _rwkv_sample_kernel(r_ref, k_ref, v_ref, g_ref, w_ref, sh_ref, s_in_ref, *rest, tdec):
    prm_refs, (o_ref, s_out_ref) = rest[:len(_RWKV_PARAM_NAMES)], rest[len(_RWKV_PARAM_NAMES):]
    nseq = TILE // tdec
    prm, mus = _rwkv_load_params(prm_refs)
    cur = (r_ref[...], k_ref[...], v_ref[...], w_ref[...])
    first = (lax.broadcasted_iota(jnp.int32, (TILE, 1), 0) & (tdec - 1)) == 0
    offs = (0, B_WIDTH, 2 * B_WIDTH, 3 * B_WIDTH, SHIFT_PAD)
    sh = sh_ref[...]
    mixed = []
    for n, x in enumerate(cur):
        prev_rows = jnp.concatenate(
            [jnp.broadcast_to(sh[q:q + 1, offs[n]:offs[n + 1]], (tdec, x.shape[1]))
             for q in range(nseq)], axis=0)
        mixed.append(_rwkv_shift_mix(x, prev_rows, first, mus[n]))
    zero = jnp.zeros((HEAD_DIM, HEAD_DIM), F32)
    states = []
    for j in range(N_PAIRS):
        pair = []
        for q in range(nseq):
            top = jnp.concatenate([s_in_ref[q, 2 * j], zero], axis=1)
            bot = jnp.concatenate([zero, s_in_ref[q, 2 * j + 1]], axis=1)
            pair.append(jnp.concatenate([top, bot], axis=0))
        states.append(pair)
    out, new_states = _rwkv_tile(*mixed, g_ref[...], prm, states, tdec)
    o_ref[...] = out
    for j in range(N_PAIRS):
        for q in range(nseq):
            s = new_states[j][q]
            s_out_ref[q, 2 * j] = s[:HEAD_DIM, :HEAD_DIM]
            s_out_ref[q, 2 * j + 1] = s[HEAD_DIM:, HEAD_DIM:]


def _rwkv_param_specs():
    zero = (lambda *_: (0, 0))
    shapes = {"mu": (1, SHIFT_PAD), "w2": (LANES, B_WIDTH), "a2": (LANES, B_WIDTH)}
    return [pl.BlockSpec(shapes.get(n, (1, B_WIDTH)), zero) for n in _RWKV_PARAM_NAMES]


def _rwkv_col_specs(row_fn, rows):
    cols = (COL_R, COL_KB, COL_VB, COL_GB)
    specs = [pl.BlockSpec((rows, B_WIDTH), (lambda *a, c=c: (row_fn(*a), c // B_WIDTH))) for c in cols]
    specs.append(pl.BlockSpec((rows, LANES), lambda *a: (row_fn(*a), COL_WLAL // LANES)))
    return specs


def _rwkv_prompt(proj, prm, batch, seq, rows=RWKV_PROMPT_ROWS):
    nt = seq // rows
    row_fn = lambda b, i: b * nt + i
    return pl.pallas_call(
        _rwkv_prompt_kernel,
        grid=(batch, nt),
        in_specs=_rwkv_col_specs(row_fn, rows) + _rwkv_param_specs(),
        out_specs=[pl.BlockSpec((rows, B_WIDTH), lambda b, i: (row_fn(b, i), 0)),
                   pl.BlockSpec((1, B_HEADS, HEAD_DIM, HEAD_DIM), lambda b, i: (b, 0, 0, 0))],
        out_shape=[jax.ShapeDtypeStruct((batch * seq, B_WIDTH), BF16),
                   jax.ShapeDtypeStruct((batch, B_HEADS, HEAD_DIM, HEAD_DIM), F32)],
        scratch_shapes=[pltpu.VMEM((N_PAIRS, LANES, LANES), F32),
                        pltpu.VMEM((8, SHIFT_PAD), F32)],
        compiler_params=pltpu.CompilerParams(
            dimension_semantics=("arbitrary", "arbitrary"), vmem_limit_bytes=VMEM_LIMIT),
        name="rwkv_prompt",
    )(proj, proj, proj, proj, proj, *[prm[n] for n in _RWKV_PARAM_NAMES])


def _rwkv_sample(proj, row0, shift_in, state_in, prm, nbatch, tdec):
    nseq = TILE // tdec
    rb = row0 // TILE
    row_fn = lambda s: rb + s
    return pl.pallas_call(
        functools.partial(_rwkv_sample_kernel, tdec=tdec),
        grid=(nbatch // nseq,),
        in_specs=_rwkv_col_specs(row_fn, TILE)
        + [pl.BlockSpec((nseq, SHIFT_PAD), lambda s: (s, 0)),
           pl.BlockSpec((nseq, B_HEADS, HEAD_DIM, HEAD_DIM), lambda s: (s, 0, 0, 0))]
        + _rwkv_param_specs(),
        out_specs=[pl.BlockSpec((TILE, B_WIDTH), lambda s: (s, 0)),
                   pl.BlockSpec((nseq, B_HEADS, HEAD_DIM, HEAD_DIM), lambda s: (s, 0, 0, 0))],
        out_shape=[jax.ShapeDtypeStruct((nbatch * tdec, B_WIDTH), BF16),
                   jax.ShapeDtypeStruct((nbatch, B_HEADS, HEAD_DIM, HEAD_DIM), F32)],
        compiler_params=pltpu.CompilerParams(
            dimension_semantics=("arbitrary",), vmem_limit_bytes=VMEM_LIMIT),
        name="rwkv_sample",
    )(proj, proj, proj, proj, proj, shift_in, state_in, *[prm[n] for n in _RWKV_PARAM_NAMES])


def _window_sums(xe):
    s2 = xe + pltpu.roll(xe, 1, axis=0)
    s4 = s2 + pltpu.roll(s2, 2, axis=0)
    s8 = s4 + pltpu.roll(s4, 4, axis=0)
    s16 = s8 + pltpu.roll(s8, 8, axis=0)
    lane = lax.broadcasted_iota(jnp.int32, (1, C_WIDTH), 1)
    return jnp.where(lane < 64, s2, jnp.where(lane < 128, s4, jnp.where(lane < 192, s8, s16)))


def _pool_window_lane():
    lane = lax.broadcasted_iota(jnp.int32, (1, C_WIDTH), 1)
    return jnp.where(lane < 64, 2, jnp.where(lane < 128, 4, jnp.where(lane < 192, 8, 16)))


def _pool_prompt_kernel(u_ref, halo_ref, g_ref, w_ref, sc_ref, o_ref, *, tm):
    i = pl.program_id(1)
    u = u_ref[...]
    halo = jnp.where(i > 0, halo_ref[...], 0.0)
    sums = _window_sums(jnp.concatenate([halo, u], axis=0))[16:]
    pos = i * tm + lax.broadcasted_iota(jnp.int32, (tm, 1), 0)
    cnt = jnp.minimum(_pool_window_lane(), pos + 1).astype(F32)
    d = sums / cnt - u
    y = _dot(d.astype(BF16), w_ref[...]) * sc_ref[...]
    o_ref[...] = (y * _silu(g_ref[...])).astype(BF16)


def _pool_prompt(proj, wbd, scale, batch, seq, tm=512):
    nt = seq // tm
    row = lambda b, i: b * nt + i
    halo_row = lambda b, i: jnp.maximum(b * (seq // 16) + i * (tm // 16) - 1, 0)
    return pl.pallas_call(
        functools.partial(_pool_prompt_kernel, tm=tm),
        grid=(batch, nt),
        in_specs=[pl.BlockSpec((tm, C_WIDTH), lambda b, i: (row(b, i), COL_UC // C_WIDTH)),
                  pl.BlockSpec((16, C_WIDTH), lambda b, i: (halo_row(b, i), COL_UC // C_WIDTH)),
                  pl.BlockSpec((tm, C_WIDTH), lambda b, i: (row(b, i), COL_GC // C_WIDTH)),
                  pl.BlockSpec((C_WIDTH, C_WIDTH), lambda b, i: (0, 0)),
                  pl.BlockSpec((1, C_WIDTH), lambda b, i: (0, 0))],
        out_specs=pl.BlockSpec((tm, C_WIDTH), lambda b, i: (row(b, i), 0)),
        out_shape=jax.ShapeDtypeStruct((batch * seq, C_WIDTH), BF16),
        compiler_params=pltpu.CompilerParams(
            dimension_semantics=("arbitrary", "arbitrary"), vmem_limit_bytes=VMEM_LIMIT),
        name="pool_prompt",
    )(proj, proj, proj, wbd, scale)


def _pool_sample_kernel(u_ref, h_ref, g_ref, w_ref, sc_ref, o_ref, *, nseq, tdec, pos0):
    u = u_ref[...]
    hist = h_ref[...]
    xe = jnp.concatenate([hist, u.reshape(nseq, tdec, C_WIDTH)], axis=1)
    xe = xe.reshape(nseq * (16 + tdec), C_WIDTH)
    sums = _window_sums(xe).reshape(nseq, 16 + tdec, C_WIDTH)[:, 16:, :]
    sums = sums.reshape(nseq * tdec, C_WIDTH)
    t = lax.broadcasted_iota(jnp.int32, (nseq * tdec, 1), 0) & (tdec - 1)
    cnt = jnp.minimum(_pool_window_lane(), pos0 + t + 1).astype(F32)
    d = sums / cnt - u
    y = _dot(d.astype(BF16), w_ref[...]) * sc_ref[...]
    o_ref[...] = (y * _silu(g_ref[...])).astype(BF16)


def _pool_sample(proj, row0, hist16, wbd, scale, nbatch, tdec, pos0, nseq=64):
    tm = nseq * tdec
    rb = row0 // tm
    return pl.pallas_call(
        functools.partial(_pool_sample_kernel, nseq=nseq, tdec=tdec, pos0=pos0),
        grid=(nbatch // nseq,),
        in_specs=[pl.BlockSpec((tm, C_WIDTH), lambda s: (rb + s, COL_UC // C_WIDTH)),
                  pl.BlockSpec((nseq, 16, C_WIDTH), lambda s: (s, 0, 0)),
                  pl.BlockSpec((tm, C_WIDTH), lambda s: (rb + s, COL_GC // C_WIDTH)),
                  pl.BlockSpec((C_WIDTH, C_WIDTH), lambda s: (0, 0)),
                  pl.BlockSpec((1, C_WIDTH), lambda s: (0, 0))],
        out_specs=pl.BlockSpec((tm, C_WIDTH), lambda s: (s, 0)),
        out_shape=jax.ShapeDtypeStruct((nbatch * tdec, C_WIDTH), BF16),
        compiler_params=pltpu.CompilerParams(
            dimension_semantics=("arbitrary",), vmem_limit_bytes=VMEM_LIMIT),
        name="pool_sample",
    )(proj, hist16, proj, wbd, scale)


def _permute_w_in(w):
    q, ka, va, ga = w[:, 0:384], w[:, 384:512], w[:, 512:640], w[:, 640:1024]
    pb, gb = w[:, 1024:2240], w[:, 2240:2624]
    uc, gc = w[:, 2624:2880], w[:, 2880:3136]
    pad = jnp.zeros((w.shape[0], LANES - 2 * LORA), w.dtype)
    return jnp.concatenate([q, ga, pb[:, :3 * B_WIDTH], gb, ka, va, uc, gc,
                            pb[:, 3 * B_WIDTH:], pad], axis=1)


def _pad_shift(x):
    pad = jnp.zeros(x.shape[:-1] + (LANES - 2 * LORA,), x.dtype)
    return jnp.concatenate([x, pad], axis=-1)


def _block_diag(w):
    g, c, _ = w.shape
    eye = jnp.eye(g, dtype=w.dtype)
    return (eye[:, None, :, None] * w[:, :, None, :]).reshape(g * c, g * c)


def _layer(x_all, l, p, n_prompt, batch, seq, nbatch, tdec, past_len):
    proj = _inproj(x_all, p["norm_g"][l][None, :], p["w_in_p"][l])
    sinks = p["attn_sinks"][l]
    qg = jnp.tile(p["q_norm_g"][l], A_HEADS)[None, :]
    kg = jnp.tile(p["k_norm_g"][l], A_KV_HEADS)[None, :]

    oa_p, kl_p, vl_p = _attn_prompt(proj, sinks, qg, kg, batch, seq)
    oa_s, nk_s, nv_s = _attn_sample(proj, n_prompt, p["cache_k"][l], p["cache_v"][l],
                                    sinks, qg, kg, nbatch, tdec)

    lora_pad = jnp.zeros((LANES - LORA, B_WIDTH), F32)
    prm = {
        "mu": _pad_shift(p["shift_mu"][l])[None, :],
        "w0": p["decay_w0"][l][None, :],
        "w2": jnp.concatenate([p["decay_w2"][l], lora_pad], axis=0).astype(BF16),
        "a0": p["iclr_a0"][l][None, :],
        "a2": jnp.concatenate([jnp.zeros((LORA, B_WIDTH), F32), p["iclr_a2"][l],
                               lora_pad[LORA:]], axis=0).astype(BF16),
        "k_k": p["k_k"][l][None, :], "k_a": p["k_a"][l][None, :],
        "r_k": p["r_k"][l].reshape(1, B_WIDTH),
        "gn_w": p["gn_w"][l][None, :], "gn_b": p["gn_b"][l][None, :],
    }
    ob_p, wkv_p = _rwkv_prompt(proj, prm, batch, seq)
    ob_s, wkv_s = _rwkv_sample(proj, n_prompt, _pad_shift(p["state_shift"][l]),
                               p["state_wkv"][l], prm, nbatch, tdec)

    wbd = _block_diag(p["pool_w"][l]).astype(BF16)
    scale = p["pool_scale"][l][None, :]
    oc_p = _pool_prompt(proj, wbd, scale, batch, seq)
    hist16 = jnp.pad(p["state_pool"][l], ((0, 0), (1, 0), (0, 0)))
    oc_s = _pool_sample(proj, n_prompt, hist16, wbd, scale, nbatch, tdec, past_len)

    oa = jnp.concatenate([oa_p, oa_s], axis=0)
    ob = jnp.concatenate([ob_p, ob_s], axis=0)
    oc = jnp.concatenate([oc_p, oc_s], axis=0)
    x_new = _outproj(x_all, oa, ob, oc, p["w_out_b"][l])

    def shift_row(rows):
        return jnp.concatenate([rows[:, COL_R:COL_R + 3 * B_WIDTH],
                                rows[:, COL_WLAL:COL_WLAL + 2 * LORA]], axis=1)

    last_p = jnp.stack([proj[(b + 1) * seq - 1] for b in range(batch)], axis=0)
    ps = proj[n_prompt:].reshape(nbatch, tdec, IN_PAD)
    shift_p = shift_row(last_p)
    shift_s = shift_row(ps[:, -1, :])
    pool_p = jnp.stack([proj[(b + 1) * seq - POOL_HIST:(b + 1) * seq, COL_UC:COL_UC + C_WIDTH]
                        for b in range(batch)], axis=0)
    pool_s = jnp.concatenate([p["state_pool"][l], ps[:, :, COL_UC:COL_UC + C_WIDTH]],
                             axis=1)[:, -POOL_HIST:]
    st_p = (kl_p.reshape(batch, WINDOW, A_KV_HEADS, HEAD_DIM),
            vl_p.reshape(batch, WINDOW, A_KV_HEADS, HEAD_DIM), wkv_p, shift_p, pool_p)
    st_s = (nk_s.reshape(nbatch, WINDOW, A_KV_HEADS, HEAD_DIM),
            nv_s.reshape(nbatch, WINDOW, A_KV_HEADS, HEAD_DIM), wkv_s, shift_s, pool_s)
    return x_new, st_p, st_s


def kernel(x_prompt, x_sample, cache_k, cache_v, state_wkv, state_shift, state_pool, norm_g, w_in, q_norm_g, k_norm_g, attn_sinks, shift_mu, decay_w0, decay_w2, iclr_a0, iclr_a2, k_k, k_a, r_k, gn_w, gn_b, pool_w, pool_scale, w_out):
    batch, seq, _ = x_prompt.shape
    nbatch, tdec, _ = x_sample.shape
    wbuf = cache_k.shape[2]
    past_len = PAST_LEN
    n_prompt = batch * seq
    p = dict(norm_g=norm_g, q_norm_g=q_norm_g, k_norm_g=k_norm_g, attn_sinks=attn_sinks,
             shift_mu=shift_mu, decay_w0=decay_w0, decay_w2=decay_w2, iclr_a0=iclr_a0,
             iclr_a2=iclr_a2, k_k=k_k, k_a=k_a, r_k=r_k, gn_w=gn_w, gn_b=gn_b,
             pool_w=pool_w, pool_scale=pool_scale, state_wkv=state_wkv,
             state_shift=state_shift, state_pool=state_pool,
             cache_k=cache_k.reshape(DEPTH, nbatch, wbuf, KV_WIDTH),
             cache_v=cache_v.reshape(DEPTH, nbatch, wbuf, KV_WIDTH),
             w_in_p=jax.vmap(_permute_w_in)(w_in).astype(BF16),
             w_out_b=w_out.astype(BF16))
    x_all = jnp.concatenate([x_prompt.reshape(n_prompt, D_MODEL),
                             x_sample.reshape(nbatch * tdec, D_MODEL)], axis=0)
    outs_p, outs_s = [], []
    for l in range(DEPTH):
        x_all, st_p, st_s = _layer(x_all, l, p, n_prompt, batch, seq, nbatch, tdec, past_len)
        outs_p.append(st_p)
        outs_s.append(st_s)

    def stack(outs, i):
        return jnp.stack([o[i] for o in outs], axis=0)

    y_p = x_all[:n_prompt].reshape(batch, seq, D_MODEL)
    y_s = x_all[n_prompt:].reshape(nbatch, tdec, D_MODEL)
    return (y_p, y_s,
            stack(outs_p, 0), stack(outs_p, 1), stack(outs_p, 2), stack(outs_p, 3), stack(outs_p, 4),
            stack(outs_s, 0), stack(outs_s, 1), stack(outs_s, 2), stack(outs_s, 3), stack(outs_s, 4))
```

```python
import functools
import math

import jax
import jax.numpy as jnp
from jax import lax
from jax.experimental import pallas as pl
from jax.experimental.pallas import tpu as pltpu

F32 = jnp.float32
BF16 = jnp.bfloat16

D_MODEL = 1024
DEPTH = 4
HEAD_DIM = 64
A_HEADS = 6
A_KV_HEADS = 2
A_GROUP = A_HEADS // A_KV_HEADS
A_WIDTH = A_HEADS * HEAD_DIM
KV_WIDTH = A_KV_HEADS * HEAD_DIM
WINDOW = 128
B_HEADS = 6
B_WIDTH = B_HEADS * HEAD_DIM
LORA = 32
GN_EPS = 6.4e-4
SHIFT_WIDTH = 3 * B_WIDTH + 2 * LORA
C_WIDTH = 256
POOL_HIST = 15
NORM_EPS = 1e-6
PAST_LEN = 8192

LANES = 128
TILE = 64
RWKV_PROMPT_ROWS = 256
N_PAIRS = B_HEADS // 2

COL_Q, COL_GA, COL_R, COL_KB, COL_VB, COL_GB = 0, 384, 768, 1152, 1536, 1920
COL_K, COL_V, COL_UC, COL_GC, COL_WLAL = 2304, 2432, 2560, 2816, 3072
IN_PAD = 3200
SHIFT_PAD = 3 * B_WIDTH + LANES

VMEM_LIMIT = 48 * 1024 * 1024


def _dot(a, b, prec=None):
    return jnp.dot(a, b, preferred_element_type=F32, precision=prec)


def _dot_nt(a, b, prec=None):
    return lax.dot_general(a, b, (((1,), (1,)), ((), ())),
                           preferred_element_type=F32, precision=prec)


def _dot_tn(a, b, prec=None):
    return lax.dot_general(a, b, (((0,), (0,)), ((), ())),
                           preferred_element_type=F32, precision=prec)


def _sigmoid(x):
    return 1.0 / (1.0 + jnp.exp(-x))


def _silu(x):
    return x * _sigmoid(x)


def _lane_half_mask(rows=1):
    lane = lax.broadcasted_iota(jnp.int32, (rows, LANES), 1)
    return lane < HEAD_DIM


def _head_sums(x):
    lo = _lane_half_mask()
    s0 = jnp.sum(jnp.where(lo, x, 0.0), axis=-1, keepdims=True)
    s1 = jnp.sum(jnp.where(lo, 0.0, x), axis=-1, keepdims=True)
    return jnp.where(lo, s0, s1)


def _head_sums_wide(x):
    n = x.shape[1] // LANES
    return jnp.concatenate(
        [_head_sums(x[:, j * LANES:(j + 1) * LANES]) for j in range(n)], axis=1)


def _head_rms(x, g):
    ms = _head_sums_wide(x * x) * (1.0 / HEAD_DIM)
    return x * lax.rsqrt(ms + NORM_EPS) * g


def _inproj_kernel(x_ref, g_ref, w_ref, o_ref):
    x = x_ref[...]
    ms = jnp.mean(x * x, axis=-1, keepdims=True)
    h = x * lax.rsqrt(ms + NORM_EPS) * g_ref[...]
    o_ref[...] = _dot(h.astype(BF16), w_ref[...])


def _skip(n, fn):
    def wrapped(*refs, **kw):
        return fn(*refs[n:], **kw)
    return wrapped


_CARRIER = pl.BlockSpec(memory_space=pl.ANY)


def _params(ndims):
    return pltpu.CompilerParams(dimension_semantics=("arbitrary",) * ndims,
                                vmem_limit_bytes=VMEM_LIMIT)


def _inproj(x, g_all, w_all, l, n_total, row0, carrier=None, tm=256):
    n = x.shape[0]
    rb = row0 // tm
    car = [] if carrier is None else [carrier]
    return pl.pallas_call(
        _skip(len(car), _inproj_kernel),
        grid=(n // tm,),
        in_specs=[_CARRIER] * len(car)
        + [pl.BlockSpec((tm, D_MODEL), lambda i: (i, 0)),
           pl.BlockSpec((None, 1, D_MODEL), lambda i: (l, 0, 0)),
           pl.BlockSpec((None, D_MODEL, IN_PAD), lambda i: (l, 0, 0))],
        out_specs=pl.BlockSpec((tm, IN_PAD), lambda i: (rb + i, 0)),
        out_shape=jax.ShapeDtypeStruct((n_total, IN_PAD), F32),
        input_output_aliases={0: 0} if car else {},
        compiler_params=_params(1),
        name="inproj",
    )(*car, x, g_all, w_all)


def _outproj_kernel(x_ref, oa_ref, ob_ref, oc_ref, wa_ref, wb_ref, wc_ref, o_ref):
    acc = _dot(oa_ref[...], wa_ref[...])
    acc += _dot(ob_ref[...], wb_ref[...])
    acc += _dot(oc_ref[...], wc_ref[...])
    o_ref[...] = x_ref[...] + acc


def _outproj(x, oa, ob, oc, w_all, l, row0, tm=512):
    n = x.shape[0]
    rb = row0 // tm
    return pl.pallas_call(
        _outproj_kernel,
        grid=(n // tm,),
        in_specs=[pl.BlockSpec((tm, D_MODEL), lambda i: (i, 0)),
                  pl.BlockSpec((tm, A_WIDTH), lambda i: (rb + i, 0)),
                  pl.BlockSpec((tm, B_WIDTH), lambda i: (rb + i, 0)),
                  pl.BlockSpec((tm, C_WIDTH), lambda i: (rb + i, 0)),
                  pl.BlockSpec((None, A_WIDTH, D_MODEL), lambda i: (l, 0, 0)),
                  pl.BlockSpec((None, B_WIDTH, D_MODEL), lambda i: (l, 1, 0)),
                  pl.BlockSpec((None, C_WIDTH, D_MODEL), lambda i: (l, 3, 0))],
        out_specs=pl.BlockSpec((tm, D_MODEL), lambda i: (i, 0)),
        out_shape=jax.ShapeDtypeStruct((n, D_MODEL), F32),
        compiler_params=_params(1),
        name="outproj",
    )(x, oa, ob, oc, w_all, w_all, w_all)


def _sink_softmax_pv(s, ok, sink_col, v_bf16):
    sm = jnp.where(ok, s, -1e30)
    m = jnp.maximum(jnp.max(sm, axis=-1, keepdims=True), sink_col)
    p = jnp.where(ok, jnp.exp(sm - m), 0.0)
    den = jnp.sum(p, axis=-1, keepdims=True) + jnp.exp(sink_col - m)
    return _dot(p.astype(BF16), v_bf16) / den


def _attn_prompt_kernel(sink_ref, q_ref, ga_ref, k_ref, v_ref, qg_ref, kg_ref,
                        o_ref, kl_ref, vl_ref, kprev, vprev, *, layer):
    i = pl.program_id(1)

    @pl.when(i == 0)
    def _():
        kprev[...] = jnp.zeros_like(kprev)
        vprev[...] = jnp.zeros_like(vprev)

    k = _head_rms(k_ref[...], kg_ref[...])
    v = v_ref[...]
    qn = _head_rms(q_ref[...], qg_ref[...])
    kc = jnp.concatenate([kprev[...], k], axis=0).astype(BF16)
    vc = jnp.concatenate([vprev[...], v], axis=0).astype(BF16)

    rows = A_GROUP * WINDOW
    r = lax.broadcasted_iota(jnp.int32, (rows, 2 * WINDOW), 0)
    j = lax.broadcasted_iota(jnp.int32, (rows, 2 * WINDOW), 1)
    rel = (r & (WINDOW - 1)) + WINDOW - j
    ok = (rel >= 0) & (rel < WINDOW) & ((j >= WINDOW) | (i > 0))
    rcol = lax.broadcasted_iota(jnp.int32, (rows, 1), 0)

    outs = []
    for n in range(A_KV_HEADS):
        heads = [A_GROUP * n + g for g in range(A_GROUP)]
        qs = jnp.concatenate(
            [qn[:, HEAD_DIM * h:HEAD_DIM * (h + 1)] for h in heads], axis=0)
        s = _dot_nt(qs.astype(BF16), kc[:, HEAD_DIM * n:HEAD_DIM * (n + 1)])
        s = s * (HEAD_DIM ** -0.5)
        sink_col = jnp.full((rows, 1), sink_ref[layer, heads[-1]], F32)
        for g in reversed(range(A_GROUP - 1)):
            sink_col = jnp.where(rcol < (g + 1) * WINDOW, sink_ref[layer, heads[g]], sink_col)
        o = _sink_softmax_pv(s, ok, sink_col, vc[:, HEAD_DIM * n:HEAD_DIM * (n + 1)])
        outs += [o[g * WINDOW:(g + 1) * WINDOW] for g in range(A_GROUP)]
    o_all = jnp.concatenate(outs, axis=1)
    o_ref[...] = (o_all * _silu(ga_ref[...])).astype(BF16)

    kprev[...] = k
    vprev[...] = v
    kl_ref[...] = k
    vl_ref[...] = v


def _attn_prompt(proj, sinks, qg_all, kg_all, l, carriers, batch, seq):
    nb = seq // WINDOW
    n_total = proj.shape[0]
    cb = lambda c, w: c // w
    row = lambda b, i: b * nb + i
    ncar = len(carriers)
    return pl.pallas_call(
        _skip(ncar, functools.partial(_attn_prompt_kernel, layer=l)),
        grid=(batch, nb),
        in_specs=[_CARRIER] * ncar
        + [pl.BlockSpec(memory_space=pltpu.SMEM),
           pl.BlockSpec((WINDOW, A_WIDTH), lambda b, i: (row(b, i), cb(COL_Q, A_WIDTH))),
           pl.BlockSpec((WINDOW, A_WIDTH), lambda b, i: (row(b, i), cb(COL_GA, A_WIDTH))),
           pl.BlockSpec((WINDOW, KV_WIDTH), lambda b, i: (row(b, i), cb(COL_K, KV_WIDTH))),
           pl.BlockSpec((WINDOW, KV_WIDTH), lambda b, i: (row(b, i), cb(COL_V, KV_WIDTH))),
           pl.BlockSpec((None, 1, A_WIDTH), lambda b, i: (l, 0, 0)),
           pl.BlockSpec((None, 1, KV_WIDTH), lambda b, i: (l, 0, 0))],
        out_specs=[pl.BlockSpec((WINDOW, A_WIDTH), lambda b, i: (row(b, i), 0)),
                   pl.BlockSpec((None, None, WINDOW, KV_WIDTH), lambda b, i: (l, b, 0, 0)),
                   pl.BlockSpec((None, None, WINDOW, KV_WIDTH), lambda b, i: (l, b, 0, 0))],
        out_shape=[jax.ShapeDtypeStruct((n_total, A_WIDTH), BF16),
                   jax.ShapeDtypeStruct((DEPTH, batch, WINDOW, KV_WIDTH), F32),
                   jax.ShapeDtypeStruct((DEPTH, batch, WINDOW, KV_WIDTH), F32)],
        scratch_shapes=[pltpu.VMEM((WINDOW, KV_WIDTH), F32),
                        pltpu.VMEM((WINDOW, KV_WIDTH), F32)],
        input_output_aliases={c: 1 + c for c in range(ncar)},
        compiler_params=_params(2),
        name="attn_prompt",
    )(*carriers, sinks, proj, proj, proj, proj, qg_all, kg_all)


def _attn_sample_kernel(sink_ref, q_ref, ga_ref, k_ref, v_ref, ck_ref, cv_ref, qg_ref, kg_ref,
                        o_ref, nk_ref, nv_ref, *, nseq, tdec, layer):
    k = _head_rms(k_ref[...], kg_ref[...])
    v = v_ref[...]
    qn = _head_rms(q_ref[...], qg_ref[...])
    ck = ck_ref[...]
    cv = cv_ref[...]
    k3 = k.reshape(nseq, tdec, KV_WIDTH)
    v3 = v.reshape(nseq, tdec, KV_WIDTH)
    kc = jnp.concatenate([ck, k3], axis=1)
    vc = jnp.concatenate([cv, v3], axis=1)
    nk_ref[...] = kc[:, tdec:, :]
    nv_ref[...] = vc[:, tdec:, :]
    kcb = kc.astype(BF16)
    vcb = vc.astype(BF16)

    rows = A_GROUP * tdec
    keys = WINDOW + tdec
    r = lax.broadcasted_iota(jnp.int32, (1, rows, keys), 1)
    j = lax.broadcasted_iota(jnp.int32, (1, rows, keys), 2)
    rel = WINDOW + (r & (tdec - 1)) - j
    ok = (rel >= 0) & (rel < WINDOW)
    rcol = lax.broadcasted_iota(jnp.int32, (1, rows, 1), 1)

    outs = []
    for n in range(A_KV_HEADS):
        heads = [A_GROUP * n + g for g in range(A_GROUP)]
        qs = jnp.concatenate(
            [qn[:, HEAD_DIM * h:HEAD_DIM * (h + 1)].reshape(nseq, tdec, HEAD_DIM) for h in heads],
            axis=1).astype(BF16)
        kn = kcb[:, :, HEAD_DIM * n:HEAD_DIM * (n + 1)]
        vn = vcb[:, :, HEAD_DIM * n:HEAD_DIM * (n + 1)]
        s = jnp.einsum("bqd,bkd->bqk", qs, kn, preferred_element_type=F32) * (HEAD_DIM ** -0.5)
        sink_col = jnp.full((1, rows, 1), sink_ref[layer, heads[-1]], F32)
        for g in reversed(range(A_GROUP - 1)):
            sink_col = jnp.where(rcol < (g + 1) * tdec, sink_ref[layer, heads[g]], sink_col)
        sm = jnp.where(ok, s, -1e30)
        m = jnp.maximum(jnp.max(sm, axis=-1, keepdims=True), sink_col)
        p = jnp.where(ok, jnp.exp(sm - m), 0.0)
        den = jnp.sum(p, axis=-1, keepdims=True) + jnp.exp(sink_col - m)
        o = jnp.einsum("bqk,bkd->bqd", p.astype(BF16), vn, preferred_element_type=F32) / den
        outs += [o[:, g * tdec:(g + 1) * tdec, :].reshape(nseq * tdec, HEAD_DIM)
                 for g in range(A_GROUP)]
    o_all = jnp.concatenate(outs, axis=1)
    o_ref[...] = (o_all * _silu(ga_ref[...])).astype(BF16)


def _attn_sample(proj, row0, oa, cache_k, cache_v, sinks, qg_all, kg_all, l, carriers,
                 nbatch, tdec, nseq=16):
    tm = nseq * tdec
    rb = row0 // tm
    cb = lambda c, w: c // w
    car = [oa] + list(carriers)
    cache_spec = pl.BlockSpec((None, nseq, WINDOW, KV_WIDTH), lambda s: (l, s, 0, 0))
    return pl.pallas_call(
        _skip(len(car), functools.partial(_attn_sample_kernel, nseq=nseq, tdec=tdec, layer=l)),
        grid=(nbatch // nseq,),
        in_specs=[_CARRIER] * len(car)
        + [pl.BlockSpec(memory_space=pltpu.SMEM),
           pl.BlockSpec((tm, A_WIDTH), lambda s: (rb + s, cb(COL_Q, A_WIDTH))),
           pl.BlockSpec((tm, A_WIDTH), lambda s: (rb + s, cb(COL_GA, A_WIDTH))),
           pl.BlockSpec((tm, KV_WIDTH), lambda s: (rb + s, cb(COL_K, KV_WIDTH))),
           pl.BlockSpec((tm, KV_WIDTH), lambda s: (rb + s, cb(COL_V, KV_WIDTH))),
           cache_spec, cache_spec,
           pl.BlockSpec((None, 1, A_WIDTH), lambda s: (l, 0, 0)),
           pl.BlockSpec((None, 1, KV_WIDTH), lambda s: (l, 0, 0))],
        out_specs=[pl.BlockSpec((tm, A_WIDTH), lambda s: (rb + s, 0)), cache_spec, cache_spec],
        out_shape=[jax.ShapeDtypeStruct(oa.shape, BF16),
                   jax.ShapeDtypeStruct((DEPTH, nbatch, WINDOW, KV_WIDTH), F32),
                   jax.ShapeDtypeStruct((DEPTH, nbatch, WINDOW, KV_WIDTH), F32)],
        input_output_aliases={c: c for c in range(len(car))},
        compiler_params=_params(1),
        name="attn_sample",
    )(*car, sinks, proj, proj, proj, proj, cache_k, cache_v, qg_all, kg_all)


def _stack_heads(x):
    lo = jnp.where(_lane_half_mask(), 1.0, 0.0).astype(x.dtype)
    return jnp.concatenate([x * lo, x * (1 - lo)], axis=0)


def _cat_rows(parts, rows):
    if (rows.stop - rows.start) % 16 == 0:
        return jnp.concatenate([x[rows] for x in parts], axis=0)
    return jnp.concatenate([x.astype(F32)[rows] for x in parts], axis=0).astype(BF16)


def _split3(x):
    hi = x.astype(BF16)
    r1 = x - hi.astype(F32)
    mid = r1.astype(BF16)
    lo = (r1 - mid.astype(F32)).astype(BF16)
    return hi, mid, lo


def _rwkv_masks(blk):
    lb = blk.bit_length() - 1
    t = lax.broadcasted_iota(jnp.int32, (TILE, LANES), 0)
    s = lax.broadcasted_iota(jnp.int32, (TILE, LANES), 1) & (TILE - 1)
    same = (t >> lb) == (s >> lb)
    levels = [((t >> (l + 1)) == (s >> (l + 1))) & (((t >> l) & 1) == 1) & (((s >> l) & 1) == 0)
              for l in range(lb)]
    tt = lax.broadcasted_iota(jnp.int32, (2 * TILE, 3 * TILE), 0)
    ss = lax.broadcasted_iota(jnp.int32, (2 * TILE, 3 * TILE), 1)
    ti = tt & (TILE - 1)
    si = jnp.where(ss >= 2 * TILE, ss - 2 * TILE, jnp.where(ss >= TILE, ss - TILE, ss))
    sel = ((ti >> lb) == (si >> lb)) & ((tt >= TILE) | (si <= ti))
    r2 = lax.broadcasted_iota(jnp.int32, (LANES, LANES), 0)
    c2 = lax.broadcasted_iota(jnp.int32, (LANES, LANES), 1)
    return dict(lb=lb, strict=same & (s < t), incl=same & (s <= t),
                eye=jnp.where(s == t, 1.0, 0.0), levels=levels,
                cumsel=jnp.where(sel, 1.0, 0.0).astype(BF16),
                same_head=(r2 >> 6) == (c2 >> 6))


def _rwkv_tile(xr, xk, xv, xw, gate, prm, states, blk):
    nchunk = xr.shape[0] // TILE
    nseq = TILE // blk
    mk = _rwkv_masks(blk)
    lw = _dot(jnp.tanh(xw).astype(BF16), prm["w2"])
    la = _dot(xw.astype(BF16), prm["a2"])
    ld = (-math.exp(-0.5)) * _sigmoid(prm["w0"] + lw)
    a = _sigmoid(prm["a0"] + la)
    kkr = xk * prm["k_k"]
    kk = kkr / jnp.maximum(jnp.sqrt(_head_sums_wide(kkr * kkr)), 1e-12)
    kmod = xk * (1.0 + (a - 1.0) * prm["k_a"])
    ka = kk * a

    probs = []
    for ch in range(nchunk):
        rows = slice(ch * TILE, (ch + 1) * TILE)
        cum = _dot(mk["cumsel"], jnp.concatenate(_split3(ld[rows]), axis=0))
        c, cl = cum[:TILE], cum[TILE:]
        e_c = jnp.exp(-c)
        e_l = jnp.exp(cl - c)
        full = dict(rt=(xr[rows] * jnp.exp(c)).astype(BF16),
                    at=(-kk[rows] * jnp.exp(c - ld[rows])).astype(BF16),
                    bt=(ka[rows] * e_c).astype(BF16), kt=(kmod[rows] * e_c).astype(BF16),
                    bh=(ka[rows] * e_l).astype(BF16), kh=(kmod[rows] * e_l).astype(BF16),
                    v=xv[rows].astype(BF16), decay=jnp.exp(cl))
        for j in range(N_PAIRS):
            sl = slice(j * LANES, (j + 1) * LANES)
            probs.append(dict(ch=ch, j=j, **{n: x[:, sl] for n, x in full.items()}))

    for p in probs:
        p["v_s"] = _stack_heads(p["v"])
        z_s = jnp.concatenate([_stack_heads(p["bt"]), _stack_heads(p["kt"])], axis=0)
        gram = _dot_nt(jnp.concatenate([p["at"], p["rt"]], axis=0), z_s)
        p["a_ab"] = jnp.where(mk["strict"], gram[:TILE, :LANES], 0.0)
        p["a_ak"] = jnp.where(mk["strict"], gram[:TILE, LANES:], 0.0).astype(BF16)
        p["a_r"] = jnp.concatenate([jnp.where(mk["incl"], gram[TILE:, :LANES], 0.0),
                                    jnp.where(mk["incl"], gram[TILE:, LANES:], 0.0)],
                                   axis=1).astype(BF16)
        p["tinv"] = mk["eye"]
        if mk["lb"] > 0:
            p["tinv"] = p["tinv"] + jnp.where(mk["levels"][0], p["a_ab"], 0.0)

    for l in range(1, mk["lb"]):
        for p in probs:
            p["tb"] = p["tinv"].astype(BF16)
            mid = _stack_heads(jnp.where(mk["levels"][l], p["a_ab"], 0.0).astype(BF16))
            p["half"] = _dot(p["tb"], mid).astype(BF16)
        for p in probs:
            p["tinv"] = p["tinv"] + _dot(p["half"], _stack_heads(p["tb"]))

    for p in probs:
        p["akv"] = _dot(p["a_ak"], p["v_s"]).astype(BF16)
    for p in probs:
        wu = _dot(p["tinv"].astype(BF16),
                  jnp.concatenate([_stack_heads(p["at"]), _stack_heads(p["akv"])], axis=1))
        p["w"], p["u0"] = wu[:, :LANES].astype(BF16), wu[:, LANES:]

    states = [list(st) for st in states]
    y_rows = []
    for ch in range(nchunk):
        cps = [p for p in probs if p["ch"] == ch]
        for p in cps:
            u_parts, rs_parts = [], []
            for q in range(nseq):
                rows = slice(q * blk, (q + 1) * blk)
                lhs = _cat_rows([p["w"], p["rt"]], rows)
                res = _dot_nt(lhs, states[p["j"]][q].astype(BF16))
                u_parts.append(res[:blk] + p["u0"][rows])
                rs_parts.append(res[blk:])
            p["u"] = (jnp.concatenate(u_parts, axis=0) if nseq > 1 else u_parts[0]).astype(BF16)
            p["rs"] = jnp.concatenate(rs_parts, axis=0) if nseq > 1 else rs_parts[0]
        for p in cps:
            for q in range(nseq):
                rows = slice(q * blk, (q + 1) * blk)
                upd = _dot_tn(_cat_rows([p["u"], p["v"]], rows),
                              _cat_rows([p["bh"], p["kh"]], rows))
                states[p["j"]][q] = (states[p["j"]][q] * p["decay"][q * blk:q * blk + 1]
                                     + jnp.where(mk["same_head"], upd, 0.0))
        y_rows.append(jnp.concatenate(
            [_dot(p["a_r"], jnp.concatenate([_stack_heads(p["u"]), p["v_s"]], axis=0)) + p["rs"]
             for p in cps], axis=1))
    y = jnp.concatenate(y_rows, axis=0) if nchunk > 1 else y_rows[0]

    mean = _head_sums_wide(y) * (1.0 / HEAD_DIM)
    yc = y - mean
    var = _head_sums_wide(yc * yc) * (1.0 / HEAD_DIM)
    yn = yc * lax.rsqrt(var + GN_EPS) * prm["gn_w"] + prm["gn_b"]
    yn = yn + _head_sums_wide(xr * kmod * prm["r_k"]) * xv
    return (yn * _silu(gate)).astype(BF16), states


_RWKV_PARAM_NAMES = ("mu", "w0", "w2", "a0", "a2", "k_k", "k_a", "r_k", "gn_w", "gn_b")


def _rwkv_shift_mix(cur, prev_rows, first_row_mask, mu):
    shifted = jnp.where(first_row_mask, prev_rows, pltpu.roll(cur, 1, axis=0))
    return cur + (shifted - cur) * mu


def _rwkv_load_params(refs):
    prm = {n: r[...] for n, r in zip(_RWKV_PARAM_NAMES, refs)}
    mu = prm.pop("mu")
    return prm, (mu[:, 0:B_WIDTH], mu[:, B_WIDTH:2 * B_WIDTH],
                 mu[:, 2 * B_WIDTH:3 * B_WIDTH], mu[:, 3 * B_WIDTH:])


def _rwkv_prompt_kernel(r_ref, k_ref, v_ref, g_ref, w_ref, *rest):
    prm_refs, (o_ref, s_out_ref, state, prev) = rest[:len(_RWKV_PARAM_NAMES)], rest[len(_RWKV_PARAM_NAMES):]
    i = pl.program_id(1)

    @pl.when(i == 0)
    def _():
        state[...] = jnp.zeros_like(state)
        prev[...] = jnp.zeros_like(prev)

    prm, mus = _rwkv_load_params(prm_refs)
    cur = (r_ref[...], k_ref[...], v_ref[...], w_ref[...])
    rows = cur[0].shape[0]
    first = lax.broadcasted_iota(jnp.int32, (rows, 1), 0) == 0
    offs = (0, B_WIDTH, 2 * B_WIDTH, 3 * B_WIDTH, SHIFT_PAD)
    mixed = []
    for n, x in enumerate(cur):
        prev_row = jnp.broadcast_to(prev[0:1, offs[n]:offs[n + 1]], x.shape)
        mixed.append(_rwkv_shift_mix(x, prev_row, first, mus[n]))
        prev[0:1, offs[n]:offs[n + 1]] = x[rows - 1:rows, :]
    states = [[state[j]] for j in range(N_PAIRS)]
    out, new_states = _rwkv_tile(*mixed, g_ref[...], prm, states, TILE)
    o_ref[...] = out
    for j in range(N_PAIRS):
        s = new_states[j][0]
        state[j] = s
        s_out_ref[2 * j] = s[:HEAD_DIM, :HEAD_DIM]
        s_out_ref[2 * j + 1] = s[HEAD_DIM:, HEAD_DIM:]


def _rwkv_sample_kernel(r_ref, k_ref, v_ref, g_ref, w_ref, sh_ref, s_in_ref, *rest, tdec):
    prm_refs, (o_ref, s_out_ref) = rest[:len(_RWKV_PARAM_NAMES)], rest[len(_RWKV_PARAM_NAMES):]
    nseq = TILE // tdec
    prm, mus = _rwkv_load_params(prm_refs)
    cur = (r_ref[...], k_ref[...], v_ref[...], w_ref[...])
    first = (lax.broadcasted_iota(jnp.int32, (TILE, 1), 0) & (tdec - 1)) == 0
    offs = (0, B_WIDTH, 2 * B_WIDTH, 3 * B_WIDTH, SHIFT_PAD)
    sh = sh_ref[...]
    mixed = []
    for n, x in enumerate(cur):
        prev_rows = jnp.concatenate(
            [jnp.broadcast_to(sh[q:q + 1, offs[n]:offs[n + 1]], (tdec, x.shape[1]))
             for q in range(nseq)], axis=0)
        mixed.append(_rwkv_shift_mix(x, prev_rows, first, mus[n]))
    zero = jnp.zeros((HEAD_DIM, HEAD_DIM), F32)
    states = []
    for j in range(N_PAIRS):
        pair = []
        for q in range(nseq):
            top = jnp.concatenate([s_in_ref[q, 2 * j], zero], axis=1)
            bot = jnp.concatenate([zero, s_in_ref[q, 2 * j + 1]], axis=1)
            pair.append(jnp.concatenate([top, bot], axis=0))
        states.append(pair)
    out, new_states = _rwkv_tile(*mixed, g_ref[...], prm, states, tdec)
    o_ref[...] = out
    for j in range(N_PAIRS):
        for q in range(nseq):
            s = new_states[j][q]
            s_out_ref[q, 2 * j] = s[:HEAD_DIM, :HEAD_DIM]
            s_out_ref[q, 2 * j + 1] = s[HEAD_DIM:, HEAD_DIM:]


def _rwkv_param_specs(l):
    shapes = {"mu": (None, 1, SHIFT_PAD), "w2": (None, LANES, B_WIDTH), "a2": (None, LANES, B_WIDTH)}
    return [pl.BlockSpec(shapes.get(n, (None, 1, B_WIDTH)), lambda *_: (l, 0, 0))
            for n in _RWKV_PARAM_NAMES]


def _rwkv_col_specs(row_fn, rows):
    cols = (COL_R, COL_KB, COL_VB, COL_GB)
    specs = [pl.BlockSpec((rows, B_WIDTH), (lambda *a, c=c: (row_fn(*a), c // B_WIDTH))) for c in cols]
    specs.append(pl.BlockSpec((rows, LANES), lambda *a: (row_fn(*a), COL_WLAL // LANES)))
    return specs


def _rwkv_prompt(proj, prm, l, carriers, batch, seq, rows=RWKV_PROMPT_ROWS):
    nt = seq // rows
    row_fn = lambda b, i: b * nt + i
    ncar = len(carriers)
    return pl.pallas_call(
        _skip(ncar, _rwkv_prompt_kernel),
        grid=(batch, nt),
        in_specs=[_CARRIER] * ncar + _rwkv_col_specs(row_fn, rows) + _rwkv_param_specs(l),
        out_specs=[pl.BlockSpec((rows, B_WIDTH), lambda b, i: (row_fn(b, i), 0)),
                   pl.BlockSpec((None, None, B_HEADS, HEAD_DIM, HEAD_DIM),
                                lambda b, i: (l, b, 0, 0, 0))],
        out_shape=[jax.ShapeDtypeStruct((proj.shape[0], B_WIDTH), BF16),
                   jax.ShapeDtypeStruct((DEPTH, batch, B_HEADS, HEAD_DIM, HEAD_DIM), F32)],
        scratch_shapes=[pltpu.VMEM((N_PAIRS, LANES, LANES), F32),
                        pltpu.VMEM((8, SHIFT_PAD), F32)],
        input_output_aliases={c: 1 + c for c in range(ncar)},
        compiler_params=_params(2),
        name="rwkv_prompt",
    )(*carriers, proj, proj, proj, proj, proj, *[prm[n] for n in _RWKV_PARAM_NAMES])


def _rwkv_sample(proj, row0, ob, shift_in, state_in, prm, l, carriers, nbatch, tdec):
    nseq = TILE // tdec
    rb = row0 // TILE
    row_fn = lambda s: rb + s
    car = [ob] + list(carriers)
    state_spec = pl.BlockSpec((None, nseq, B_HEADS, HEAD_DIM, HEAD_DIM), lambda s: (l, s, 0, 0, 0))
    return pl.pallas_call(
        _skip(len(car), functools.partial(_rwkv_sample_kernel, tdec=tdec)),
        grid=(nbatch // nseq,),
        in_specs=[_CARRIER] * len(car) + _rwkv_col_specs(row_fn, TILE)
        + [pl.BlockSpec((None, nseq, SHIFT_PAD), lambda s: (l, s, 0)), state_spec]
        + _rwkv_param_specs(l),
        out_specs=[pl.BlockSpec((TILE, B_WIDTH), lambda s: (rb + s, 0)), state_spec],
        out_shape=[jax.ShapeDtypeStruct(ob.shape, BF16),
                   jax.ShapeDtypeStruct((DEPTH, nbatch, B_HEADS, HEAD_DIM, HEAD_DIM), F32)],
        input_output_aliases={c: c for c in range(len(car))},
        compiler_params=_params(1),
        name="rwkv_sample",
    )(*car, proj, proj, proj, proj, proj, shift_in, state_in,
      *[prm[n] for n in _RWKV_PARAM_NAMES])


def _window_sums(xe):
    s2 = xe + pltpu.roll(xe, 1, axis=0)
    s4 = s2 + pltpu.roll(s2, 2, axis=0)
    s8 = s4 + pltpu.roll(s4, 4, axis=0)
    s16 = s8 + pltpu.roll(s8, 8, axis=0)
    lane = lax.broadcasted_iota(jnp.int32, (1, C_WIDTH), 1)
    return jnp.where(lane < 64, s2, jnp.where(lane < 128, s4, jnp.where(lane < 192, s8, s16)))


def _pool_window_lane():
    lane = lax.broadcasted_iota(jnp.int32, (1, C_WIDTH), 1)
    return jnp.where(lane < 64, 2, jnp.where(lane < 128, 4, jnp.where(lane < 192, 8, 16)))


def _pool_prompt_kernel(u_ref, halo_ref, g_ref, w_ref, sc_ref, o_ref, *, tm):
    i = pl.program_id(1)
    u = u_ref[...]
    halo = jnp.where(i > 0, halo_ref[...], 0.0)
    sums = _window_sums(jnp.concatenate([halo, u], axis=0))[16:]
    pos = i * tm + lax.broadcasted_iota(jnp.int32, (tm, 1), 0)
    cnt = jnp.minimum(_pool_window_lane(), pos + 1).astype(F32)
    d = sums / cnt - u
    y = _dot(d.astype(BF16), w_ref[...]) * sc_ref[...]
    o_ref[...] = (y * _silu(g_ref[...])).astype(BF16)


def _pool_prompt(proj, wbd_all, scale_all, l, batch, seq, tm=512):
    nt = seq // tm
    row = lambda b, i: b * nt + i
    halo_row = lambda b, i: jnp.maximum(b * (seq // 16) + i * (tm // 16) - 1, 0)
    return pl.pallas_call(
        functools.partial(_pool_prompt_kernel, tm=tm),
        grid=(batch, nt),
        in_specs=[pl.BlockSpec((tm, C_WIDTH), lambda b, i: (row(b, i), COL_UC // C_WIDTH)),
                  pl.BlockSpec((16, C_WIDTH), lambda b, i: (halo_row(b, i), COL_UC // C_WIDTH)),
                  pl.BlockSpec((tm, C_WIDTH), lambda b, i: (row(b, i), COL_GC // C_WIDTH)),
                  pl.BlockSpec((None, C_WIDTH, C_WIDTH), lambda b, i: (l, 0, 0)),
                  pl.BlockSpec((None, 1, C_WIDTH), lambda b, i: (l, 0, 0))],
        out_specs=pl.BlockSpec((tm, C_WIDTH), lambda b, i: (row(b, i), 0)),
        out_shape=jax.ShapeDtypeStruct((proj.shape[0], C_WIDTH), BF16),
        compiler_params=_params(2),
        name="pool_prompt",
    )(proj, proj, proj, wbd_all, scale_all)


def _pool_sample_kernel(u_ref, h_ref, g_ref, w_ref, sc_ref, o_ref, *, nseq, tdec, pos0):
    u = u_ref[...]
    hist = h_ref[...]
    xe = jnp.concatenate([hist, u.reshape(nseq, tdec, C_WIDTH)], axis=1)
    xe = xe.reshape(nseq * (16 + tdec), C_WIDTH)
    sums = _window_sums(xe).reshape(nseq, 16 + tdec, C_WIDTH)[:, 16:, :]
    sums = sums.reshape(nseq * tdec, C_WIDTH)
    t = lax.broadcasted_iota(jnp.int32, (nseq * tdec, 1), 0) & (tdec - 1)
    cnt = jnp.minimum(_pool_window_lane(), pos0 + t + 1).astype(F32)
    d = sums / cnt - u
    y = _dot(d.astype(BF16), w_ref[...]) * sc_ref[...]
    o_ref[...] = (y * _silu(g_ref[...])).astype(BF16)


def _pool_sample(proj, row0, oc, hist16_all, wbd_all, scale_all, l, nbatch, tdec, pos0, nseq=64):
    tm = nseq * tdec
    rb = row0 // tm
    return pl.pallas_call(
        _skip(1, functools.partial(_pool_sample_kernel, nseq=nseq, tdec=tdec, pos0=pos0)),
        grid=(nbatch // nseq,),
        in_specs=[_CARRIER,
                  pl.BlockSpec((tm, C_WIDTH), lambda s: (rb + s, COL_UC // C_WIDTH)),
                  pl.BlockSpec((None, nseq, 16, C_WIDTH), lambda s: (l, s, 0, 0)),
                  pl.BlockSpec((tm, C_WIDTH), lambda s: (rb + s, COL_GC // C_WIDTH)),
                  pl.BlockSpec((None, C_WIDTH, C_WIDTH), lambda s: (l, 0, 0)),
                  pl.BlockSpec((None, 1, C_WIDTH), lambda s: (l, 0, 0))],
        out_specs=pl.BlockSpec((tm, C_WIDTH), lambda s: (rb + s, 0)),
        out_shape=jax.ShapeDtypeStruct(oc.shape, BF16),
        input_output_aliases={0: 0},
        compiler_params=_params(1),
        name="pool_sample",
    )(oc, proj, hist16_all, proj, wbd_all, scale_all)


def _permute_w_in(w):
    q, ka, va, ga = w[:, 0:384], w[:, 384:512], w[:, 512:640], w[:, 640:1024]
    pb, gb = w[:, 1024:2240], w[:, 2240:2624]
    uc, gc = w[:, 2624:2880], w[:, 2880:3136]
    pad = jnp.zeros((w.shape[0], LANES - 2 * LORA), w.dtype)
    return jnp.concatenate([q, ga, pb[:, :3 * B_WIDTH], gb, ka, va, uc, gc,
                            pb[:, 3 * B_WIDTH:], pad], axis=1)


def _pad_shift(x):
    pad = jnp.zeros(x.shape[:-1] + (LANES - 2 * LORA,), x.dtype)
    return jnp.concatenate([x, pad], axis=-1)


def _block_diag(w):
    n, g, c, _ = w.shape
    eye = jnp.eye(g, dtype=w.dtype)
    return (eye[None, :, None, :, None] * w[:, :, :, None, :]).reshape(n, g * c, g * c)


def _shift_row(rows):
    return jnp.concatenate([rows[..., COL_R:COL_R + 3 * B_WIDTH],
                            rows[..., COL_WLAL:COL_WLAL + 2 * LORA]], axis=-1)


def kernel(x_prompt, x_sample, cache_k, cache_v, state_wkv, state_shift, state_pool, norm_g, w_in, q_norm_g, k_norm_g, attn_sinks, shift_mu, decay_w0, decay_w2, iclr_a0, iclr_a2, k_k, k_a, r_k, gn_w, gn_b, pool_w, pool_scale, w_out):
    batch, seq, _ = x_prompt.shape
    nbatch, tdec, _ = x_sample.shape
    wbuf = cache_k.shape[2]
    n_prompt, n_sample = batch * seq, nbatch * tdec
    n_total = n_prompt + n_sample

    row = lambda a: a.reshape(DEPTH, 1, -1)
    lora_pad = jnp.zeros((DEPTH, LANES - LORA, B_WIDTH), F32)
    prm = {
        "mu": row(_pad_shift(shift_mu)), "w0": row(decay_w0), "a0": row(iclr_a0),
        "w2": jnp.concatenate([decay_w2, lora_pad], axis=1).astype(BF16),
        "a2": jnp.concatenate([lora_pad[:, :LORA], iclr_a2, lora_pad[:, LORA:]], axis=1).astype(BF16),
        "k_k": row(k_k), "k_a": row(k_a), "r_k": row(r_k), "gn_w": row(gn_w), "gn_b": row(gn_b),
    }
    g_all = row(norm_g)
    qg_all = row(jnp.tile(q_norm_g, (1, A_HEADS)))
    kg_all = row(jnp.tile(k_norm_g, (1, A_KV_HEADS)))
    w_in_all = jax.vmap(_permute_w_in)(w_in).astype(BF16)
    w_out_all = w_out.astype(BF16)
    wbd_all = _block_diag(pool_w).astype(BF16)
    scale_all = row(pool_scale)
    ck_all = cache_k.reshape(DEPTH, nbatch, wbuf, KV_WIDTH)
    cv_all = cache_v.reshape(DEPTH, nbatch, wbuf, KV_WIDTH)
    shift_all = _pad_shift(state_shift)
    hist16_all = jnp.pad(state_pool, ((0, 0), (0, 0), (1, 0), (0, 0)))

    x_p = x_prompt.reshape(n_prompt, D_MODEL)
    x_s = x_sample.reshape(n_sample, D_MODEL)
    kv_p, kv_s, wkv_p, wkv_s = [], [], [], []
    shift_p, shift_s, pool_p, pool_s = [], [], [], []
    for l in range(DEPTH):
        proj = _inproj(x_p, g_all, w_in_all, l, n_total, 0)
        proj = _inproj(x_s, g_all, w_in_all, l, n_total, n_prompt, carrier=proj)

        oa, *kv_p = _attn_prompt(proj, attn_sinks, qg_all, kg_all, l, kv_p, batch, seq)
        oa, *kv_s = _attn_sample(proj, n_prompt, oa, ck_all, cv_all, attn_sinks, qg_all, kg_all,
                                 l, kv_s, nbatch, tdec)
        ob, *wkv_p = _rwkv_prompt(proj, prm, l, wkv_p, batch, seq)
        ob, *wkv_s = _rwkv_sample(proj, n_prompt, ob, shift_all, state_wkv, prm, l, wkv_s,
                                  nbatch, tdec)
        oc = _pool_prompt(proj, wbd_all, scale_all, l, batch, seq)
        oc = _pool_sample(proj, n_prompt, oc, hist16_all, wbd_all, scale_all, l, nbatch, tdec,
                          PAST_LEN)
        x_p = _outproj(x_p, oa, ob, oc, w_out_all, l, 0)
        x_s = _outproj(x_s, oa, ob, oc, w_out_all, l, n_prompt)

        ps = proj[n_prompt:].reshape(nbatch, tdec, IN_PAD)
        shift_p.append(jnp.stack([_shift_row(proj[(b + 1) * seq - 1]) for b in range(batch)]))
        shift_s.append(_shift_row(ps[:, -1, :]))
        pool_p.append(jnp.stack([proj[(b + 1) * seq - POOL_HIST:(b + 1) * seq,
                                      COL_UC:COL_UC + C_WIDTH] for b in range(batch)]))
        pool_s.append(jnp.concatenate([state_pool[l], ps[:, :, COL_UC:COL_UC + C_WIDTH]],
                                      axis=1)[:, -POOL_HIST:])

    heads = lambda a: a.reshape(a.shape[:-1] + (A_KV_HEADS, HEAD_DIM))
    return (x_p.reshape(batch, seq, D_MODEL), x_s.reshape(nbatch, tdec, D_MODEL),
            heads(kv_p[0]), heads(kv_p[1]), wkv_p[0], jnp.stack(shift_p), jnp.stack(pool_p),
            heads(kv_s[0]), heads(kv_s[1]), wkv_s[0], jnp.stack(shift_s), jnp.stack(pool_s))
```

```python
import functools
import math

import jax
import jax.numpy as jnp
from jax import lax
from jax.experimental import pallas as pl
from jax.experimental.pallas import tpu as pltpu

F32 = jnp.float32
BF16 = jnp.bfloat16

D_MODEL = 1024
DEPTH = 4
HEAD_DIM = 64
A_HEADS = 6
A_KV_HEADS = 2
A_GROUP = A_HEADS // A_KV_HEADS
A_WIDTH = A_HEADS * HEAD_DIM
KV_WIDTH = A_KV_HEADS * HEAD_DIM
WINDOW = 128
B_HEADS = 6
B_WIDTH = B_HEADS * HEAD_DIM
LORA = 32
GN_EPS = 6.4e-4
SHIFT_WIDTH = 3 * B_WIDTH + 2 * LORA
C_WIDTH = 256
POOL_HIST = 15
NORM_EPS = 1e-6
PAST_LEN = 8192

LANES = 128
TILE = 64
RWKV_PROMPT_ROWS = 256
ATTN_PROMPT_ROWS = 256
N_PAIRS = B_HEADS // 2

COL_Q, COL_GA, COL_R, COL_KB, COL_VB, COL_GB = 0, 384, 768, 1152, 1536, 1920
COL_K, COL_V, COL_UC, COL_GC, COL_WLAL = 2304, 2432, 2560, 2816, 3072
IN_PAD = 3200
SHIFT_PAD = 3 * B_WIDTH + LANES

VMEM_LIMIT = 48 * 1024 * 1024


def _dot(a, b, prec=None):
    return jnp.dot(a, b, preferred_element_type=F32, precision=prec)


def _dot_nt(a, b, prec=None):
    return lax.dot_general(a, b, (((1,), (1,)), ((), ())),
                           preferred_element_type=F32, precision=prec)


def _dot_tn(a, b, prec=None):
    return lax.dot_general(a, b, (((0,), (0,)), ((), ())),
                           preferred_element_type=F32, precision=prec)


def _sigmoid(x):
    return 1.0 / (1.0 + jnp.exp(-x))


def _silu(x):
    return x * _sigmoid(x)


def _lane_half_mask(rows=1):
    lane = lax.broadcasted_iota(jnp.int32, (rows, LANES), 1)
    return lane < HEAD_DIM


def _head_sums(x):
    lo = _lane_half_mask()
    s0 = jnp.sum(jnp.where(lo, x, 0.0), axis=-1, keepdims=True)
    s1 = jnp.sum(jnp.where(lo, 0.0, x), axis=-1, keepdims=True)
    return jnp.where(lo, s0, s1)


def _head_sums_wide(x):
    n = x.shape[1] // LANES
    return jnp.concatenate(
        [_head_sums(x[:, j * LANES:(j + 1) * LANES]) for j in range(n)], axis=1)


def _head_rms(x, g):
    ms = _head_sums_wide(x * x) * (1.0 / HEAD_DIM)
    return x * lax.rsqrt(ms + NORM_EPS) * g


_W_SEGMENTS = ((0, 384, COL_Q), (384, 128, COL_K), (512, 128, COL_V), (640, 384, COL_GA),
               (1024, 3 * B_WIDTH, COL_R), (2176, 2 * LORA, COL_WLAL), (2240, 384, COL_GB),
               (2624, 256, COL_UC), (2880, 256, COL_GC))
IN_WIDTH = 3136


def _inproj_kernel(x_ref, g_ref, wt_ref, o_ref, wb):
    @pl.when(pl.program_id(0) == 0)
    def _():
        for src, n, dst in _W_SEGMENTS:
            wb[dst:dst + n, :] = wt_ref[src:src + n, :].astype(BF16)
        pad0 = COL_WLAL + 2 * LORA
        wb[pad0:IN_PAD, :] = jnp.zeros((IN_PAD - pad0, D_MODEL), BF16)

    x = x_ref[...]
    ms = jnp.mean(x * x, axis=-1, keepdims=True)
    h = x * lax.rsqrt(ms + NORM_EPS) * g_ref[...]
    o_ref[...] = _dot_nt(h.astype(BF16), wb[...])


def _skip(n, fn):
    def wrapped(*refs, **kw):
        return fn(*refs[n:], **kw)
    return wrapped


_CARRIER = pl.BlockSpec(memory_space=pl.ANY)


def _params(ndims):
    return pltpu.CompilerParams(dimension_semantics=("arbitrary",) * ndims,
                                vmem_limit_bytes=VMEM_LIMIT)


def _inproj(x, g_all, wt_all, l, n_total, row0, carrier=None, tm=512):
    n = x.shape[0]
    rb = row0 // tm
    car = [] if carrier is None else [carrier]
    return pl.pallas_call(
        _skip(len(car), _inproj_kernel),
        grid=(n // tm,),
        in_specs=[_CARRIER] * len(car)
        + [pl.BlockSpec((tm, D_MODEL), lambda i: (i, 0)),
           pl.BlockSpec((None, 1, D_MODEL), lambda i: (l, 0, 0)),
           pl.BlockSpec((None, IN_WIDTH, D_MODEL), lambda i: (l, 0, 0),
                        pipeline_mode=pl.Buffered(1))],
        out_specs=pl.BlockSpec((tm, IN_PAD), lambda i: (rb + i, 0)),
        out_shape=jax.ShapeDtypeStruct((n_total, IN_PAD), F32),
        scratch_shapes=[pltpu.VMEM((IN_PAD, D_MODEL), BF16)],
        input_output_aliases={0: 0} if car else {},
        compiler_params=_params(1),
        name="inproj",
    )(*car, x, g_all, wt_all)


def _outproj_kernel(x_ref, oa_ref, ob_ref, oc_ref, wa_ref, wb_ref, wc_ref, o_ref):
    acc = _dot(oa_ref[...], wa_ref[...])
    acc += _dot(ob_ref[...], wb_ref[...])
    acc += _dot(oc_ref[...], wc_ref[...])
    o_ref[...] = x_ref[...] + acc


def _outproj(x, oa, ob, oc, w_all, l, row0, tm=512):
    n = x.shape[0]
    rb = row0 // tm
    return pl.pallas_call(
        _outproj_kernel,
        grid=(n // tm,),
        in_specs=[pl.BlockSpec((tm, D_MODEL), lambda i: (i, 0)),
                  pl.BlockSpec((tm, A_WIDTH), lambda i: (rb + i, 0)),
                  pl.BlockSpec((tm, B_WIDTH), lambda i: (rb + i, 0)),
                  pl.BlockSpec((tm, C_WIDTH), lambda i: (rb + i, 0)),
                  pl.BlockSpec((None, A_WIDTH, D_MODEL), lambda i: (l, 0, 0)),
                  pl.BlockSpec((None, B_WIDTH, D_MODEL), lambda i: (l, 1, 0)),
                  pl.BlockSpec((None, C_WIDTH, D_MODEL), lambda i: (l, 3, 0))],
        out_specs=pl.BlockSpec((tm, D_MODEL), lambda i: (i, 0)),
        out_shape=jax.ShapeDtypeStruct((n, D_MODEL), F32),
        compiler_params=_params(1),
        name="outproj",
    )(x, oa, ob, oc, w_all, w_all, w_all)


def _attn_prompt_kernel(sink_ref, q_ref, ga_ref, k_ref, v_ref, qg_ref, kg_ref,
                        o_ref, kl_ref, vl_ref, kprev, vprev, *, layer):
    i = pl.program_id(1)

    @pl.when(i == 0)
    def _():
        kprev[...] = jnp.zeros_like(kprev)
        vprev[...] = jnp.zeros_like(vprev)

    nblk = q_ref.shape[0] // WINDOW
    k = _head_rms(k_ref[...], kg_ref[...])
    v = v_ref[...]
    qn = (_head_rms(q_ref[...], qg_ref[...]) * (HEAD_DIM ** -0.5)).astype(BF16)
    kc = jnp.concatenate([kprev[...], k], axis=0).astype(BF16)
    vc = jnp.concatenate([vprev[...], v], axis=0).astype(BF16)

    rows = A_GROUP * WINDOW
    r = lax.broadcasted_iota(jnp.int32, (rows, 2 * WINDOW), 0)
    j = lax.broadcasted_iota(jnp.int32, (rows, 2 * WINDOW), 1)
    rel = (r & (WINDOW - 1)) + WINDOW - j
    band = (rel >= 0) & (rel < WINDOW)
    band_first = band & ((j >= WINDOW) | (i > 0))
    rcol = lax.broadcasted_iota(jnp.int32, (rows, 1), 0)

    probs = [dict(b=b, n=n) for b in range(nblk) for n in range(A_KV_HEADS)]
    for p in probs:
        b, n = p["b"], p["n"]
        heads = [A_GROUP * n + g for g in range(A_GROUP)]
        qs = jnp.concatenate([qn[b * WINDOW:(b + 1) * WINDOW, HEAD_DIM * h:HEAD_DIM * (h + 1)]
                              for h in heads], axis=0)
        keys = slice(b * WINDOW, (b + 2) * WINDOW)
        p["s"] = _dot_nt(qs, kc[keys, HEAD_DIM * n:HEAD_DIM * (n + 1)])
        p["v"] = vc[keys, HEAD_DIM * n:HEAD_DIM * (n + 1)]
        sink_col = jnp.full((rows, 1), sink_ref[layer, heads[-1]], F32)
        for g in reversed(range(A_GROUP - 1)):
            sink_col = jnp.where(rcol < (g + 1) * WINDOW, sink_ref[layer, heads[g]], sink_col)
        p["sink"] = sink_col
    for p in probs:
        sm = jnp.where(band_first if p["b"] == 0 else band, p["s"], -1e30)
        m = jnp.maximum(jnp.max(sm, axis=-1, keepdims=True), p["sink"])
        e = jnp.exp(sm - m)
        p["den"] = jnp.sum(e, axis=-1, keepdims=True) + jnp.exp(p["sink"] - m)
        p["e"] = e.astype(BF16)
    for p in probs:
        p["o"] = _dot(p["e"], p["v"]) / p["den"]
    o_rows = []
    for b in range(nblk):
        outs = []
        for p in probs[b * A_KV_HEADS:(b + 1) * A_KV_HEADS]:
            outs += [p["o"][g * WINDOW:(g + 1) * WINDOW] for g in range(A_GROUP)]
        o_rows.append(jnp.concatenate(outs, axis=1))
    o_all = jnp.concatenate(o_rows, axis=0) if nblk > 1 else o_rows[0]
    o_ref[...] = (o_all * _silu(ga_ref[...])).astype(BF16)

    last = slice((nblk - 1) * WINDOW, nblk * WINDOW)
    kprev[...] = k[last]
    vprev[...] = v[last]
    kl_ref[...] = k[last]
    vl_ref[...] = v[last]


def _attn_prompt(proj, sinks, qg_all, kg_all, l, carriers, batch, seq):
    tm = ATTN_PROMPT_ROWS
    nb = seq // tm
    n_total = proj.shape[0]
    cb = lambda c, w: c // w
    row = lambda b, i: b * nb + i
    ncar = len(carriers)
    return pl.pallas_call(
        _skip(ncar, functools.partial(_attn_prompt_kernel, layer=l)),
        grid=(batch, nb),
        in_specs=[_CARRIER] * ncar
        + [pl.BlockSpec(memory_space=pltpu.SMEM),
           pl.BlockSpec((tm, A_WIDTH), lambda b, i: (row(b, i), cb(COL_Q, A_WIDTH))),
           pl.BlockSpec((tm, A_WIDTH), lambda b, i: (row(b, i), cb(COL_GA, A_WIDTH))),
           pl.BlockSpec((tm, KV_WIDTH), lambda b, i: (row(b, i), cb(COL_K, KV_WIDTH))),
           pl.BlockSpec((tm, KV_WIDTH), lambda b, i: (row(b, i), cb(COL_V, KV_WIDTH))),
           pl.BlockSpec((None, 1, A_WIDTH), lambda b, i: (l, 0, 0)),
           pl.BlockSpec((None, 1, KV_WIDTH), lambda b, i: (l, 0, 0))],
        out_specs=[pl.BlockSpec((tm, A_WIDTH), lambda b, i: (row(b, i), 0)),
                   pl.BlockSpec((None, None, WINDOW, KV_WIDTH), lambda b, i: (l, b, 0, 0)),
                   pl.BlockSpec((None, None, WINDOW, KV_WIDTH), lambda b, i: (l, b, 0, 0))],
        out_shape=[jax.ShapeDtypeStruct((n_total, A_WIDTH), BF16),
                   jax.ShapeDtypeStruct((DEPTH, batch, WINDOW, KV_WIDTH), F32),
                   jax.ShapeDtypeStruct((DEPTH, batch, WINDOW, KV_WIDTH), F32)],
        scratch_shapes=[pltpu.VMEM((WINDOW, KV_WIDTH), F32),
                        pltpu.VMEM((WINDOW, KV_WIDTH), F32)],
        input_output_aliases={c: 1 + c for c in range(ncar)},
        compiler_params=_params(2),
        name="attn_prompt",
    )(*carriers, sinks, proj, proj, proj, proj, qg_all, kg_all)


def _attn_sample_kernel(sink_ref, q_ref, ga_ref, k_ref, v_ref, ck_ref, cv_ref, qg_ref, kg_ref,
                        o_ref, nk_ref, nv_ref, *, nseq, tdec, layer):
    k = _head_rms(k_ref[...], kg_ref[...])
    v = v_ref[...]
    qn = _head_rms(q_ref[...], qg_ref[...])
    ck = ck_ref[...]
    cv = cv_ref[...]
    k3 = k.reshape(nseq, tdec, KV_WIDTH)
    v3 = v.reshape(nseq, tdec, KV_WIDTH)
    kc = jnp.concatenate([ck, k3], axis=1)
    vc = jnp.concatenate([cv, v3], axis=1)
    nk_ref[...] = kc[:, tdec:, :]
    nv_ref[...] = vc[:, tdec:, :]
    kcb = kc.astype(BF16)
    vcb = vc.astype(BF16)

    rows = A_GROUP * tdec
    keys = WINDOW + tdec
    r = lax.broadcasted_iota(jnp.int32, (1, rows, keys), 1)
    j = lax.broadcasted_iota(jnp.int32, (1, rows, keys), 2)
    rel = WINDOW + (r & (tdec - 1)) - j
    ok = (rel >= 0) & (rel < WINDOW)
    rcol = lax.broadcasted_iota(jnp.int32, (1, rows, 1), 1)

    outs = []
    for n in range(A_KV_HEADS):
        heads = [A_GROUP * n + g for g in range(A_GROUP)]
        qs = jnp.concatenate(
            [qn[:, HEAD_DIM * h:HEAD_DIM * (h + 1)].reshape(nseq, tdec, HEAD_DIM) for h in heads],
            axis=1).astype(BF16)
        kn = kcb[:, :, HEAD_DIM * n:HEAD_DIM * (n + 1)]
        vn = vcb[:, :, HEAD_DIM * n:HEAD_DIM * (n + 1)]
        s = jnp.einsum("bqd,bkd->bqk", qs, kn, preferred_element_type=F32) * (HEAD_DIM ** -0.5)
        sink_col = jnp.full((1, rows, 1), sink_ref[layer, heads[-1]], F32)
        for g in reversed(range(A_GROUP - 1)):
            sink_col = jnp.where(rcol < (g + 1) * tdec, sink_ref[layer, heads[g]], sink_col)
        sm = jnp.where(ok, s, -1e30)
        m = jnp.maximum(jnp.max(sm, axis=-1, keepdims=True), sink_col)
        p = jnp.where(ok, jnp.exp(sm - m), 0.0)
        den = jnp.sum(p, axis=-1, keepdims=True) + jnp.exp(sink_col - m)
        o = jnp.einsum("bqk,bkd->bqd", p.astype(BF16), vn, preferred_element_type=F32) / den
        outs += [o[:, g * tdec:(g + 1) * tdec, :].reshape(nseq * tdec, HEAD_DIM)
                 for g in range(A_GROUP)]
    o_all = jnp.concatenate(outs, axis=1)
    o_ref[...] = (o_all * _silu(ga_ref[...])).astype(BF16)


def _attn_sample(proj, row0, oa, cache_k, cache_v, sinks, qg_all, kg_all, l, carriers,
                 nbatch, tdec, nseq=16):
    tm = nseq * tdec
    rb = row0 // tm
    cb = lambda c, w: c // w
    car = [oa] + list(carriers)
    cache_spec = pl.BlockSpec((None, nseq, WINDOW, KV_WIDTH), lambda s: (l, s, 0, 0))
    return pl.pallas_call(
        _skip(len(car), functools.partial(_attn_sample_kernel, nseq=nseq, tdec=tdec, layer=l)),
        grid=(nbatch // nseq,),
        in_specs=[_CARRIER] * len(car)
        + [pl.BlockSpec(memory_space=pltpu.SMEM),
           pl.BlockSpec((tm, A_WIDTH), lambda s: (rb + s, cb(COL_Q, A_WIDTH))),
           pl.BlockSpec((tm, A_WIDTH), lambda s: (rb + s, cb(COL_GA, A_WIDTH))),
           pl.BlockSpec((tm, KV_WIDTH), lambda s: (rb + s, cb(COL_K, KV_WIDTH))),
           pl.BlockSpec((tm, KV_WIDTH), lambda s: (rb + s, cb(COL_V, KV_WIDTH))),
           cache_spec, cache_spec,
           pl.BlockSpec((None, 1, A_WIDTH), lambda s: (l, 0, 0)),
           pl.BlockSpec((None, 1, KV_WIDTH), lambda s: (l, 0, 0))],
        out_specs=[pl.BlockSpec((tm, A_WIDTH), lambda s: (rb + s, 0)), cache_spec, cache_spec],
        out_shape=[jax.ShapeDtypeStruct(oa.shape, BF16),
                   jax.ShapeDtypeStruct((DEPTH, nbatch, WINDOW, KV_WIDTH), F32),
                   jax.ShapeDtypeStruct((DEPTH, nbatch, WINDOW, KV_WIDTH), F32)],
        input_output_aliases={c: c for c in range(len(car))},
        compiler_params=_params(1),
        name="attn_sample",
    )(*car, sinks, proj, proj, proj, proj, cache_k, cache_v, qg_all, kg_all)


def _stack_heads(x):
    lo = jnp.where(_lane_half_mask(), 1.0, 0.0).astype(x.dtype)
    return jnp.concatenate([x * lo, x * (1 - lo)], axis=0)


def _cat_rows(parts, rows):
    if (rows.stop - rows.start) % 16 == 0:
        return jnp.concatenate([x[rows] for x in parts], axis=0)
    return jnp.concatenate([x.astype(F32)[rows] for x in parts], axis=0).astype(BF16)


def _split3(x):
    hi = x.astype(BF16)
    r1 = x - hi.astype(F32)
    mid = r1.astype(BF16)
    lo = (r1 - mid.astype(F32)).astype(BF16)
    return hi, mid, lo


def _rwkv_masks(blk):
    lb = blk.bit_length() - 1
    t = lax.broadcasted_iota(jnp.int32, (TILE, LANES), 0)
    s = lax.broadcasted_iota(jnp.int32, (TILE, LANES), 1) & (TILE - 1)
    same = (t >> lb) == (s >> lb)
    levels = [((t >> (l + 1)) == (s >> (l + 1))) & (((t >> l) & 1) == 1) & (((s >> l) & 1) == 0)
              for l in range(lb)]
    tt = lax.broadcasted_iota(jnp.int32, (2 * TILE, 3 * TILE), 0)
    ss = lax.broadcasted_iota(jnp.int32, (2 * TILE, 3 * TILE), 1)
    ti = tt & (TILE - 1)
    si = jnp.where(ss >= 2 * TILE, ss - 2 * TILE, jnp.where(ss >= TILE, ss - TILE, ss))
    sel = ((ti >> lb) == (si >> lb)) & ((tt >= TILE) | (si <= ti))
    r2 = lax.broadcasted_iota(jnp.int32, (LANES, LANES), 0)
    c2 = lax.broadcasted_iota(jnp.int32, (LANES, LANES), 1)
    return dict(lb=lb, strict=same & (s < t), incl=same & (s <= t),
                eye=jnp.where(s == t, 1.0, 0.0), levels=levels,
                cumsel=jnp.where(sel, 1.0, 0.0).astype(BF16),
                same_head=(r2 >> 6) == (c2 >> 6))


def _rwkv_tile(xr, xk, xv, xw, gate, prm, states, blk):
    nchunk = xr.shape[0] // TILE
    nseq = TILE // blk
    mk = _rwkv_masks(blk)
    lw = _dot(jnp.tanh(xw).astype(BF16), prm["w2"])
    la = _dot(xw.astype(BF16), prm["a2"])
    ld = (-math.exp(-0.5)) * _sigmoid(prm["w0"] + lw)
    a = _sigmoid(prm["a0"] + la)
    kkr = xk * prm["k_k"]
    kk = kkr / jnp.maximum(jnp.sqrt(_head_sums_wide(kkr * kkr)), 1e-12)
    kmod = xk * (1.0 + (a - 1.0) * prm["k_a"])
    ka = kk * a

    probs = []
    for ch in range(nchunk):
        rows = slice(ch * TILE, (ch + 1) * TILE)
        cum = _dot(mk["cumsel"], jnp.concatenate(_split3(ld[rows]), axis=0))
        c, cl = cum[:TILE], cum[TILE:]
        e_c = jnp.exp(-c)
        e_l = jnp.exp(cl - c)
        full = dict(rt=(xr[rows] * jnp.exp(c)).astype(BF16),
                    at=(-kk[rows] * jnp.exp(c - ld[rows])).astype(BF16),
                    bt=(ka[rows] * e_c).astype(BF16), kt=(kmod[rows] * e_c).astype(BF16),
                    bh=(ka[rows] * e_l).astype(BF16), kh=(kmod[rows] * e_l).astype(BF16),
                    v=xv[rows].astype(BF16), decay=jnp.exp(cl))
        for j in range(N_PAIRS):
            sl = slice(j * LANES, (j + 1) * LANES)
            probs.append(dict(ch=ch, j=j, **{n: x[:, sl] for n, x in full.items()}))

    for p in probs:
        p["v_s"] = _stack_heads(p["v"])
        z_s = jnp.concatenate([_stack_heads(p["bt"]), _stack_heads(p["kt"])], axis=0)
        gram = _dot_nt(jnp.concatenate([p["at"], p["rt"]], axis=0), z_s)
        p["a_ab"] = jnp.where(mk["strict"], gram[:TILE, :LANES], 0.0)
        p["a_ak"] = jnp.where(mk["strict"], gram[:TILE, LANES:], 0.0).astype(BF16)
        p["a_r"] = jnp.concatenate([jnp.where(mk["incl"], gram[TILE:, :LANES], 0.0),
                                    jnp.where(mk["incl"], gram[TILE:, LANES:], 0.0)],
                                   axis=1).astype(BF16)
        p["tinv"] = mk["eye"]
        if mk["lb"] > 0:
            p["tinv"] = p["tinv"] + jnp.where(mk["levels"][0], p["a_ab"], 0.0)

    for l in range(1, mk["lb"]):
        for p in probs:
            p["tb"] = p["tinv"].astype(BF16)
            mid = _stack_heads(jnp.where(mk["levels"][l], p["a_ab"], 0.0).astype(BF16))
            p["half"] = _dot(p["tb"], mid).astype(BF16)
        for p in probs:
            p["tinv"] = p["tinv"] + _dot(p["half"], _stack_heads(p["tb"]))

    for p in probs:
        p["akv"] = _dot(p["a_ak"], p["v_s"]).astype(BF16)
    for p in probs:
        wu = _dot(p["tinv"].astype(BF16),
                  jnp.concatenate([_stack_heads(p["at"]), _stack_heads(p["akv"])], axis=1))
        p["w"], p["u0"] = wu[:, :LANES].astype(BF16), wu[:, LANES:]

    states = [list(st) for st in states]
    y_rows = []
    for ch in range(nchunk):
        cps = [p for p in probs if p["ch"] == ch]
        for p in cps:
            u_parts, rs_parts = [], []
            for q in range(nseq):
                rows = slice(q * blk, (q + 1) * blk)
                lhs = _cat_rows([p["w"], p["rt"]], rows)
                res = _dot_nt(lhs, states[p["j"]][q].astype(BF16))
                u_parts.append(res[:blk] + p["u0"][rows])
                rs_parts.append(res[blk:])
            p["u"] = (jnp.concatenate(u_parts, axis=0) if nseq > 1 else u_parts[0]).astype(BF16)
            p["rs"] = jnp.concatenate(rs_parts, axis=0) if nseq > 1 else rs_parts[0]
        for p in cps:
            for q in range(nseq):
                rows = slice(q * blk, (q + 1) * blk)
                upd = _dot_tn(_cat_rows([p["u"], p["v"]], rows),
                              _cat_rows([p["bh"], p["kh"]], rows))
                states[p["j"]][q] = (states[p["j"]][q] * p["decay"][q * blk:q * blk + 1]
                                     + jnp.where(mk["same_head"], upd, 0.0))
        y_rows.append(jnp.concatenate(
            [_dot(p["a_r"], jnp.concatenate([_stack_heads(p["u"]), p["v_s"]], axis=0)) + p["rs"]
             for p in cps], axis=1))
    y = jnp.concatenate(y_rows, axis=0) if nchunk > 1 else y_rows[0]

    mean = _head_sums_wide(y) * (1.0 / HEAD_DIM)
    yc = y - mean
    var = _head_sums_wide(yc * yc) * (1.0 / HEAD_DIM)
    yn = yc * lax.rsqrt(var + GN_EPS) * prm["gn_w"] + prm["gn_b"]
    yn = yn + _head_sums_wide(xr * kmod * prm["r_k"]) * xv
    return (yn * _silu(gate)).astype(BF16), states


_RWKV_PARAM_NAMES = ("mu", "w0", "w2", "a0", "a2", "k_k", "k_a", "r_k", "gn_w", "gn_b")


def _rwkv_shift_mix(cur, prev_rows, first_row_mask, mu):
    shifted = jnp.where(first_row_mask, prev_rows, pltpu.roll(cur, 1, axis=0))
    return cur + (shifted - cur) * mu


def _rwkv_load_params(refs):
    prm = {n: r[...] for n, r in zip(_RWKV_PARAM_NAMES, refs)}
    mu = prm.pop("mu")
    return prm, (mu[:, 0:B_WIDTH], mu[:, B_WIDTH:2 * B_WIDTH],
                 mu[:, 2 * B_WIDTH:3 * B_WIDTH], mu[:, 3 * B_WIDTH:])


def _rwkv_prompt_kernel(r_ref, k_ref, v_ref, g_ref, w_ref, *rest):
    prm_refs, (o_ref, s_out_ref, state, prev) = rest[:len(_RWKV_PARAM_NAMES)], rest[len(_RWKV_PARAM_NAMES):]
    i = pl.program_id(1)

    @pl.when(i == 0)
    def _():
        state[...] = jnp.zeros_like(state)
        prev[...] = jnp.zeros_like(prev)

    prm, mus = _rwkv_load_params(prm_refs)
    cur = (r_ref[...], k_ref[...], v_ref[...], w_ref[...])
    rows = cur[0].shape[0]
    first = lax.broadcasted_iota(jnp.int32, (rows, 1), 0) == 0
    offs = (0, B_WIDTH, 2 * B_WIDTH, 3 * B_WIDTH, SHIFT_PAD)
    mixed = []
    for n, x in enumerate(cur):
        prev_row = jnp.broadcast_to(prev[0:1, offs[n]:offs[n + 1]], x.shape)
        mixed.append(_rwkv_shift_mix(x, prev_row, first, mus[n]))
        prev[0:1, offs[n]:offs[n + 1]] = x[rows - 1:rows, :]
    states = [[state[j]] for j in range(N_PAIRS)]
    out, new_states = _rwkv_tile(*mixed, g_ref[...], prm, states, TILE)
    o_ref[...] = out
    for j in range(N_PAIRS):
        s = new_states[j][0]
        state[j] = s
        s_out_ref[2 * j] = s[:HEAD_DIM, :HEAD_DIM]
        s_out_ref[2 * j + 1] = s[HEAD_DIM:, HEAD_DIM:]


def _rwkv_sample_kernel(r_ref, k_ref, v_ref, g_ref, w_ref, sh_ref, s_in_ref, *rest, tdec):
    prm_refs, (o_ref, s_out_ref) = rest[:len(_RWKV_PARAM_NAMES)], rest[len(_RWKV_PARAM_NAMES):]
    nseq = TILE // tdec
    prm, mus = _rwkv_load_params(prm_refs)
    cur = (r_ref[...], k_ref[...], v_ref[...], w_ref[...])
    first = (lax.broadcasted_iota(jnp.int32, (TILE, 1), 0) & (tdec - 1)) == 0
    offs = (0, B_WIDTH, 2 * B_WIDTH, 3 * B_WIDTH, SHIFT_PAD)
    sh = sh_ref[...]
    mixed = []
    for n, x in enumerate(cur):
        prev_rows = jnp.concatenate(
            [jnp.broadcast_to(sh[q:q + 1, offs[n]:offs[n + 1]], (tdec, x.shape[1]))
             for q in range(nseq)], axis=0)
        mixed.append(_rwkv_shift_mix(x, prev_rows, first, mus[n]))
    zero = jnp.zeros((HEAD_DIM, HEAD_DIM), F32)
    states = []
    for j in range(N_PAIRS):
        pair = []
        for q in range(nseq):
            top = jnp.concatenate([s_in_ref[q, 2 * j], zero], axis=1)
            bot = jnp.concatenate([zero, s_in_ref[q, 2 * j + 1]], axis=1)
            pair.append(jnp.concatenate([top, bot], axis=0))
        states.append(pair)
    out, new_states = _rwkv_tile(*mixed, g_ref[...], prm, states, tdec)
    o_ref[...] = out
    for j in range(N_PAIRS):
        for q in range(nseq):
            s = new_states[j][q]
            s_out_ref[q, 2 * j] = s[:HEAD_DIM, :HEAD_DIM]
            s_out_ref[q, 2 * j + 1] = s[HEAD_DIM:, HEAD_DIM:]


def _rwkv_param_specs(l):
    shapes = {"mu": (None, 1, SHIFT_PAD), "w2": (None, LANES, B_WIDTH), "a2": (None, LANES, B_WIDTH)}
    return [pl.BlockSpec(shapes.get(n, (None, 1, B_WIDTH)), lambda *_: (l, 0, 0))
            for n in _RWKV_PARAM_NAMES]


def _rwkv_col_specs(row_fn, rows):
    cols = (COL_R, COL_KB, COL_VB, COL_GB)
    specs = [pl.BlockSpec((rows, B_WIDTH), (lambda *a, c=c: (row_fn(*a), c // B_WIDTH))) for c in cols]
    specs.append(pl.BlockSpec((rows, LANES), lambda *a: (row_fn(*a), COL_WLAL // LANES)))
    return specs


def _rwkv_prompt(proj, prm, l, carriers, batch, seq, rows=RWKV_PROMPT_ROWS):
    nt = seq // rows
    row_fn = lambda b, i: b * nt + i
    ncar = len(carriers)
    return pl.pallas_call(
        _skip(ncar, _rwkv_prompt_kernel),
        grid=(batch, nt),
        in_specs=[_CARRIER] * ncar + _rwkv_col_specs(row_fn, rows) + _rwkv_param_specs(l),
        out_specs=[pl.BlockSpec((rows, B_WIDTH), lambda b, i: (row_fn(b, i), 0)),
                   pl.BlockSpec((None, None, B_HEADS, HEAD_DIM, HEAD_DIM),
                                lambda b, i: (l, b, 0, 0, 0))],
        out_shape=[jax.ShapeDtypeStruct((proj.shape[0], B_WIDTH), BF16),
                   jax.ShapeDtypeStruct((DEPTH, batch, B_HEADS, HEAD_DIM, HEAD_DIM), F32)],
        scratch_shapes=[pltpu.VMEM((N_PAIRS, LANES, LANES), F32),
                        pltpu.VMEM((8, SHIFT_PAD), F32)],
        input_output_aliases={c: 1 + c for c in range(ncar)},
        compiler_params=_params(2),
        name="rwkv_prompt",
    )(*carriers, proj, proj, proj, proj, proj, *[prm[n] for n in _RWKV_PARAM_NAMES])


def _rwkv_sample(proj, row0, ob, shift_in, state_in, prm, l, carriers, nbatch, tdec):
    nseq = TILE // tdec
    rb = row0 // TILE
    row_fn = lambda s: rb + s
    car = [ob] + list(carriers)
    state_spec = pl.BlockSpec((None, nseq, B_HEADS, HEAD_DIM, HEAD_DIM), lambda s: (l, s, 0, 0, 0))
    return pl.pallas_call(
        _skip(len(car), functools.partial(_rwkv_sample_kernel, tdec=tdec)),
        grid=(nbatch // nseq,),
        in_specs=[_CARRIER] * len(car) + _rwkv_col_specs(row_fn, TILE)
        + [pl.BlockSpec((None, nseq, SHIFT_PAD), lambda s: (l, s, 0)), state_spec]
        + _rwkv_param_specs(l),
        out_specs=[pl.BlockSpec((TILE, B_WIDTH), lambda s: (rb + s, 0)), state_spec],
        out_shape=[jax.ShapeDtypeStruct(ob.shape, BF16),
                   jax.ShapeDtypeStruct((DEPTH, nbatch, B_HEADS, HEAD_DIM, HEAD_DIM), F32)],
        input_output_aliases={c: c for c in range(len(car))},
        compiler_params=_params(1),
        name="rwkv_sample",
    )(*car, proj, proj, proj, proj, proj, shift_in, state_in,
      *[prm[n] for n in _RWKV_PARAM_NAMES])


def _window_sums(xe):
    s2 = xe + pltpu.roll(xe, 1, axis=0)
    s4 = s2 + pltpu.roll(s2, 2, axis=0)
    s8 = s4 + pltpu.roll(s4, 4, axis=0)
    s16 = s8 + pltpu.roll(s8, 8, axis=0)
    lane = lax.broadcasted_iota(jnp.int32, (1, C_WIDTH), 1)
    return jnp.where(lane < 64, s2, jnp.where(lane < 128, s4, jnp.where(lane < 192, s8, s16)))


def _pool_window_lane():
    lane = lax.broadcasted_iota(jnp.int32, (1, C_WIDTH), 1)
    return jnp.where(lane < 64, 2, jnp.where(lane < 128, 4, jnp.where(lane < 192, 8, 16)))


def _pool_prompt_kernel(u_ref, halo_ref, g_ref, w_ref, sc_ref, o_ref, *, tm):
    i = pl.program_id(1)
    u = u_ref[...]
    halo = jnp.where(i > 0, halo_ref[...], 0.0)
    sums = _window_sums(jnp.concatenate([halo, u], axis=0))[16:]
    pos = i * tm + lax.broadcasted_iota(jnp.int32, (tm, 1), 0)
    cnt = jnp.minimum(_pool_window_lane(), pos + 1).astype(F32)
    d = sums / cnt - u
    y = _dot(d.astype(BF16), w_ref[...]) * sc_ref[...]
    o_ref[...] = (y * _silu(g_ref[...])).astype(BF16)


def _pool_prompt(proj, wbd_all, scale_all, l, batch, seq, tm=512):
    nt = seq // tm
    row = lambda b, i: b * nt + i
    halo_row = lambda b, i: jnp.maximum(b * (seq // 16) + i * (tm // 16) - 1, 0)
    return pl.pallas_call(
        functools.partial(_pool_prompt_kernel, tm=tm),
        grid=(batch, nt),
        in_specs=[pl.BlockSpec((tm, C_WIDTH), lambda b, i: (row(b, i), COL_UC // C_WIDTH)),
                  pl.BlockSpec((16, C_WIDTH), lambda b, i: (halo_row(b, i), COL_UC // C_WIDTH)),
                  pl.BlockSpec((tm, C_WIDTH), lambda b, i: (row(b, i), COL_GC // C_WIDTH)),
                  pl.BlockSpec((None, C_WIDTH, C_WIDTH), lambda b, i: (l, 0, 0)),
                  pl.BlockSpec((None, 1, C_WIDTH), lambda b, i: (l, 0, 0))],
        out_specs=pl.BlockSpec((tm, C_WIDTH), lambda b, i: (row(b, i), 0)),
        out_shape=jax.ShapeDtypeStruct((proj.shape[0], C_WIDTH), BF16),
        compiler_params=_params(2),
        name="pool_prompt",
    )(proj, proj, proj, wbd_all, scale_all)


def _pool_sample_kernel(u_ref, h_ref, g_ref, w_ref, sc_ref, o_ref, *, nseq, tdec, pos0):
    u = u_ref[...]
    hist = h_ref[...]
    xe = jnp.concatenate([hist, u.reshape(nseq, tdec, C_WIDTH)], axis=1)
    xe = xe.reshape(nseq * (16 + tdec), C_WIDTH)
    sums = _window_sums(xe).reshape(nseq, 16 + tdec, C_WIDTH)[:, 16:, :]
    sums = sums.reshape(nseq * tdec, C_WIDTH)
    t = lax.broadcasted_iota(jnp.int32, (nseq * tdec, 1), 0) & (tdec - 1)
    cnt = jnp.minimum(_pool_window_lane(), pos0 + t + 1).astype(F32)
    d = sums / cnt - u
    y = _dot(d.astype(BF16), w_ref[...]) * sc_ref[...]
    o_ref[...] = (y * _silu(g_ref[...])).astype(BF16)


def _pool_sample(proj, row0, oc, hist16_all, wbd_all, scale_all, l, nbatch, tdec, pos0, nseq=64):
    tm = nseq * tdec
    rb = row0 // tm
    return pl.pallas_call(
        _skip(1, functools.partial(_pool_sample_kernel, nseq=nseq, tdec=tdec, pos0=pos0)),
        grid=(nbatch // nseq,),
        in_specs=[_CARRIER,
                  pl.BlockSpec((tm, C_WIDTH), lambda s: (rb + s, COL_UC // C_WIDTH)),
                  pl.BlockSpec((None, nseq, 16, C_WIDTH), lambda s: (l, s, 0, 0)),
                  pl.BlockSpec((tm, C_WIDTH), lambda s: (rb + s, COL_GC // C_WIDTH)),
                  pl.BlockSpec((None, C_WIDTH, C_WIDTH), lambda s: (l, 0, 0)),
                  pl.BlockSpec((None, 1, C_WIDTH), lambda s: (l, 0, 0))],
        out_specs=pl.BlockSpec((tm, C_WIDTH), lambda s: (rb + s, 0)),
        out_shape=jax.ShapeDtypeStruct(oc.shape, BF16),
        input_output_aliases={0: 0},
        compiler_params=_params(1),
        name="pool_sample",
    )(oc, proj, hist16_all, proj, wbd_all, scale_all)


def _pad_shift(x):
    pad = jnp.zeros(x.shape[:-1] + (LANES - 2 * LORA,), x.dtype)
    return jnp.concatenate([x, pad], axis=-1)


def _block_diag(w):
    n, g, c, _ = w.shape
    eye = jnp.eye(g, dtype=w.dtype)
    return (eye[None, :, None, :, None] * w[:, :, :, None, :]).reshape(n, g * c, g * c)


def _shift_row(rows):
    return jnp.concatenate([rows[..., COL_R:COL_R + 3 * B_WIDTH],
                            rows[..., COL_WLAL:COL_WLAL + 2 * LORA]], axis=-1)


def kernel(x_prompt, x_sample, cache_k, cache_v, state_wkv, state_shift, state_pool, norm_g, w_in, q_norm_g, k_norm_g, attn_sinks, shift_mu, decay_w0, decay_w2, iclr_a0, iclr_a2, k_k, k_a, r_k, gn_w, gn_b, pool_w, pool_scale, w_out):
    batch, seq, _ = x_prompt.shape
    nbatch, tdec, _ = x_sample.shape
    wbuf = cache_k.shape[2]
    n_prompt, n_sample = batch * seq, nbatch * tdec
    n_total = n_prompt + n_sample

    row = lambda a: a.reshape(DEPTH, 1, -1)
    lora_pad = jnp.zeros((DEPTH, LANES - LORA, B_WIDTH), F32)
    prm = {
        "mu": row(_pad_shift(shift_mu)), "w0": row(decay_w0), "a0": row(iclr_a0),
        "w2": jnp.concatenate([decay_w2, lora_pad], axis=1).astype(BF16),
        "a2": jnp.concatenate([lora_pad[:, :LORA], iclr_a2, lora_pad[:, LORA:]], axis=1).astype(BF16),
        "k_k": row(k_k), "k_a": row(k_a), "r_k": row(r_k), "gn_w": row(gn_w), "gn_b": row(gn_b),
    }
    g_all = row(norm_g)
    qg_all = row(jnp.tile(q_norm_g, (1, A_HEADS)))
    kg_all = row(jnp.tile(k_norm_g, (1, A_KV_HEADS)))
    w_in_all = jnp.swapaxes(w_in, 1, 2)
    w_out_all = w_out.astype(BF16)
    wbd_all = _block_diag(pool_w).astype(BF16)
    scale_all = row(pool_scale)
    ck_all = cache_k.reshape(DEPTH, nbatch, wbuf, KV_WIDTH)
    cv_all = cache_v.reshape(DEPTH, nbatch, wbuf, KV_WIDTH)
    shift_all = _pad_shift(state_shift)
    hist16_all = jnp.pad(state_pool, ((0, 0), (0, 0), (1, 0), (0, 0)))

    x_p = x_prompt.reshape(n_prompt, D_MODEL)
    x_s = x_sample.reshape(n_sample, D_MODEL)
    kv_p, kv_s, wkv_p, wkv_s = [], [], [], []
    shift_p, shift_s, pool_p, pool_s = [], [], [], []
    for l in range(DEPTH):
        proj = _inproj(x_p, g_all, w_in_all, l, n_total, 0)
        proj = _inproj(x_s, g_all, w_in_all, l, n_total, n_prompt, carrier=proj)

        oa, *kv_p = _attn_prompt(proj, attn_sinks, qg_all, kg_all, l, kv_p, batch, seq)
        oa, *kv_s = _attn_sample(proj, n_prompt, oa, ck_all, cv_all, attn_sinks, qg_all, kg_all,
                                 l, kv_s, nbatch, tdec)
        ob, *wkv_p = _rwkv_prompt(proj, prm, l, wkv_p, batch, seq)
        ob, *wkv_s = _rwkv_sample(proj, n_prompt, ob, shift_all, state_wkv, prm, l, wkv_s,
                                  nbatch, tdec)
        oc = _pool_prompt(proj, wbd_all, scale_all, l, batch, seq)
        oc = _pool_sample(proj, n_prompt, oc, hist16_all, wbd_all, scale_all, l, nbatch, tdec,
                          PAST_LEN)
        x_p = _outproj(x_p, oa, ob, oc, w_out_all, l, 0)
        x_s = _outproj(x_s, oa, ob, oc, w_out_all, l, n_prompt)

        ps = proj[n_prompt:].reshape(nbatch, tdec, IN_PAD)
        shift_p.append(jnp.stack([_shift_row(proj[(b + 1) * seq - 1]) for b in range(batch)]))
        shift_s.append(_shift_row(ps[:, -1, :]))
        pool_p.append(jnp.stack([proj[(b + 1) * seq - POOL_HIST:(b + 1) * seq,
                                      COL_UC:COL_UC + C_WIDTH] for b in range(batch)]))
        pool_s.append(jnp.concatenate([state_pool[l], ps[:, :, COL_UC:COL_UC + C_WIDTH]],
                                      axis=1)[:, -POOL_HIST:])

    heads = lambda a: a.reshape(a.shape[:-1] + (A_KV_HEADS, HEAD_DIM))
    return (x_p.reshape(batch, seq, D_MODEL), x_s.reshape(nbatch, tdec, D_MODEL),
            heads(kv_p[0]), heads(kv_p[1]), wkv_p[0], jnp.stack(shift_p), jnp.stack(pool_p),
            heads(kv_s[0]), heads(kv_s[1]), wkv_s[0], jnp.stack(shift_s), jnp.stack(pool_s))
```

```python
import functools
import math

import jax
import jax.numpy as jnp
from jax import lax
from jax.experimental import pallas as pl
from jax.experimental.pallas import tpu as pltpu

F32 = jnp.float32
BF16 = jnp.bfloat16

D_MODEL = 1024
DEPTH = 4
HEAD_DIM = 64
A_HEADS = 6
A_KV_HEADS = 2
A_GROUP = A_HEADS // A_KV_HEADS
A_WIDTH = A_HEADS * HEAD_DIM
KV_WIDTH = A_KV_HEADS * HEAD_DIM
WINDOW = 128
B_HEADS = 6
B_WIDTH = B_HEADS * HEAD_DIM
LORA = 32
GN_EPS = 6.4e-4
SHIFT_WIDTH = 3 * B_WIDTH + 2 * LORA
C_WIDTH = 256
POOL_HIST = 15
NORM_EPS = 1e-6
PAST_LEN = 8192

LANES = 128
TILE = 64
RWKV_PROMPT_ROWS = 512
ATTN_PROMPT_ROWS = 256
N_PAIRS = B_HEADS // 2

COL_Q, COL_GA, COL_R, COL_KB, COL_VB, COL_GB = 0, 384, 768, 1152, 1536, 1920
COL_K, COL_V, COL_UC, COL_GC, COL_WLAL = 2304, 2432, 2560, 2816, 3072
IN_PAD = 3200
SHIFT_PAD = 3 * B_WIDTH + LANES

VMEM_LIMIT = 48 * 1024 * 1024


def _dot(a, b, prec=None):
    return jnp.dot(a, b, preferred_element_type=F32, precision=prec)


def _dot_nt(a, b, prec=None):
    return lax.dot_general(a, b, (((1,), (1,)), ((), ())),
                           preferred_element_type=F32, precision=prec)


def _dot_tn(a, b, prec=None):
    return lax.dot_general(a, b, (((0,), (0,)), ((), ())),
                           preferred_element_type=F32, precision=prec)


def _sigmoid(x):
    return 1.0 / (1.0 + jnp.exp(-x))


def _silu(x):
    return x * _sigmoid(x)


def _lane_half_mask(rows=1):
    lane = lax.broadcasted_iota(jnp.int32, (rows, LANES), 1)
    return lane < HEAD_DIM


def _head_sums(x):
    lo = _lane_half_mask()
    s0 = jnp.sum(jnp.where(lo, x, 0.0), axis=-1, keepdims=True)
    s1 = jnp.sum(jnp.where(lo, 0.0, x), axis=-1, keepdims=True)
    return jnp.where(lo, s0, s1)


def _head_sums_wide(x):
    n = x.shape[1] // LANES
    return jnp.concatenate(
        [_head_sums(x[:, j * LANES:(j + 1) * LANES]) for j in range(n)], axis=1)


def _head_rms(x, g):
    ms = _head_sums_wide(x * x) * (1.0 / HEAD_DIM)
    return x * lax.rsqrt(ms + NORM_EPS) * g


_W_SEGMENTS = ((0, 384, COL_Q), (384, 128, COL_K), (512, 128, COL_V), (640, 384, COL_GA),
               (1024, 3 * B_WIDTH, COL_R), (2176, 2 * LORA, COL_WLAL), (2240, 384, COL_GB),
               (2624, 256, COL_UC), (2880, 256, COL_GC))
IN_WIDTH = 3136


def _inproj_kernel(x_ref, g_ref, wt_ref, o_ref, wb):
    @pl.when(pl.program_id(0) == 0)
    def _():
        for src, n, dst in _W_SEGMENTS:
            wb[dst:dst + n, :] = wt_ref[src:src + n, :].astype(BF16)
        pad0 = COL_WLAL + 2 * LORA
        wb[pad0:IN_PAD, :] = jnp.zeros((IN_PAD - pad0, D_MODEL), BF16)

    x = x_ref[...]
    ms = jnp.mean(x * x, axis=-1, keepdims=True)
    h = x * lax.rsqrt(ms + NORM_EPS) * g_ref[...]
    o_ref[...] = _dot_nt(h.astype(BF16), wb[...])


def _skip(n, fn):
    def wrapped(*refs, **kw):
        return fn(*refs[n:], **kw)
    return wrapped


_CARRIER = pl.BlockSpec(memory_space=pl.ANY)


def _params(ndims):
    return pltpu.CompilerParams(dimension_semantics=("arbitrary",) * ndims,
                                vmem_limit_bytes=VMEM_LIMIT)


def _inproj(x, g_all, wt_all, l, n_total, row0, carrier=None, tm=512):
    n = x.shape[0]
    rb = row0 // tm
    car = [] if carrier is None else [carrier]
    return pl.pallas_call(
        _skip(len(car), _inproj_kernel),
        grid=(n // tm,),
        in_specs=[_CARRIER] * len(car)
        + [pl.BlockSpec((tm, D_MODEL), lambda i: (i, 0)),
           pl.BlockSpec((None, 1, D_MODEL), lambda i: (l, 0, 0)),
           pl.BlockSpec((None, IN_WIDTH, D_MODEL), lambda i: (l, 0, 0),
                        pipeline_mode=pl.Buffered(1))],
        out_specs=pl.BlockSpec((tm, IN_PAD), lambda i: (rb + i, 0)),
        out_shape=jax.ShapeDtypeStruct((n_total, IN_PAD), F32),
        scratch_shapes=[pltpu.VMEM((IN_PAD, D_MODEL), BF16)],
        input_output_aliases={0: 0} if car else {},
        compiler_params=_params(1),
        name="inproj",
    )(*car, x, g_all, wt_all)


def _outproj_kernel(x_ref, oa_ref, ob_ref, oc_ref, wa_ref, wb_ref, wc_ref, o_ref):
    acc = _dot(oa_ref[...], wa_ref[...])
    acc += _dot(ob_ref[...], wb_ref[...])
    acc += _dot(oc_ref[...], wc_ref[...])
    o_ref[...] = x_ref[...] + acc


def _outproj_pool_kernel(x_ref, oa_ref, ob_ref, u_ref, halo_ref, g_ref, wp_ref, sc_ref,
                         wa_ref, wb_ref, wc_ref, o_ref, *, tiles_per_seq):
    it = lax.rem(pl.program_id(0), tiles_per_seq)
    oc = _pool_prompt_rows(u_ref[...], halo_ref[...], g_ref[...], wp_ref[...], sc_ref[...], it)
    acc = _dot(oa_ref[...], wa_ref[...])
    acc += _dot(ob_ref[...], wb_ref[...])
    acc += _dot(oc, wc_ref[...])
    o_ref[...] = x_ref[...] + acc


def _w_out_specs(l):
    return [pl.BlockSpec((None, A_WIDTH, D_MODEL), lambda i: (l, 0, 0)),
            pl.BlockSpec((None, B_WIDTH, D_MODEL), lambda i: (l, 1, 0)),
            pl.BlockSpec((None, C_WIDTH, D_MODEL), lambda i: (l, 3, 0))]


def _outproj_prompt(x, oa, ob, proj, wp_all, sc_all, w_all, l, seq, tm=512):
    n = x.shape[0]
    halo_row = lambda i: jnp.maximum(i * (tm // 16) - 1, 0)
    return pl.pallas_call(
        functools.partial(_outproj_pool_kernel, tiles_per_seq=seq // tm),
        grid=(n // tm,),
        in_specs=[pl.BlockSpec((tm, D_MODEL), lambda i: (i, 0)),
                  pl.BlockSpec((tm, A_WIDTH), lambda i: (i, 0)),
                  pl.BlockSpec((tm, B_WIDTH), lambda i: (i, 0)),
                  pl.BlockSpec((tm, C_WIDTH), lambda i: (i, COL_UC // C_WIDTH)),
                  pl.BlockSpec((16, C_WIDTH), lambda i: (halo_row(i), COL_UC // C_WIDTH)),
                  pl.BlockSpec((tm, C_WIDTH), lambda i: (i, COL_GC // C_WIDTH)),
                  pl.BlockSpec((None, C_WIDTH, C_WIDTH), lambda i: (l, 0, 0)),
                  pl.BlockSpec((None, 1, C_WIDTH), lambda i: (l, 0, 0))] + _w_out_specs(l),
        out_specs=pl.BlockSpec((tm, D_MODEL), lambda i: (i, 0)),
        out_shape=jax.ShapeDtypeStruct((n, D_MODEL), F32),
        compiler_params=_params(1),
        name="outproj_prompt",
    )(x, oa, ob, proj, proj, proj, wp_all, sc_all, w_all, w_all, w_all)


def _outproj_sample(x, oa, ob, oc, w_all, l, row0, tm=512):
    n = x.shape[0]
    rb = row0 // tm
    return pl.pallas_call(
        _outproj_kernel,
        grid=(n // tm,),
        in_specs=[pl.BlockSpec((tm, D_MODEL), lambda i: (i, 0)),
                  pl.BlockSpec((tm, A_WIDTH), lambda i: (rb + i, 0)),
                  pl.BlockSpec((tm, B_WIDTH), lambda i: (rb + i, 0)),
                  pl.BlockSpec((tm, C_WIDTH), lambda i: (i, 0))] + _w_out_specs(l),
        out_specs=pl.BlockSpec((tm, D_MODEL), lambda i: (i, 0)),
        out_shape=jax.ShapeDtypeStruct((n, D_MODEL), F32),
        compiler_params=_params(1),
        name="outproj_sample",
    )(x, oa, ob, oc, w_all, w_all, w_all)


def _attn_prompt_kernel(sink_ref, q_ref, ga_ref, k_ref, v_ref, qg_ref, kg_ref,
                        o_ref, kl_ref, vl_ref, kprev, vprev, *, layer):
    i = pl.program_id(1)

    @pl.when(i == 0)
    def _():
        kprev[...] = jnp.zeros_like(kprev)
        vprev[...] = jnp.zeros_like(vprev)

    nblk = q_ref.shape[0] // WINDOW
    k = _head_rms(k_ref[...], kg_ref[...])
    v = v_ref[...]
    qn = (_head_rms(q_ref[...], qg_ref[...]) * (HEAD_DIM ** -0.5)).astype(BF16)
    kc = jnp.concatenate([kprev[...], k], axis=0).astype(BF16)
    vc = jnp.concatenate([vprev[...], v], axis=0).astype(BF16)

    rows = A_GROUP * WINDOW
    r = lax.broadcasted_iota(jnp.int32, (rows, 2 * WINDOW), 0)
    j = lax.broadcasted_iota(jnp.int32, (rows, 2 * WINDOW), 1)
    rel = (r & (WINDOW - 1)) + WINDOW - j
    band = (rel >= 0) & (rel < WINDOW)
    band_first = band & ((j >= WINDOW) | (i > 0))
    rcol = lax.broadcasted_iota(jnp.int32, (rows, 1), 0)

    probs = [dict(b=b, n=n) for b in range(nblk) for n in range(A_KV_HEADS)]
    for p in probs:
        b, n = p["b"], p["n"]
        heads = [A_GROUP * n + g for g in range(A_GROUP)]
        qs = jnp.concatenate([qn[b * WINDOW:(b + 1) * WINDOW, HEAD_DIM * h:HEAD_DIM * (h + 1)]
                              for h in heads], axis=0)
        keys = slice(b * WINDOW, (b + 2) * WINDOW)
        p["s"] = _dot_nt(qs, kc[keys, HEAD_DIM * n:HEAD_DIM * (n + 1)])
        p["v"] = vc[keys, HEAD_DIM * n:HEAD_DIM * (n + 1)]
        sink_col = jnp.full((rows, 1), sink_ref[layer, heads[-1]], F32)
        for g in reversed(range(A_GROUP - 1)):
            sink_col = jnp.where(rcol < (g + 1) * WINDOW, sink_ref[layer, heads[g]], sink_col)
        p["sink"] = sink_col
    for p in probs:
        sm = jnp.where(band_first if p["b"] == 0 else band, p["s"], -1e30)
        m = jnp.maximum(jnp.max(sm, axis=-1, keepdims=True), p["sink"])
        e = jnp.exp(sm - m)
        p["den"] = jnp.sum(e, axis=-1, keepdims=True) + jnp.exp(p["sink"] - m)
        p["e"] = e.astype(BF16)
    for p in probs:
        p["o"] = _dot(p["e"], p["v"]) / p["den"]
    o_rows = []
    for b in range(nblk):
        outs = []
        for p in probs[b * A_KV_HEADS:(b + 1) * A_KV_HEADS]:
            outs += [p["o"][g * WINDOW:(g + 1) * WINDOW] for g in range(A_GROUP)]
        o_rows.append(jnp.concatenate(outs, axis=1))
    o_all = jnp.concatenate(o_rows, axis=0) if nblk > 1 else o_rows[0]
    o_ref[...] = (o_all * _silu(ga_ref[...])).astype(BF16)

    last = slice((nblk - 1) * WINDOW, nblk * WINDOW)
    kprev[...] = k[last]
    vprev[...] = v[last]
    kl_ref[...] = k[last]
    vl_ref[...] = v[last]


def _attn_prompt(proj, sinks, qg_all, kg_all, l, carriers, batch, seq):
    tm = ATTN_PROMPT_ROWS
    nb = seq // tm
    n_total = proj.shape[0]
    cb = lambda c, w: c // w
    row = lambda b, i: b * nb + i
    ncar = len(carriers)
    return pl.pallas_call(
        _skip(ncar, functools.partial(_attn_prompt_kernel, layer=l)),
        grid=(batch, nb),
        in_specs=[_CARRIER] * ncar
        + [pl.BlockSpec(memory_space=pltpu.SMEM),
           pl.BlockSpec((tm, A_WIDTH), lambda b, i: (row(b, i), cb(COL_Q, A_WIDTH))),
           pl.BlockSpec((tm, A_WIDTH), lambda b, i: (row(b, i), cb(COL_GA, A_WIDTH))),
           pl.BlockSpec((tm, KV_WIDTH), lambda b, i: (row(b, i), cb(COL_K, KV_WIDTH))),
           pl.BlockSpec((tm, KV_WIDTH), lambda b, i: (row(b, i), cb(COL_V, KV_WIDTH))),
           pl.BlockSpec((None, 1, A_WIDTH), lambda b, i: (l, 0, 0)),
           pl.BlockSpec((None, 1, KV_WIDTH), lambda b, i: (l, 0, 0))],
        out_specs=[pl.BlockSpec((tm, A_WIDTH), lambda b, i: (row(b, i), 0)),
                   pl.BlockSpec((None, None, WINDOW, KV_WIDTH), lambda b, i: (l, b, 0, 0)),
                   pl.BlockSpec((None, None, WINDOW, KV_WIDTH), lambda b, i: (l, b, 0, 0))],
        out_shape=[jax.ShapeDtypeStruct((n_total, A_WIDTH), BF16),
                   jax.ShapeDtypeStruct((DEPTH, batch, WINDOW, KV_WIDTH), F32),
                   jax.ShapeDtypeStruct((DEPTH, batch, WINDOW, KV_WIDTH), F32)],
        scratch_shapes=[pltpu.VMEM((WINDOW, KV_WIDTH), F32),
                        pltpu.VMEM((WINDOW, KV_WIDTH), F32)],
        input_output_aliases={c: 1 + c for c in range(ncar)},
        compiler_params=_params(2),
        name="attn_prompt",
    )(*carriers, sinks, proj, proj, proj, proj, qg_all, kg_all)


def _attn_sample_kernel(sink_ref, q_ref, ga_ref, k_ref, v_ref, ck_ref, cv_ref, qg_ref, kg_ref,
                        o_ref, nk_ref, nv_ref, *, nseq, tdec, layer):
    k = _head_rms(k_ref[...], kg_ref[...])
    v = v_ref[...]
    qn = _head_rms(q_ref[...], qg_ref[...])
    ck = ck_ref[...]
    cv = cv_ref[...]
    k3 = k.reshape(nseq, tdec, KV_WIDTH)
    v3 = v.reshape(nseq, tdec, KV_WIDTH)
    kc = jnp.concatenate([ck, k3], axis=1)
    vc = jnp.concatenate([cv, v3], axis=1)
    nk_ref[...] = kc[:, tdec:, :]
    nv_ref[...] = vc[:, tdec:, :]
    kcb = kc.astype(BF16)
    vcb = vc.astype(BF16)

    rows = A_GROUP * tdec
    keys = WINDOW + tdec
    r = lax.broadcasted_iota(jnp.int32, (1, rows, keys), 1)
    j = lax.broadcasted_iota(jnp.int32, (1, rows, keys), 2)
    rel = WINDOW + (r & (tdec - 1)) - j
    ok = (rel >= 0) & (rel < WINDOW)
    rcol = lax.broadcasted_iota(jnp.int32, (1, rows, 1), 1)

    outs = []
    for n in range(A_KV_HEADS):
        heads = [A_GROUP * n + g for g in range(A_GROUP)]
        qs = jnp.concatenate(
            [qn[:, HEAD_DIM * h:HEAD_DIM * (h + 1)].reshape(nseq, tdec, HEAD_DIM) for h in heads],
            axis=1).astype(BF16)
        kn = kcb[:, :, HEAD_DIM * n:HEAD_DIM * (n + 1)]
        vn = vcb[:, :, HEAD_DIM * n:HEAD_DIM * (n + 1)]
        s = jnp.einsum("bqd,bkd->bqk", qs, kn, preferred_element_type=F32) * (HEAD_DIM ** -0.5)
        sink_col = jnp.full((1, rows, 1), sink_ref[layer, heads[-1]], F32)
        for g in reversed(range(A_GROUP - 1)):
            sink_col = jnp.where(rcol < (g + 1) * tdec, sink_ref[layer, heads[g]], sink_col)
        sm = jnp.where(ok, s, -1e30)
        m = jnp.maximum(jnp.max(sm, axis=-1, keepdims=True), sink_col)
        p = jnp.where(ok, jnp.exp(sm - m), 0.0)
        den = jnp.sum(p, axis=-1, keepdims=True) + jnp.exp(sink_col - m)
        o = jnp.einsum("bqk,bkd->bqd", p.astype(BF16), vn, preferred_element_type=F32) / den
        outs += [o[:, g * tdec:(g + 1) * tdec, :].reshape(nseq * tdec, HEAD_DIM)
                 for g in range(A_GROUP)]
    o_all = jnp.concatenate(outs, axis=1)
    o_ref[...] = (o_all * _silu(ga_ref[...])).astype(BF16)


def _attn_sample(proj, row0, oa, cache_k, cache_v, sinks, qg_all, kg_all, l, carriers,
                 nbatch, tdec, nseq=16):
    tm = nseq * tdec
    rb = row0 // tm
    cb = lambda c, w: c // w
    car = [oa] + list(carriers)
    cache_spec = pl.BlockSpec((None, nseq, WINDOW, KV_WIDTH), lambda s: (l, s, 0, 0))
    return pl.pallas_call(
        _skip(len(car), functools.partial(_attn_sample_kernel, nseq=nseq, tdec=tdec, layer=l)),
        grid=(nbatch // nseq,),
        in_specs=[_CARRIER] * len(car)
        + [pl.BlockSpec(memory_space=pltpu.SMEM),
           pl.BlockSpec((tm, A_WIDTH), lambda s: (rb + s, cb(COL_Q, A_WIDTH))),
           pl.BlockSpec((tm, A_WIDTH), lambda s: (rb + s, cb(COL_GA, A_WIDTH))),
           pl.BlockSpec((tm, KV_WIDTH), lambda s: (rb + s, cb(COL_K, KV_WIDTH))),
           pl.BlockSpec((tm, KV_WIDTH), lambda s: (rb + s, cb(COL_V, KV_WIDTH))),
           cache_spec, cache_spec,
           pl.BlockSpec((None, 1, A_WIDTH), lambda s: (l, 0, 0)),
           pl.BlockSpec((None, 1, KV_WIDTH), lambda s: (l, 0, 0))],
        out_specs=[pl.BlockSpec((tm, A_WIDTH), lambda s: (rb + s, 0)), cache_spec, cache_spec],
        out_shape=[jax.ShapeDtypeStruct(oa.shape, BF16),
                   jax.ShapeDtypeStruct((DEPTH, nbatch, WINDOW, KV_WIDTH), F32),
                   jax.ShapeDtypeStruct((DEPTH, nbatch, WINDOW, KV_WIDTH), F32)],
        input_output_aliases={c: c for c in range(len(car))},
        compiler_params=_params(1),
        name="attn_sample",
    )(*car, sinks, proj, proj, proj, proj, cache_k, cache_v, qg_all, kg_all)


def _stack_heads(x):
    lo = jnp.where(_lane_half_mask(), 1.0, 0.0).astype(x.dtype)
    return jnp.concatenate([x * lo, x * (1 - lo)], axis=0)


def _cat_rows(parts, rows):
    if (rows.stop - rows.start) % 16 == 0:
        return jnp.concatenate([x[rows] for x in parts], axis=0)
    return jnp.concatenate([x.astype(F32)[rows] for x in parts], axis=0).astype(BF16)


def _split3(x):
    hi = x.astype(BF16)
    r1 = x - hi.astype(F32)
    mid = r1.astype(BF16)
    lo = (r1 - mid.astype(F32)).astype(BF16)
    return hi, mid, lo


def _rwkv_masks(blk):
    lb = blk.bit_length() - 1
    t = lax.broadcasted_iota(jnp.int32, (TILE, LANES), 0)
    s = lax.broadcasted_iota(jnp.int32, (TILE, LANES), 1) & (TILE - 1)
    same = (t >> lb) == (s >> lb)
    levels = [((t >> (l + 1)) == (s >> (l + 1))) & (((t >> l) & 1) == 1) & (((s >> l) & 1) == 0)
              for l in range(lb)]
    tt = lax.broadcasted_iota(jnp.int32, (2 * TILE, 3 * TILE), 0)
    ss = lax.broadcasted_iota(jnp.int32, (2 * TILE, 3 * TILE), 1)
    ti = tt & (TILE - 1)
    si = jnp.where(ss >= 2 * TILE, ss - 2 * TILE, jnp.where(ss >= TILE, ss - TILE, ss))
    sel = ((ti >> lb) == (si >> lb)) & ((tt >= TILE) | (si <= ti))
    r2 = lax.broadcasted_iota(jnp.int32, (LANES, LANES), 0)
    c2 = lax.broadcasted_iota(jnp.int32, (LANES, LANES), 1)
    return dict(lb=lb, strict=same & (s < t), incl=same & (s <= t),
                eye=jnp.where(s == t, 1.0, 0.0), levels=levels,
                cumsel=jnp.where(sel, 1.0, 0.0).astype(BF16),
                same_head=(r2 >> 6) == (c2 >> 6))


def _rwkv_tile(xr, xk, xv, xw, gate, prm, states, blk):
    nchunk = xr.shape[0] // TILE
    nseq = TILE // blk
    mk = _rwkv_masks(blk)
    lw = _dot(jnp.tanh(xw).astype(BF16), prm["w2"])
    la = _dot(xw.astype(BF16), prm["a2"])
    ld = (-math.exp(-0.5)) * _sigmoid(prm["w0"] + lw)
    a = _sigmoid(prm["a0"] + la)
    kkr = xk * prm["k_k"]
    kk = kkr / jnp.maximum(jnp.sqrt(_head_sums_wide(kkr * kkr)), 1e-12)
    kmod = xk * (1.0 + (a - 1.0) * prm["k_a"])
    ka = kk * a

    probs = []
    for ch in range(nchunk):
        rows = slice(ch * TILE, (ch + 1) * TILE)
        cum = _dot(mk["cumsel"], jnp.concatenate(_split3(ld[rows]), axis=0))
        c, cl = cum[:TILE], cum[TILE:]
        e_c = jnp.exp(-c)
        e_l = jnp.exp(cl - c)
        full = dict(rt=(xr[rows] * jnp.exp(c)).astype(BF16),
                    at=(-kk[rows] * jnp.exp(c - ld[rows])).astype(BF16),
                    bt=(ka[rows] * e_c).astype(BF16), kt=(kmod[rows] * e_c).astype(BF16),
                    bh=(ka[rows] * e_l).astype(BF16), kh=(kmod[rows] * e_l).astype(BF16),
                    v=xv[rows].astype(BF16), decay=jnp.exp(cl))
        for j in range(N_PAIRS):
            sl = slice(j * LANES, (j + 1) * LANES)
            probs.append(dict(ch=ch, j=j, **{n: x[:, sl] for n, x in full.items()}))

    for p in probs:
        p["v_s"] = _stack_heads(p["v"])
        z_s = jnp.concatenate([_stack_heads(p["bt"]), _stack_heads(p["kt"])], axis=0)
        gram = _dot_nt(jnp.concatenate([p["at"], p["rt"]], axis=0), z_s)
        p["a_ab"] = jnp.where(mk["strict"], gram[:TILE, :LANES], 0.0)
        p["a_ak"] = jnp.where(mk["strict"], gram[:TILE, LANES:], 0.0).astype(BF16)
        p["a_r"] = jnp.concatenate([jnp.where(mk["incl"], gram[TILE:, :LANES], 0.0),
                                    jnp.where(mk["incl"], gram[TILE:, LANES:], 0.0)],
                                   axis=1).astype(BF16)
        p["tinv"] = mk["eye"]
        if mk["lb"] > 0:
            p["tinv"] = p["tinv"] + jnp.where(mk["levels"][0], p["a_ab"], 0.0)

    for l in range(1, mk["lb"]):
        for p in probs:
            p["tb"] = p["tinv"].astype(BF16)
            mid = _stack_heads(jnp.where(mk["levels"][l], p["a_ab"], 0.0).astype(BF16))
            p["half"] = _dot(p["tb"], mid).astype(BF16)
        for p in probs:
            p["tinv"] = p["tinv"] + _dot(p["half"], _stack_heads(p["tb"]))

    for p in probs:
        p["akv"] = _dot(p["a_ak"], p["v_s"]).astype(BF16)
    for p in probs:
        p["wu"] = _dot(p["tinv"].astype(BF16),
                       jnp.concatenate([_stack_heads(p["at"]), _stack_heads(p["akv"])], axis=1))
        p["w"], p["u0"] = p["wu"][:, :LANES].astype(BF16), p["wu"][:, LANES:]

    carried = nseq == 1
    if carried:
        for p in probs:
            p["wtb"] = _dot_tn(p["wu"].astype(BF16), p["bh"])
        for p in probs:
            vtk = _dot_tn(p["v"], p["kh"])
            p["m"] = jnp.where(mk["same_head"], p["wtb"][:LANES], 0.0).astype(BF16)
            p["c"] = jnp.where(mk["same_head"], p["wtb"][LANES:] + vtk, 0.0)

    def read_state(p, sts):
        u_parts, rs_parts = [], []
        for q in range(nseq):
            rows = slice(q * blk, (q + 1) * blk)
            res = _dot_nt(_cat_rows([p["w"], p["rt"]], rows), sts[q].astype(BF16))
            u_parts.append(res[:blk] + p["u0"][rows])
            rs_parts.append(res[blk:])
        p["u"] = (jnp.concatenate(u_parts, axis=0) if nseq > 1 else u_parts[0]).astype(BF16)
        p["rs"] = jnp.concatenate(rs_parts, axis=0) if nseq > 1 else rs_parts[0]

    states = [list(st) for st in states]
    y_rows = []
    for ch in range(nchunk):
        cps = [p for p in probs if p["ch"] == ch]
        start = [list(states[p["j"]]) for p in cps]
        if carried:
            for p in cps:
                s = states[p["j"]][0]
                states[p["j"]][0] = s * p["decay"][0:1] + _dot(s.astype(BF16), p["m"]) + p["c"]
        for p, sts in zip(cps, start):
            read_state(p, sts)
        if not carried:
            for p in cps:
                for q in range(nseq):
                    rows = slice(q * blk, (q + 1) * blk)
                    upd = _dot_tn(_cat_rows([p["u"], p["v"]], rows),
                                  _cat_rows([p["bh"], p["kh"]], rows))
                    states[p["j"]][q] = (states[p["j"]][q] * p["decay"][q * blk:q * blk + 1]
                                         + jnp.where(mk["same_head"], upd, 0.0))
        y_rows.append(jnp.concatenate(
            [_dot(p["a_r"], jnp.concatenate([_stack_heads(p["u"]), p["v_s"]], axis=0)) + p["rs"]
             for p in cps], axis=1))
    y = jnp.concatenate(y_rows, axis=0) if nchunk > 1 else y_rows[0]

    mean = _head_sums_wide(y) * (1.0 / HEAD_DIM)
    yc = y - mean
    var = _head_sums_wide(yc * yc) * (1.0 / HEAD_DIM)
    yn = yc * lax.rsqrt(var + GN_EPS) * prm["gn_w"] + prm["gn_b"]
    yn = yn + _head_sums_wide(xr * kmod * prm["r_k"]) * xv
    return (yn * _silu(gate)).astype(BF16), states


_RWKV_PARAM_NAMES = ("mu", "w0", "w2", "a0", "a2", "k_k", "k_a", "r_k", "gn_w", "gn_b")


def _rwkv_shift_mix(cur, prev_rows, first_row_mask, mu):
    shifted = jnp.where(first_row_mask, prev_rows, pltpu.roll(cur, 1, axis=0))
    return cur + (shifted - cur) * mu


def _rwkv_load_params(refs):
    prm = {n: r[...] for n, r in zip(_RWKV_PARAM_NAMES, refs)}
    mu = prm.pop("mu")
    return prm, (mu[:, 0:B_WIDTH], mu[:, B_WIDTH:2 * B_WIDTH],
                 mu[:, 2 * B_WIDTH:3 * B_WIDTH], mu[:, 3 * B_WIDTH:])


def _rwkv_prompt_kernel(r_ref, k_ref, v_ref, g_ref, w_ref, *rest):
    prm_refs, (o_ref, s_out_ref, state, prev) = rest[:len(_RWKV_PARAM_NAMES)], rest[len(_RWKV_PARAM_NAMES):]
    i = pl.program_id(1)

    @pl.when(i == 0)
    def _():
        state[...] = jnp.zeros_like(state)
        prev[...] = jnp.zeros_like(prev)

    prm, mus = _rwkv_load_params(prm_refs)
    cur = (r_ref[...], k_ref[...], v_ref[...], w_ref[...])
    rows = cur[0].shape[0]
    first = lax.broadcasted_iota(jnp.int32, (rows, 1), 0) == 0
    offs = (0, B_WIDTH, 2 * B_WIDTH, 3 * B_WIDTH, SHIFT_PAD)
    mixed = []
    for n, x in enumerate(cur):
        prev_row = jnp.broadcast_to(prev[0:1, offs[n]:offs[n + 1]], x.shape)
        mixed.append(_rwkv_shift_mix(x, prev_row, first, mus[n]))
        prev[0:1, offs[n]:offs[n + 1]] = x[rows - 1:rows, :]
    states = [[state[j]] for j in range(N_PAIRS)]
    out, new_states = _rwkv_tile(*mixed, g_ref[...], prm, states, TILE)
    o_ref[...] = out
    for j in range(N_PAIRS):
        s = new_states[j][0]
        state[j] = s
        s_out_ref[2 * j] = s[:HEAD_DIM, :HEAD_DIM]
        s_out_ref[2 * j + 1] = s[HEAD_DIM:, HEAD_DIM:]


def _rwkv_sample_kernel(r_ref, k_ref, v_ref, g_ref, w_ref, sh_ref, s_in_ref, *rest, tdec):
    prm_refs, (o_ref, s_out_ref) = rest[:len(_RWKV_PARAM_NAMES)], rest[len(_RWKV_PARAM_NAMES):]
    nseq = TILE // tdec
    prm, mus = _rwkv_load_params(prm_refs)
    cur = (r_ref[...], k_ref[...], v_ref[...], w_ref[...])
    first = (lax.broadcasted_iota(jnp.int32, (TILE, 1), 0) & (tdec - 1)) == 0
    offs = (0, B_WIDTH, 2 * B_WIDTH, 3 * B_WIDTH, SHIFT_PAD)
    sh = sh_ref[...]
    mixed = []
    for n, x in enumerate(cur):
        prev_rows = jnp.concatenate(
            [jnp.broadcast_to(sh[q:q + 1, offs[n]:offs[n + 1]], (tdec, x.shape[1]))
             for q in range(nseq)], axis=0)
        mixed.append(_rwkv_shift_mix(x, prev_rows, first, mus[n]))
    zero = jnp.zeros((HEAD_DIM, HEAD_DIM), F32)
    states = []
    for j in range(N_PAIRS):
        pair = []
        for q in range(nseq):
            top = jnp.concatenate([s_in_ref[q, 2 * j], zero], axis=1)
            bot = jnp.concatenate([zero, s_in_ref[q, 2 * j + 1]], axis=1)
            pair.append(jnp.concatenate([top, bot], axis=0))
        states.append(pair)
    out, new_states = _rwkv_tile(*mixed, g_ref[...], prm, states, tdec)
    o_ref[...] = out
    for j in range(N_PAIRS):
        for q in range(nseq):
            s = new_states[j][q]
            s_out_ref[q, 2 * j] = s[:HEAD_DIM, :HEAD_DIM]
            s_out_ref[q, 2 * j + 1] = s[HEAD_DIM:, HEAD_DIM:]


def _rwkv_param_specs(l):
    shapes = {"mu": (None, 1, SHIFT_PAD), "w2": (None, LANES, B_WIDTH), "a2": (None, LANES, B_WIDTH)}
    return [pl.BlockSpec(shapes.get(n, (None, 1, B_WIDTH)), lambda *_: (l, 0, 0))
            for n in _RWKV_PARAM_NAMES]


def _rwkv_col_specs(row_fn, rows):
    cols = (COL_R, COL_KB, COL_VB, COL_GB)
    specs = [pl.BlockSpec((rows, B_WIDTH), (lambda *a, c=c: (row_fn(*a), c // B_WIDTH))) for c in cols]
    specs.append(pl.BlockSpec((rows, LANES), lambda *a: (row_fn(*a), COL_WLAL // LANES)))
    return specs


def _rwkv_prompt(proj, prm, l, carriers, batch, seq, rows=RWKV_PROMPT_ROWS):
    nt = seq // rows
    row_fn = lambda b, i: b * nt + i
    ncar = len(carriers)
    return pl.pallas_call(
        _skip(ncar, _rwkv_prompt_kernel),
        grid=(batch, nt),
        in_specs=[_CARRIER] * ncar + _rwkv_col_specs(row_fn, rows) + _rwkv_param_specs(l),
        out_specs=[pl.BlockSpec((rows, B_WIDTH), lambda b, i: (row_fn(b, i), 0)),
                   pl.BlockSpec((None, None, B_HEADS, HEAD_DIM, HEAD_DIM),
                                lambda b, i: (l, b, 0, 0, 0))],
        out_shape=[jax.ShapeDtypeStruct((proj.shape[0], B_WIDTH), BF16),
                   jax.ShapeDtypeStruct((DEPTH, batch, B_HEADS, HEAD_DIM, HEAD_DIM), F32)],
        scratch_shapes=[pltpu.VMEM((N_PAIRS, LANES, LANES), F32),
                        pltpu.VMEM((8, SHIFT_PAD), F32)],
        input_output_aliases={c: 1 + c for c in range(ncar)},
        compiler_params=_params(2),
        name="rwkv_prompt",
    )(*carriers, proj, proj, proj, proj, proj, *[prm[n] for n in _RWKV_PARAM_NAMES])


def _rwkv_sample(proj, row0, ob, shift_in, state_in, prm, l, carriers, nbatch, tdec):
    nseq = TILE // tdec
    rb = row0 // TILE
    row_fn = lambda s: rb + s
    car = [ob] + list(carriers)
    state_spec = pl.BlockSpec((None, nseq, B_HEADS, HEAD_DIM, HEAD_DIM), lambda s: (l, s, 0, 0, 0))
    return pl.pallas_call(
        _skip(len(car), functools.partial(_rwkv_sample_kernel, tdec=tdec)),
        grid=(nbatch // nseq,),
        in_specs=[_CARRIER] * len(car) + _rwkv_col_specs(row_fn, TILE)
        + [pl.BlockSpec((None, nseq, SHIFT_PAD), lambda s: (l, s, 0)), state_spec]
        + _rwkv_param_specs(l),
        out_specs=[pl.BlockSpec((TILE, B_WIDTH), lambda s: (rb + s, 0)), state_spec],
        out_shape=[jax.ShapeDtypeStruct(ob.shape, BF16),
                   jax.ShapeDtypeStruct((DEPTH, nbatch, B_HEADS, HEAD_DIM, HEAD_DIM), F32)],
        input_output_aliases={c: c for c in range(len(car))},
        compiler_params=_params(1),
        name="rwkv_sample",
    )(*car, proj, proj, proj, proj, proj, shift_in, state_in,
      *[prm[n] for n in _RWKV_PARAM_NAMES])


def _window_sums(xe):
    s2 = xe + pltpu.roll(xe, 1, axis=0)
    s4 = s2 + pltpu.roll(s2, 2, axis=0)
    s8 = s4 + pltpu.roll(s4, 4, axis=0)
    s16 = s8 + pltpu.roll(s8, 8, axis=0)
    lane = lax.broadcasted_iota(jnp.int32, (1, C_WIDTH), 1)
    return jnp.where(lane < 64, s2, jnp.where(lane < 128, s4, jnp.where(lane < 192, s8, s16)))


def _pool_window_lane():
    lane = lax.broadcasted_iota(jnp.int32, (1, C_WIDTH), 1)
    return jnp.where(lane < 64, 2, jnp.where(lane < 128, 4, jnp.where(lane < 192, 8, 16)))


def _pool_prompt_rows(u, halo, gate, wbd, scale, it):
    tm = u.shape[0]
    halo = jnp.where(it > 0, halo, 0.0)
    sums = _window_sums(jnp.concatenate([halo, u], axis=0))[16:]
    pos = it * tm + lax.broadcasted_iota(jnp.int32, (tm, 1), 0)
    cnt = jnp.minimum(_pool_window_lane(), pos + 1).astype(F32)
    d = sums / cnt - u
    y = _dot(d.astype(BF16), wbd) * scale
    return (y * _silu(gate)).astype(BF16)


def _pool_sample_kernel(u_ref, h_ref, g_ref, w_ref, sc_ref, o_ref, *, nseq, tdec, pos0):
    u = u_ref[...]
    hist = h_ref[...]
    xe = jnp.concatenate([hist, u.reshape(nseq, tdec, C_WIDTH)], axis=1)
    xe = xe.reshape(nseq * (16 + tdec), C_WIDTH)
    sums = _window_sums(xe).reshape(nseq, 16 + tdec, C_WIDTH)[:, 16:, :]
    sums = sums.reshape(nseq * tdec, C_WIDTH)
    t = lax.broadcasted_iota(jnp.int32, (nseq * tdec, 1), 0) & (tdec - 1)
    cnt = jnp.minimum(_pool_window_lane(), pos0 + t + 1).astype(F32)
    d = sums / cnt - u
    y = _dot(d.astype(BF16), w_ref[...]) * sc_ref[...]
    o_ref[...] = (y * _silu(g_ref[...])).astype(BF16)


def _pool_sample(proj, row0, hist16_all, wbd_all, scale_all, l, nbatch, tdec, pos0, nseq=64):
    tm = nseq * tdec
    rb = row0 // tm
    return pl.pallas_call(
        functools.partial(_pool_sample_kernel, nseq=nseq, tdec=tdec, pos0=pos0),
        grid=(nbatch // nseq,),
        in_specs=[pl.BlockSpec((tm, C_WIDTH), lambda s: (rb + s, COL_UC // C_WIDTH)),
                  pl.BlockSpec((None, nseq, 16, C_WIDTH), lambda s: (l, s, 0, 0)),
                  pl.BlockSpec((tm, C_WIDTH), lambda s: (rb + s, COL_GC // C_WIDTH)),
                  pl.BlockSpec((None, C_WIDTH, C_WIDTH), lambda s: (l, 0, 0)),
                  pl.BlockSpec((None, 1, C_WIDTH), lambda s: (l, 0, 0))],
        out_specs=pl.BlockSpec((tm, C_WIDTH), lambda s: (s, 0)),
        out_shape=jax.ShapeDtypeStruct((nbatch * tdec, C_WIDTH), BF16),
        compiler_params=_params(1),
        name="pool_sample",
    )(proj, hist16_all, proj, wbd_all, scale_all)


def _pad_shift(x):
    pad = jnp.zeros(x.shape[:-1] + (LANES - 2 * LORA,), x.dtype)
    return jnp.concatenate([x, pad], axis=-1)


def _block_diag(w):
    n, g, c, _ = w.shape
    eye = jnp.eye(g, dtype=w.dtype)
    return (eye[None, :, None, :, None] * w[:, :, :, None, :]).reshape(n, g * c, g * c)


def _shift_row(rows):
    return jnp.concatenate([rows[..., COL_R:COL_R + 3 * B_WIDTH],
                            rows[..., COL_WLAL:COL_WLAL + 2 * LORA]], axis=-1)


def kernel(x_prompt, x_sample, cache_k, cache_v, state_wkv, state_shift, state_pool, norm_g, w_in, q_norm_g, k_norm_g, attn_sinks, shift_mu, decay_w0, decay_w2, iclr_a0, iclr_a2, k_k, k_a, r_k, gn_w, gn_b, pool_w, pool_scale, w_out):
    batch, seq, _ = x_prompt.shape
    nbatch, tdec, _ = x_sample.shape
    wbuf = cache_k.shape[2]
    n_prompt, n_sample = batch * seq, nbatch * tdec
    n_total = n_prompt + n_sample

    row = lambda a: a.reshape(DEPTH, 1, -1)
    lora_pad = jnp.zeros((DEPTH, LANES - LORA, B_WIDTH), F32)
    prm = {
        "mu": row(_pad_shift(shift_mu)), "w0": row(decay_w0), "a0": row(iclr_a0),
        "w2": jnp.concatenate([decay_w2, lora_pad], axis=1).astype(BF16),
        "a2": jnp.concatenate([lora_pad[:, :LORA], iclr_a2, lora_pad[:, LORA:]], axis=1).astype(BF16),
        "k_k": row(k_k), "k_a": row(k_a), "r_k": row(r_k), "gn_w": row(gn_w), "gn_b": row(gn_b),
    }
    g_all = row(norm_g)
    qg_all = row(jnp.tile(q_norm_g, (1, A_HEADS)))
    kg_all = row(jnp.tile(k_norm_g, (1, A_KV_HEADS)))
    w_in_all = jnp.swapaxes(w_in, 1, 2)
    w_out_all = w_out.astype(BF16)
    wbd_all = _block_diag(pool_w).astype(BF16)
    scale_all = row(pool_scale)
    ck_all = cache_k.reshape(DEPTH, nbatch, wbuf, KV_WIDTH)
    cv_all = cache_v.reshape(DEPTH, nbatch, wbuf, KV_WIDTH)
    shift_all = _pad_shift(state_shift)
    hist16_all = jnp.pad(state_pool, ((0, 0), (0, 0), (1, 0), (0, 0)))

    x_p = x_prompt.reshape(n_prompt, D_MODEL)
    x_s = x_sample.reshape(n_sample, D_MODEL)
    kv_p, kv_s, wkv_p, wkv_s = [], [], [], []
    shift_p, shift_s, pool_p, pool_s = [], [], [], []
    for l in range(DEPTH):
        proj = _inproj(x_p, g_all, w_in_all, l, n_total, 0)
        proj = _inproj(x_s, g_all, w_in_all, l, n_total, n_prompt, carrier=proj)

        oa, *kv_p = _attn_prompt(proj, attn_sinks, qg_all, kg_all, l, kv_p, batch, seq)
        oa, *kv_s = _attn_sample(proj, n_prompt, oa, ck_all, cv_all, attn_sinks, qg_all, kg_all,
                                 l, kv_s, nbatch, tdec)
        ob, *wkv_p = _rwkv_prompt(proj, prm, l, wkv_p, batch, seq)
        ob, *wkv_s = _rwkv_sample(proj, n_prompt, ob, shift_all, state_wkv, prm, l, wkv_s,
                                  nbatch, tdec)
        oc_s = _pool_sample(proj, n_prompt, hist16_all, wbd_all, scale_all, l, nbatch, tdec,
                            PAST_LEN)
        x_p = _outproj_prompt(x_p, oa, ob, proj, wbd_all, scale_all, w_out_all, l, seq)
        x_s = _outproj_sample(x_s, oa, ob, oc_s, w_out_all, l, n_prompt)

        shift_p.append(jnp.stack([_shift_row(proj[(b + 1) * seq - 1]) for b in range(batch)]))
        shift_s.append(_shift_row(proj[n_prompt + tdec - 1::tdec]))
        pool_p.append(jnp.stack([proj[(b + 1) * seq - POOL_HIST:(b + 1) * seq,
                                      COL_UC:COL_UC + C_WIDTH] for b in range(batch)]))
        u_s = proj[n_prompt:, COL_UC:COL_UC + C_WIDTH].reshape(nbatch, tdec, C_WIDTH)
        pool_s.append(jnp.concatenate([state_pool[l], u_s], axis=1)[:, -POOL_HIST:])

    heads = lambda a: a.reshape(a.shape[:-1] + (A_KV_HEADS, HEAD_DIM))
    return (x_p.reshape(batch, seq, D_MODEL), x_s.reshape(nbatch, tdec, D_MODEL),
            heads(kv_p[0]), heads(kv_p[1]), wkv_p[0], jnp.stack(shift_p), jnp.stack(pool_p),
            heads(kv_s[0]), heads(kv_s[1]), wkv_s[0], jnp.stack(shift_s), jnp.stack(pool_s))
```

```python
import functools
import math

import jax
import jax.numpy as jnp
from jax import lax
from jax.experimental import pallas as pl
from jax.experimental.pallas import tpu as pltpu

F32 = jnp.float32
BF16 = jnp.bfloat16

D_MODEL = 1024
DEPTH = 4
HEAD_DIM = 64
A_HEADS = 6
A_KV_HEADS = 2
A_GROUP = A_HEADS // A_KV_HEADS
A_WIDTH = A_HEADS * HEAD_DIM
KV_WIDTH = A_KV_HEADS * HEAD_DIM
WINDOW = 128
B_HEADS = 6
B_WIDTH = B_HEADS * HEAD_DIM
LORA = 32
GN_EPS = 6.4e-4
SHIFT_WIDTH = 3 * B_WIDTH + 2 * LORA
C_WIDTH = 256
POOL_HIST = 15
NORM_EPS = 1e-6
PAST_LEN = 8192

LANES = 128
TILE = 64
RWKV_PROMPT_ROWS = 512
ATTN_PROMPT_ROWS = 512
N_PAIRS = B_HEADS // 2

COL_Q, COL_GA, COL_R, COL_KB, COL_VB, COL_GB = 0, 384, 768, 1152, 1536, 1920
COL_K, COL_V, COL_UC, COL_GC, COL_WLAL = 2304, 2432, 2560, 2816, 3072
IN_PAD = 3200
SHIFT_PAD = 3 * B_WIDTH + LANES

VMEM_LIMIT = 48 * 1024 * 1024
VMEM_LIMIT_FUSED = 58 * 1024 * 1024


def _dot(a, b, prec=None):
    return jnp.dot(a, b, preferred_element_type=F32, precision=prec)


def _dot_nt(a, b, prec=None):
    return lax.dot_general(a, b, (((1,), (1,)), ((), ())),
                           preferred_element_type=F32, precision=prec)


def _dot_tn(a, b, prec=None):
    return lax.dot_general(a, b, (((0,), (0,)), ((), ())),
                           preferred_element_type=F32, precision=prec)


def _sigmoid(x):
    return 1.0 / (1.0 + jnp.exp(-x))


def _silu(x):
    return x * _sigmoid(x)


def _lane_half_mask(rows=1):
    lane = lax.broadcasted_iota(jnp.int32, (rows, LANES), 1)
    return lane < HEAD_DIM


def _head_sums(x):
    lo = _lane_half_mask()
    s0 = jnp.sum(jnp.where(lo, x, 0.0), axis=-1, keepdims=True)
    s1 = jnp.sum(jnp.where(lo, 0.0, x), axis=-1, keepdims=True)
    return jnp.where(lo, s0, s1)


def _head_sums_wide(x):
    n = x.shape[1] // LANES
    return jnp.concatenate(
        [_head_sums(x[:, j * LANES:(j + 1) * LANES]) for j in range(n)], axis=1)


def _head_rms(x, g):
    ms = _head_sums_wide(x * x) * (1.0 / HEAD_DIM)
    return x * lax.rsqrt(ms + NORM_EPS) * g


_W_SEGMENTS = ((0, 384, COL_Q), (384, 128, COL_K), (512, 128, COL_V), (640, 384, COL_GA),
               (1024, 3 * B_WIDTH, COL_R), (2176, 2 * LORA, COL_WLAL), (2240, 384, COL_GB),
               (2624, 256, COL_UC), (2880, 256, COL_GC))
IN_WIDTH = 3136


def _regroup_w_in(wt_ref, wb):
    @pl.when(pl.program_id(0) == 0)
    def _():
        for src, n, dst in _W_SEGMENTS:
            wb[dst:dst + n, :] = wt_ref[src:src + n, :].astype(BF16)
        pad0 = COL_WLAL + 2 * LORA
        wb[pad0:IN_PAD, :] = jnp.zeros((IN_PAD - pad0, D_MODEL), BF16)


def _norm_project(x, g, wb):
    ms = jnp.mean(x * x, axis=-1, keepdims=True)
    h = x * lax.rsqrt(ms + NORM_EPS) * g
    return _dot_nt(h.astype(BF16), wb[...])


def _inproj_kernel(x_ref, g_ref, wt_ref, o_ref, wb):
    _regroup_w_in(wt_ref, wb)
    o_ref[...] = _norm_project(x_ref[...], g_ref[...], wb)


def _skip(n, fn):
    def wrapped(*refs, **kw):
        return fn(*refs[n:], **kw)
    return wrapped


_CARRIER = pl.BlockSpec(memory_space=pl.ANY)


def _params(ndims):
    return pltpu.CompilerParams(dimension_semantics=("arbitrary",) * ndims,
                                vmem_limit_bytes=VMEM_LIMIT)


def _inproj(x, g_all, wt_all, l, n_total, row0, carrier=None, tm=512):
    n = x.shape[0]
    rb = row0 // tm
    car = [] if carrier is None else [carrier]
    return pl.pallas_call(
        _skip(len(car), _inproj_kernel),
        grid=(n // tm,),
        in_specs=[_CARRIER] * len(car)
        + [pl.BlockSpec((tm, D_MODEL), lambda i: (i, 0)),
           pl.BlockSpec((None, 1, D_MODEL), lambda i: (l, 0, 0)),
           pl.BlockSpec((None, IN_WIDTH, D_MODEL), lambda i: (l, 0, 0),
                        pipeline_mode=pl.Buffered(1))],
        out_specs=pl.BlockSpec((tm, IN_PAD), lambda i: (rb + i, 0)),
        out_shape=jax.ShapeDtypeStruct((n_total, IN_PAD), F32),
        scratch_shapes=[pltpu.VMEM((IN_PAD, D_MODEL), BF16)],
        input_output_aliases={0: 0} if car else {},
        compiler_params=_params(1),
        name="inproj",
    )(*car, x, g_all, wt_all)


def _outproj_kernel(x_ref, oa_ref, ob_ref, oc_ref, wa_ref, wb_ref, wc_ref, o_ref):
    acc = _dot(oa_ref[...], wa_ref[...])
    acc += _dot(ob_ref[...], wb_ref[...])
    acc += _dot(oc_ref[...], wc_ref[...])
    o_ref[...] = x_ref[...] + acc


def _outproj_pool_rows(x_ref, oa_ref, ob_ref, u_ref, halo_ref, g_ref, wp_ref, sc_ref,
                       wa_ref, wb_ref, wc_ref, tiles_per_seq):
    it = lax.rem(pl.program_id(0), tiles_per_seq)
    oc = _pool_prompt_rows(u_ref[...], halo_ref[...], g_ref[...], wp_ref[...], sc_ref[...], it)
    acc = _dot(oa_ref[...], wa_ref[...])
    acc += _dot(ob_ref[...], wb_ref[...])
    acc += _dot(oc, wc_ref[...])
    return x_ref[...] + acc


def _outproj_pool_kernel(*refs, tiles_per_seq):
    *ins, o_ref = refs
    o_ref[...] = _outproj_pool_rows(*ins, tiles_per_seq)


def _outproj_inproj_kernel(*refs, tiles_per_seq):
    *ins, ng_ref, wt_ref, xo_ref, po_ref, wb = refs
    _regroup_w_in(wt_ref, wb)
    x_new = _outproj_pool_rows(*ins, tiles_per_seq)
    xo_ref[...] = x_new
    po_ref[...] = _norm_project(x_new, ng_ref[...], wb)


def _w_out_specs(l):
    return [pl.BlockSpec((None, A_WIDTH, D_MODEL), lambda i: (l, 0, 0)),
            pl.BlockSpec((None, B_WIDTH, D_MODEL), lambda i: (l, 1, 0)),
            pl.BlockSpec((None, C_WIDTH, D_MODEL), lambda i: (l, 3, 0))]


def _outproj_prompt(x, oa, ob, proj, wp_all, sc_all, w_all, l, seq, next_in=None, tm=512):
    n = x.shape[0]
    halo_row = lambda i: jnp.maximum(i * (tm // 16) - 1, 0)
    in_specs = [pl.BlockSpec((tm, D_MODEL), lambda i: (i, 0)),
                pl.BlockSpec((tm, A_WIDTH), lambda i: (i, 0)),
                pl.BlockSpec((tm, B_WIDTH), lambda i: (i, 0)),
                pl.BlockSpec((tm, C_WIDTH), lambda i: (i, COL_UC // C_WIDTH)),
                pl.BlockSpec((16, C_WIDTH), lambda i: (halo_row(i), COL_UC // C_WIDTH)),
                pl.BlockSpec((tm, C_WIDTH), lambda i: (i, COL_GC // C_WIDTH)),
                pl.BlockSpec((None, C_WIDTH, C_WIDTH), lambda i: (l, 0, 0)),
                pl.BlockSpec((None, 1, C_WIDTH), lambda i: (l, 0, 0))] + _w_out_specs(l)
    args = [x, oa, ob, proj, proj, proj, wp_all, sc_all, w_all, w_all, w_all]
    x_spec = pl.BlockSpec((tm, D_MODEL), lambda i: (i, 0))
    x_shape = jax.ShapeDtypeStruct((n, D_MODEL), F32)
    if next_in is None:
        return pl.pallas_call(
            functools.partial(_outproj_pool_kernel, tiles_per_seq=seq // tm),
            grid=(n // tm,), in_specs=in_specs, out_specs=x_spec, out_shape=x_shape,
            compiler_params=_params(1), name="outproj_prompt")(*args)
    g_all, wt_all, n_total = next_in
    return pl.pallas_call(
        functools.partial(_outproj_inproj_kernel, tiles_per_seq=seq // tm),
        grid=(n // tm,),
        in_specs=in_specs
        + [pl.BlockSpec((None, 1, D_MODEL), lambda i: (l + 1, 0, 0)),
           pl.BlockSpec((None, IN_WIDTH, D_MODEL), lambda i: (l + 1, 0, 0),
                        pipeline_mode=pl.Buffered(1))],
        out_specs=[x_spec, pl.BlockSpec((tm, IN_PAD), lambda i: (i, 0))],
        out_shape=[x_shape, jax.ShapeDtypeStruct((n_total, IN_PAD), F32)],
        scratch_shapes=[pltpu.VMEM((IN_PAD, D_MODEL), BF16)],
        compiler_params=pltpu.CompilerParams(dimension_semantics=("arbitrary",),
                                             vmem_limit_bytes=VMEM_LIMIT_FUSED),
        name="outproj_inproj",
    )(*args, g_all, wt_all)


def _outproj_sample(x, oa, ob, oc, w_all, l, row0, tm=512):
    n = x.shape[0]
    rb = row0 // tm
    return pl.pallas_call(
        _outproj_kernel,
        grid=(n // tm,),
        in_specs=[pl.BlockSpec((tm, D_MODEL), lambda i: (i, 0)),
                  pl.BlockSpec((tm, A_WIDTH), lambda i: (rb + i, 0)),
                  pl.BlockSpec((tm, B_WIDTH), lambda i: (rb + i, 0)),
                  pl.BlockSpec((tm, C_WIDTH), lambda i: (i, 0))] + _w_out_specs(l),
        out_specs=pl.BlockSpec((tm, D_MODEL), lambda i: (i, 0)),
        out_shape=jax.ShapeDtypeStruct((n, D_MODEL), F32),
        compiler_params=_params(1),
        name="outproj_sample",
    )(x, oa, ob, oc, w_all, w_all, w_all)


def _attn_prompt_kernel(sink_ref, q_ref, ga_ref, k_ref, v_ref, qg_ref, kg_ref,
                        o_ref, kl_ref, vl_ref, kprev, vprev, *, layer):
    i = pl.program_id(1)

    @pl.when(i == 0)
    def _():
        kprev[...] = jnp.zeros_like(kprev)
        vprev[...] = jnp.zeros_like(vprev)

    nblk = q_ref.shape[0] // WINDOW
    k = _head_rms(k_ref[...], kg_ref[...])
    v = v_ref[...]
    qn = (_head_rms(q_ref[...], qg_ref[...]) * (HEAD_DIM ** -0.5)).astype(BF16)
    kc = jnp.concatenate([kprev[...], k], axis=0).astype(BF16)
    vc = jnp.concatenate([vprev[...], v], axis=0).astype(BF16)

    rows = A_GROUP * WINDOW
    r = lax.broadcasted_iota(jnp.int32, (rows, 2 * WINDOW), 0)
    j = lax.broadcasted_iota(jnp.int32, (rows, 2 * WINDOW), 1)
    rel = (r & (WINDOW - 1)) + WINDOW - j
    band = (rel >= 0) & (rel < WINDOW)
    band_first = band & ((j >= WINDOW) | (i > 0))
    rcol = lax.broadcasted_iota(jnp.int32, (rows, 1), 0)

    probs = [dict(b=b, n=n) for b in range(nblk) for n in range(A_KV_HEADS)]
    for p in probs:
        b, n = p["b"], p["n"]
        heads = [A_GROUP * n + g for g in range(A_GROUP)]
        qs = jnp.concatenate([qn[b * WINDOW:(b + 1) * WINDOW, HEAD_DIM * h:HEAD_DIM * (h + 1)]
                              for h in heads], axis=0)
        keys = slice(b * WINDOW, (b + 2) * WINDOW)
        p["s"] = _dot_nt(qs, kc[keys, HEAD_DIM * n:HEAD_DIM * (n + 1)])
        p["v"] = vc[keys, HEAD_DIM * n:HEAD_DIM * (n + 1)]
        sink_col = jnp.full((rows, 1), sink_ref[layer, heads[-1]], F32)
        for g in reversed(range(A_GROUP - 1)):
            sink_col = jnp.where(rcol < (g + 1) * WINDOW, sink_ref[layer, heads[g]], sink_col)
        p["sink"] = sink_col
    for p in probs:
        sm = jnp.where(band_first if p["b"] == 0 else band, p["s"], -1e30)
        m = jnp.maximum(jnp.max(sm, axis=-1, keepdims=True), p["sink"])
        e = jnp.exp(sm - m)
        p["den"] = jnp.sum(e, axis=-1, keepdims=True) + jnp.exp(p["sink"] - m)
        p["e"] = e.astype(BF16)
    for p in probs:
        p["o"] = _dot(p["e"], p["v"]) / p["den"]
    o_rows = []
    for b in range(nblk):
        outs = []
        for p in probs[b * A_KV_HEADS:(b + 1) * A_KV_HEADS]:
            outs += [p["o"][g * WINDOW:(g + 1) * WINDOW] for g in range(A_GROUP)]
        o_rows.append(jnp.concatenate(outs, axis=1))
    o_all = jnp.concatenate(o_rows, axis=0) if nblk > 1 else o_rows[0]
    o_ref[...] = (o_all * _silu(ga_ref[...])).astype(BF16)

    last = slice((nblk - 1) * WINDOW, nblk * WINDOW)
    kprev[...] = k[last]
    vprev[...] = v[last]
    kl_ref[...] = k[last]
    vl_ref[...] = v[last]


def _attn_prompt(proj, sinks, qg_all, kg_all, l, carriers, batch, seq):
    tm = ATTN_PROMPT_ROWS
    nb = seq // tm
    n_total = proj.shape[0]
    cb = lambda c, w: c // w
    row = lambda b, i: b * nb + i
    ncar = len(carriers)
    return pl.pallas_call(
        _skip(ncar, functools.partial(_attn_prompt_kernel, layer=l)),
        grid=(batch, nb),
        in_specs=[_CARRIER] * ncar
        + [pl.BlockSpec(memory_space=pltpu.SMEM),
           pl.BlockSpec((tm, A_WIDTH), lambda b, i: (row(b, i), cb(COL_Q, A_WIDTH))),
           pl.BlockSpec((tm, A_WIDTH), lambda b, i: (row(b, i), cb(COL_GA, A_WIDTH))),
           pl.BlockSpec((tm, KV_WIDTH), lambda b, i: (row(b, i), cb(COL_K, KV_WIDTH))),
           pl.BlockSpec((tm, KV_WIDTH), lambda b, i: (row(b, i), cb(COL_V, KV_WIDTH))),
           pl.BlockSpec((None, 1, A_WIDTH), lambda b, i: (l, 0, 0)),
           pl.BlockSpec((None, 1, KV_WIDTH), lambda b, i: (l, 0, 0))],
        out_specs=[pl.BlockSpec((tm, A_WIDTH), lambda b, i: (row(b, i), 0)),
                   pl.BlockSpec((None, None, WINDOW, KV_WIDTH), lambda b, i: (l, b, 0, 0)),
                   pl.BlockSpec((None, None, WINDOW, KV_WIDTH), lambda b, i: (l, b, 0, 0))],
        out_shape=[jax.ShapeDtypeStruct((n_total, A_WIDTH), BF16),
                   jax.ShapeDtypeStruct((DEPTH, batch, WINDOW, KV_WIDTH), F32),
                   jax.ShapeDtypeStruct((DEPTH, batch, WINDOW, KV_WIDTH), F32)],
        scratch_shapes=[pltpu.VMEM((WINDOW, KV_WIDTH), F32),
                        pltpu.VMEM((WINDOW, KV_WIDTH), F32)],
        input_output_aliases={c: 1 + c for c in range(ncar)},
        compiler_params=_params(2),
        name="attn_prompt",
    )(*carriers, sinks, proj, proj, proj, proj, qg_all, kg_all)


def _attn_sample_kernel(sink_ref, q_ref, ga_ref, k_ref, v_ref, ck_ref, cv_ref, qg_ref, kg_ref,
                        o_ref, nk_ref, nv_ref, *, nseq, tdec, layer):
    k = _head_rms(k_ref[...], kg_ref[...])
    v = v_ref[...]
    qn = _head_rms(q_ref[...], qg_ref[...])
    ck = ck_ref[...]
    cv = cv_ref[...]
    k3 = k.reshape(nseq, tdec, KV_WIDTH)
    v3 = v.reshape(nseq, tdec, KV_WIDTH)
    kc = jnp.concatenate([ck, k3], axis=1)
    vc = jnp.concatenate([cv, v3], axis=1)
    nk_ref[...] = kc[:, tdec:, :]
    nv_ref[...] = vc[:, tdec:, :]
    kcb = kc.astype(BF16)
    vcb = vc.astype(BF16)

    rows = A_GROUP * tdec
    keys = WINDOW + tdec
    r = lax.broadcasted_iota(jnp.int32, (1, rows, keys), 1)
    j = lax.broadcasted_iota(jnp.int32, (1, rows, keys), 2)
    rel = WINDOW + (r & (tdec - 1)) - j
    ok = (rel >= 0) & (rel < WINDOW)
    rcol = lax.broadcasted_iota(jnp.int32, (1, rows, 1), 1)

    outs = []
    for n in range(A_KV_HEADS):
        heads = [A_GROUP * n + g for g in range(A_GROUP)]
        qs = jnp.concatenate(
            [qn[:, HEAD_DIM * h:HEAD_DIM * (h + 1)].reshape(nseq, tdec, HEAD_DIM) for h in heads],
            axis=1).astype(BF16)
        kn = kcb[:, :, HEAD_DIM * n:HEAD_DIM * (n + 1)]
        vn = vcb[:, :, HEAD_DIM * n:HEAD_DIM * (n + 1)]
        s = jnp.einsum("bqd,bkd->bqk", qs, kn, preferred_element_type=F32) * (HEAD_DIM ** -0.5)
        sink_col = jnp.full((1, rows, 1), sink_ref[layer, heads[-1]], F32)
        for g in reversed(range(A_GROUP - 1)):
            sink_col = jnp.where(rcol < (g + 1) * tdec, sink_ref[layer, heads[g]], sink_col)
        sm = jnp.where(ok, s, -1e30)
        m = jnp.maximum(jnp.max(sm, axis=-1, keepdims=True), sink_col)
        p = jnp.where(ok, jnp.exp(sm - m), 0.0)
        den = jnp.sum(p, axis=-1, keepdims=True) + jnp.exp(sink_col - m)
        o = jnp.einsum("bqk,bkd->bqd", p.astype(BF16), vn, preferred_element_type=F32) / den
        outs += [o[:, g * tdec:(g + 1) * tdec, :].reshape(nseq * tdec, HEAD_DIM)
                 for g in range(A_GROUP)]
    o_all = jnp.concatenate(outs, axis=1)
    o_ref[...] = (o_all * _silu(ga_ref[...])).astype(BF16)


def _attn_sample(proj, row0, oa, cache_k, cache_v, sinks, qg_all, kg_all, l, carriers,
                 nbatch, tdec, nseq=16):
    tm = nseq * tdec
    rb = row0 // tm
    cb = lambda c, w: c // w
    car = [oa] + list(carriers)
    cache_spec = pl.BlockSpec((None, nseq, WINDOW, KV_WIDTH), lambda s: (l, s, 0, 0))
    return pl.pallas_call(
        _skip(len(car), functools.partial(_attn_sample_kernel, nseq=nseq, tdec=tdec, layer=l)),
        grid=(nbatch // nseq,),
        in_specs=[_CARRIER] * len(car)
        + [pl.BlockSpec(memory_space=pltpu.SMEM),
           pl.BlockSpec((tm, A_WIDTH), lambda s: (rb + s, cb(COL_Q, A_WIDTH))),
           pl.BlockSpec((tm, A_WIDTH), lambda s: (rb + s, cb(COL_GA, A_WIDTH))),
           pl.BlockSpec((tm, KV_WIDTH), lambda s: (rb + s, cb(COL_K, KV_WIDTH))),
           pl.BlockSpec((tm, KV_WIDTH), lambda s: (rb + s, cb(COL_V, KV_WIDTH))),
           cache_spec, cache_spec,
           pl.BlockSpec((None, 1, A_WIDTH), lambda s: (l, 0, 0)),
           pl.BlockSpec((None, 1, KV_WIDTH), lambda s: (l, 0, 0))],
        out_specs=[pl.BlockSpec((tm, A_WIDTH), lambda s: (rb + s, 0)), cache_spec, cache_spec],
        out_shape=[jax.ShapeDtypeStruct(oa.shape, BF16),
                   jax.ShapeDtypeStruct((DEPTH, nbatch, WINDOW, KV_WIDTH), F32),
                   jax.ShapeDtypeStruct((DEPTH, nbatch, WINDOW, KV_WIDTH), F32)],
        input_output_aliases={c: c for c in range(len(car))},
        compiler_params=_params(1),
        name="attn_sample",
    )(*car, sinks, proj, proj, proj, proj, cache_k, cache_v, qg_all, kg_all)


def _stack_heads(x):
    lo = jnp.where(_lane_half_mask(), 1.0, 0.0).astype(x.dtype)
    return jnp.concatenate([x * lo, x * (1 - lo)], axis=0)


def _cat_rows(parts, rows):
    if (rows.stop - rows.start) % 16 == 0:
        return jnp.concatenate([x[rows] for x in parts], axis=0)
    return jnp.concatenate([x.astype(F32)[rows] for x in parts], axis=0).astype(BF16)


def _split3(x):
    hi = x.astype(BF16)
    r1 = x - hi.astype(F32)
    mid = r1.astype(BF16)
    lo = (r1 - mid.astype(F32)).astype(BF16)
    return hi, mid, lo


def _rwkv_masks(blk):
    lb = blk.bit_length() - 1
    t = lax.broadcasted_iota(jnp.int32, (TILE, LANES), 0)
    s = lax.broadcasted_iota(jnp.int32, (TILE, LANES), 1) & (TILE - 1)
    same = (t >> lb) == (s >> lb)
    levels = [((t >> (l + 1)) == (s >> (l + 1))) & (((t >> l) & 1) == 1) & (((s >> l) & 1) == 0)
              for l in range(lb)]
    tt = lax.broadcasted_iota(jnp.int32, (2 * TILE, 3 * TILE), 0)
    ss = lax.broadcasted_iota(jnp.int32, (2 * TILE, 3 * TILE), 1)
    ti = tt & (TILE - 1)
    si = jnp.where(ss >= 2 * TILE, ss - 2 * TILE, jnp.where(ss >= TILE, ss - TILE, ss))
    sel = ((ti >> lb) == (si >> lb)) & ((tt >= TILE) | (si <= ti))
    r2 = lax.broadcasted_iota(jnp.int32, (LANES, LANES), 0)
    c2 = lax.broadcasted_iota(jnp.int32, (LANES, LANES), 1)
    return dict(lb=lb, strict=same & (s < t), incl=same & (s <= t),
                eye=jnp.where(s == t, 1.0, 0.0), levels=levels,
                cumsel=jnp.where(sel, 1.0, 0.0).astype(BF16),
                same_head=(r2 >> 6) == (c2 >> 6))


def _rwkv_tile(xr, xk, xv, xw, gate, prm, states, blk):
    nchunk = xr.shape[0] // TILE
    nseq = TILE // blk
    mk = _rwkv_masks(blk)
    lw = _dot(jnp.tanh(xw).astype(BF16), prm["w2"])
    la = _dot(xw.astype(BF16), prm["a2"])
    ld = (-math.exp(-0.5) * math.log2(math.e)) * _sigmoid(prm["w0"] + lw)
    a = _sigmoid(prm["a0"] + la)
    kkr = xk * prm["k_k"]
    kk = kkr * lax.rsqrt(jnp.maximum(_head_sums_wide(kkr * kkr), 1e-24))
    kmod = xk * (1.0 + (a - 1.0) * prm["k_a"])
    ka = kk * a

    probs = []
    for ch in range(nchunk):
        rows = slice(ch * TILE, (ch + 1) * TILE)
        cum = _dot(mk["cumsel"], jnp.concatenate(_split3(ld[rows]), axis=0))
        c, cl = cum[:TILE], cum[TILE:]
        e_c = jnp.exp2(-c)
        e_l = jnp.exp2(cl - c)
        full = dict(rt=(xr[rows] * jnp.exp2(c)).astype(BF16),
                    at=(-kk[rows] * jnp.exp2(c - ld[rows])).astype(BF16),
                    bt=(ka[rows] * e_c).astype(BF16), kt=(kmod[rows] * e_c).astype(BF16),
                    bh=(ka[rows] * e_l).astype(BF16), kh=(kmod[rows] * e_l).astype(BF16),
                    v=xv[rows].astype(BF16), decay=jnp.exp2(cl))
        for j in range(N_PAIRS):
            sl = slice(j * LANES, (j + 1) * LANES)
            probs.append(dict(ch=ch, j=j, **{n: x[:, sl] for n, x in full.items()}))

    for p in probs:
        p["v_s"] = _stack_heads(p["v"])
        z_s = jnp.concatenate([_stack_heads(p["bt"]), _stack_heads(p["kt"])], axis=0)
        gram = _dot_nt(jnp.concatenate([p["at"], p["rt"]], axis=0), z_s)
        p["a_ab"] = jnp.where(mk["strict"], gram[:TILE, :LANES], 0.0)
        p["a_ak"] = jnp.where(mk["strict"], gram[:TILE, LANES:], 0.0).astype(BF16)
        p["a_r"] = jnp.concatenate([jnp.where(mk["incl"], gram[TILE:, :LANES], 0.0),
                                    jnp.where(mk["incl"], gram[TILE:, LANES:], 0.0)],
                                   axis=1).astype(BF16)
        p["tinv"] = mk["eye"]
        if mk["lb"] > 0:
            p["tinv"] = p["tinv"] + jnp.where(mk["levels"][0], p["a_ab"], 0.0)

    for l in range(1, mk["lb"]):
        for p in probs:
            p["tb"] = p["tinv"].astype(BF16)
            mid = _stack_heads(jnp.where(mk["levels"][l], p["a_ab"], 0.0).astype(BF16))
            p["half"] = _dot(p["tb"], mid).astype(BF16)
        for p in probs:
            p["tinv"] = p["tinv"] + _dot(p["half"], _stack_heads(p["tb"]))

    for p in probs:
        p["akv"] = _dot(p["a_ak"], p["v_s"]).astype(BF16)
    for p in probs:
        p["wu"] = _dot(p["tinv"].astype(BF16),
                       jnp.concatenate([_stack_heads(p["at"]), _stack_heads(p["akv"])], axis=1))
        p["w"], p["u0"] = p["wu"][:, :LANES].astype(BF16), p["wu"][:, LANES:]

    carried = nseq == 1
    if carried:
        for p in probs:
            p["wtb"] = _dot_tn(p["wu"].astype(BF16), p["bh"])
        for p in probs:
            vtk = _dot_tn(p["v"], p["kh"])
            p["m"] = jnp.where(mk["same_head"], p["wtb"][:LANES], 0.0).astype(BF16)
            p["c"] = jnp.where(mk["same_head"], p["wtb"][LANES:] + vtk, 0.0)

    def read_state(p, sts):
        u_parts, rs_parts = [], []
        for q in range(nseq):
            rows = slice(q * blk, (q + 1) * blk)
            res = _dot_nt(_cat_rows([p["w"], p["rt"]], rows), sts[q].astype(BF16))
            u_parts.append(res[:blk] + p["u0"][rows])
            rs_parts.append(res[blk:])
        p["u"] = (jnp.concatenate(u_parts, axis=0) if nseq > 1 else u_parts[0]).astype(BF16)
        p["rs"] = jnp.concatenate(rs_parts, axis=0) if nseq > 1 else rs_parts[0]

    states = [list(st) for st in states]
    y_rows = []
    for ch in range(nchunk):
        cps = [p for p in probs if p["ch"] == ch]
        start = [list(states[p["j"]]) for p in cps]
        if carried:
            for p in cps:
                s = states[p["j"]][0]
                states[p["j"]][0] = s * p["decay"][0:1] + _dot(s.astype(BF16), p["m"]) + p["c"]
        for p, sts in zip(cps, start):
            read_state(p, sts)
        if not carried:
            for p in cps:
                for q in range(nseq):
                    rows = slice(q * blk, (q + 1) * blk)
                    upd = _dot_tn(_cat_rows([p["u"], p["v"]], rows),
                                  _cat_rows([p["bh"], p["kh"]], rows))
                    states[p["j"]][q] = (states[p["j"]][q] * p["decay"][q * blk:q * blk + 1]
                                         + jnp.where(mk["same_head"], upd, 0.0))
        y_rows.append(jnp.concatenate(
            [_dot(p["a_r"], jnp.concatenate([_stack_heads(p["u"]), p["v_s"]], axis=0)) + p["rs"]
             for p in cps], axis=1))
    y = jnp.concatenate(y_rows, axis=0) if nchunk > 1 else y_rows[0]

    mean = _head_sums_wide(y) * (1.0 / HEAD_DIM)
    yc = y - mean
    var = _head_sums_wide(yc * yc) * (1.0 / HEAD_DIM)
    yn = yc * lax.rsqrt(var + GN_EPS) * prm["gn_w"] + prm["gn_b"]
    yn = yn + _head_sums_wide(xr * kmod * prm["r_k"]) * xv
    return (yn * _silu(gate)).astype(BF16), states


_RWKV_PARAM_NAMES = ("mu", "w0", "w2", "a0", "a2", "k_k", "k_a", "r_k", "gn_w", "gn_b")


def _rwkv_shift_mix(cur, prev_rows, first_row_mask, mu):
    shifted = jnp.where(first_row_mask, prev_rows, pltpu.roll(cur, 1, axis=0))
    return cur + (shifted - cur) * mu


def _rwkv_load_params(refs):
    prm = {n: r[...] for n, r in zip(_RWKV_PARAM_NAMES, refs)}
    mu = prm.pop("mu")
    return prm, (mu[:, 0:B_WIDTH], mu[:, B_WIDTH:2 * B_WIDTH],
                 mu[:, 2 * B_WIDTH:3 * B_WIDTH], mu[:, 3 * B_WIDTH:])


def _rwkv_prompt_kernel(r_ref, k_ref, v_ref, g_ref, w_ref, *rest):
    prm_refs, (o_ref, s_out_ref, state, prev) = rest[:len(_RWKV_PARAM_NAMES)], rest[len(_RWKV_PARAM_NAMES):]
    i = pl.program_id(1)

    @pl.when(i == 0)
    def _():
        state[...] = jnp.zeros_like(state)
        prev[...] = jnp.zeros_like(prev)

    prm, mus = _rwkv_load_params(prm_refs)
    cur = (r_ref[...], k_ref[...], v_ref[...], w_ref[...])
    rows = cur[0].shape[0]
    first = lax.broadcasted_iota(jnp.int32, (8, 1), 0) == 0
    offs = (0, B_WIDTH, 2 * B_WIDTH, 3 * B_WIDTH, SHIFT_PAD)
    mixed = []
    for n, x in enumerate(cur):
        m = x + (pltpu.roll(x, 1, axis=0) - x) * mus[n]
        top = x[0:8]
        top = jnp.where(first, top + (prev[0:1, offs[n]:offs[n + 1]] - top) * mus[n], m[0:8])
        mixed.append(jnp.concatenate([top, m[8:]], axis=0))
        prev[0:1, offs[n]:offs[n + 1]] = x[rows - 1:rows, :]
    states = [[state[j]] for j in range(N_PAIRS)]
    out, new_states = _rwkv_tile(*mixed, g_ref[...], prm, states, TILE)
    o_ref[...] = out
    for j in range(N_PAIRS):
        s = new_states[j][0]
        state[j] = s
        s_out_ref[2 * j] = s[:HEAD_DIM, :HEAD_DIM]
        s_out_ref[2 * j + 1] = s[HEAD_DIM:, HEAD_DIM:]


def _rwkv_sample_kernel(r_ref, k_ref, v_ref, g_ref, w_ref, sh_ref, s_in_ref, *rest, tdec):
    prm_refs, (o_ref, s_out_ref) = rest[:len(_RWKV_PARAM_NAMES)], rest[len(_RWKV_PARAM_NAMES):]
    nseq = TILE // tdec
    prm, mus = _rwkv_load_params(prm_refs)
    cur = (r_ref[...], k_ref[...], v_ref[...], w_ref[...])
    first = (lax.broadcasted_iota(jnp.int32, (TILE, 1), 0) & (tdec - 1)) == 0
    offs = (0, B_WIDTH, 2 * B_WIDTH, 3 * B_WIDTH, SHIFT_PAD)
    sh = sh_ref[...]
    mixed = []
    for n, x in enumerate(cur):
        prev_rows = jnp.concatenate(
            [jnp.broadcast_to(sh[q:q + 1, offs[n]:offs[n + 1]], (tdec, x.shape[1]))
             for q in range(nseq)], axis=0)
        mixed.append(_rwkv_shift_mix(x, prev_rows, first, mus[n]))
    zero = jnp.zeros((HEAD_DIM, HEAD_DIM), F32)
    states = []
    for j in range(N_PAIRS):
        pair = []
        for q in range(nseq):
            top = jnp.concatenate([s_in_ref[q, 2 * j], zero], axis=1)
            bot = jnp.concatenate([zero, s_in_ref[q, 2 * j + 1]], axis=1)
            pair.append(jnp.concatenate([top, bot], axis=0))
        states.append(pair)
    out, new_states = _rwkv_tile(*mixed, g_ref[...], prm, states, tdec)
    o_ref[...] = out
    for j in range(N_PAIRS):
        for q in range(nseq):
            s = new_states[j][q]
            s_out_ref[q, 2 * j] = s[:HEAD_DIM, :HEAD_DIM]
            s_out_ref[q, 2 * j + 1] = s[HEAD_DIM:, HEAD_DIM:]


def _rwkv_param_specs(l):
    shapes = {"mu": (None, 1, SHIFT_PAD), "w2": (None, LANES, B_WIDTH), "a2": (None, LANES, B_WIDTH)}
    return [pl.BlockSpec(shapes.get(n, (None, 1, B_WIDTH)), lambda *_: (l, 0, 0))
            for n in _RWKV_PARAM_NAMES]


def _rwkv_col_specs(row_fn, rows):
    cols = (COL_R, COL_KB, COL_VB, COL_GB)
    specs = [pl.BlockSpec((rows, B_WIDTH), (lambda *a, c=c: (row_fn(*a), c // B_WIDTH))) for c in cols]
    specs.append(pl.BlockSpec((rows, LANES), lambda *a: (row_fn(*a), COL_WLAL // LANES)))
    return specs


def _rwkv_prompt(proj, prm, l, carriers, batch, seq, rows=RWKV_PROMPT_ROWS):
    nt = seq // rows
    row_fn = lambda b, i: b * nt + i
    ncar = len(carriers)
    return pl.pallas_call(
        _skip(ncar, _rwkv_prompt_kernel),
        grid=(batch, nt),
        in_specs=[_CARRIER] * ncar + _rwkv_col_specs(row_fn, rows) + _rwkv_param_specs(l),
        out_specs=[pl.BlockSpec((rows, B_WIDTH), lambda b, i: (row_fn(b, i), 0)),
                   pl.BlockSpec((None, None, B_HEADS, HEAD_DIM, HEAD_DIM),
                                lambda b, i: (l, b, 0, 0, 0))],
        out_shape=[jax.ShapeDtypeStruct((proj.shape[0], B_WIDTH), BF16),
                   jax.ShapeDtypeStruct((DEPTH, batch, B_HEADS, HEAD_DIM, HEAD_DIM), F32)],
        scratch_shapes=[pltpu.VMEM((N_PAIRS, LANES, LANES), F32),
                        pltpu.VMEM((8, SHIFT_PAD), F32)],
        input_output_aliases={c: 1 + c for c in range(ncar)},
        compiler_params=_params(2),
        name="rwkv_prompt",
    )(*carriers, proj, proj, proj, proj, proj, *[prm[n] for n in _RWKV_PARAM_NAMES])


def _rwkv_sample(proj, row0, ob, shift_in, state_in, prm, l, carriers, nbatch, tdec):
    nseq = TILE // tdec
    rb = row0 // TILE
    row_fn = lambda s: rb + s
    car = [ob] + list(carriers)
    state_spec = pl.BlockSpec((None, nseq, B_HEADS, HEAD_DIM, HEAD_DIM), lambda s: (l, s, 0, 0, 0))
    return pl.pallas_call(
        _skip(len(car), functools.partial(_rwkv_sample_kernel, tdec=tdec)),
        grid=(nbatch // nseq,),
        in_specs=[_CARRIER] * len(car) + _rwkv_col_specs(row_fn, TILE)
        + [pl.BlockSpec((None, nseq, SHIFT_PAD), lambda s: (l, s, 0)), state_spec]
        + _rwkv_param_specs(l),
        out_specs=[pl.BlockSpec((TILE, B_WIDTH), lambda s: (rb + s, 0)), state_spec],
        out_shape=[jax.ShapeDtypeStruct(ob.shape, BF16),
                   jax.ShapeDtypeStruct((DEPTH, nbatch, B_HEADS, HEAD_DIM, HEAD_DIM), F32)],
        input_output_aliases={c: c for c in range(len(car))},
        compiler_params=_params(1),
        name="rwkv_sample",
    )(*car, proj, proj, proj, proj, proj, shift_in, state_in,
      *[prm[n] for n in _RWKV_PARAM_NAMES])


def _window_sums(xe):
    s2 = xe + pltpu.roll(xe, 1, axis=0)
    s4 = s2 + pltpu.roll(s2, 2, axis=0)
    s8 = s4 + pltpu.roll(s4, 4, axis=0)
    s16 = s8 + pltpu.roll(s8, 8, axis=0)
    lane = lax.broadcasted_iota(jnp.int32, (1, C_WIDTH), 1)
    return jnp.where(lane < 64, s2, jnp.where(lane < 128, s4, jnp.where(lane < 192, s8, s16)))


def _pool_window_lane():
    lane = lax.broadcasted_iota(jnp.int32, (1, C_WIDTH), 1)
    return jnp.where(lane < 64, 2, jnp.where(lane < 128, 4, jnp.where(lane < 192, 8, 16)))


def _pool_prompt_rows(u, halo, gate, wbd, scale, it):
    tm = u.shape[0]
    halo = jnp.where(it > 0, halo, 0.0)
    sums = _window_sums(jnp.concatenate([halo, u], axis=0))[16:]
    pos = it * tm + lax.broadcasted_iota(jnp.int32, (tm, 1), 0)
    cnt = jnp.minimum(_pool_window_lane(), pos + 1).astype(F32)
    d = sums / cnt - u
    y = _dot(d.astype(BF16), wbd) * scale
    return (y * _silu(gate)).astype(BF16)


def _pool_sample_kernel(u_ref, h_ref, g_ref, w_ref, sc_ref, o_ref, *, nseq, tdec, pos0):
    u = u_ref[...]
    hist = h_ref[...]
    xe = jnp.concatenate([hist, u.reshape(nseq, tdec, C_WIDTH)], axis=1)
    xe = xe.reshape(nseq * (16 + tdec), C_WIDTH)
    sums = _window_sums(xe).reshape(nseq, 16 + tdec, C_WIDTH)[:, 16:, :]
    sums = sums.reshape(nseq * tdec, C_WIDTH)
    t = lax.broadcasted_iota(jnp.int32, (nseq * tdec, 1), 0) & (tdec - 1)
    cnt = jnp.minimum(_pool_window_lane(), pos0 + t + 1).astype(F32)
    d = sums / cnt - u
    y = _dot(d.astype(BF16), w_ref[...]) * sc_ref[...]
    o_ref[...] = (y * _silu(g_ref[...])).astype(BF16)


def _pool_sample(proj, row0, hist16_all, wbd_all, scale_all, l, nbatch, tdec, pos0, nseq=64):
    tm = nseq * tdec
    rb = row0 // tm
    return pl.pallas_call(
        functools.partial(_pool_sample_kernel, nseq=nseq, tdec=tdec, pos0=pos0),
        grid=(nbatch // nseq,),
        in_specs=[pl.BlockSpec((tm, C_WIDTH), lambda s: (rb + s, COL_UC // C_WIDTH)),
                  pl.BlockSpec((None, nseq, 16, C_WIDTH), lambda s: (l, s, 0, 0)),
                  pl.BlockSpec((tm, C_WIDTH), lambda s: (rb + s, COL_GC // C_WIDTH)),
                  pl.BlockSpec((None, C_WIDTH, C_WIDTH), lambda s: (l, 0, 0)),
                  pl.BlockSpec((None, 1, C_WIDTH), lambda s: (l, 0, 0))],
        out_specs=pl.BlockSpec((tm, C_WIDTH), lambda s: (s, 0)),
        out_shape=jax.ShapeDtypeStruct((nbatch * tdec, C_WIDTH), BF16),
        compiler_params=_params(1),
        name="pool_sample",
    )(proj, hist16_all, proj, wbd_all, scale_all)


def _pad_shift(x):
    pad = jnp.zeros(x.shape[:-1] + (LANES - 2 * LORA,), x.dtype)
    return jnp.concatenate([x, pad], axis=-1)


def _block_diag(w):
    n, g, c, _ = w.shape
    eye = jnp.eye(g, dtype=w.dtype)
    return (eye[None, :, None, :, None] * w[:, :, :, None, :]).reshape(n, g * c, g * c)


def _shift_row(rows):
    return jnp.concatenate([rows[..., COL_R:COL_R + 3 * B_WIDTH],
                            rows[..., COL_WLAL:COL_WLAL + 2 * LORA]], axis=-1)


def kernel(x_prompt, x_sample, cache_k, cache_v, state_wkv, state_shift, state_pool, norm_g, w_in, q_norm_g, k_norm_g, attn_sinks, shift_mu, decay_w0, decay_w2, iclr_a0, iclr_a2, k_k, k_a, r_k, gn_w, gn_b, pool_w, pool_scale, w_out):
    batch, seq, _ = x_prompt.shape
    nbatch, tdec, _ = x_sample.shape
    wbuf = cache_k.shape[2]
    n_prompt, n_sample = batch * seq, nbatch * tdec
    n_total = n_prompt + n_sample

    row = lambda a: a.reshape(DEPTH, 1, -1)
    lora_pad = jnp.zeros((DEPTH, LANES - LORA, B_WIDTH), F32)
    prm = {
        "mu": row(_pad_shift(shift_mu)), "w0": row(decay_w0), "a0": row(iclr_a0),
        "w2": jnp.concatenate([decay_w2, lora_pad], axis=1).astype(BF16),
        "a2": jnp.concatenate([lora_pad[:, :LORA], iclr_a2, lora_pad[:, LORA:]], axis=1).astype(BF16),
        "k_k": row(k_k), "k_a": row(k_a), "r_k": row(r_k), "gn_w": row(gn_w), "gn_b": row(gn_b),
    }
    g_all = row(norm_g)
    qg_all = row(jnp.tile(q_norm_g, (1, A_HEADS)))
    kg_all = row(jnp.tile(k_norm_g, (1, A_KV_HEADS)))
    w_in_all = jnp.swapaxes(w_in, 1, 2)
    w_out_all = w_out.astype(BF16)
    wbd_all = _block_diag(pool_w).astype(BF16)
    scale_all = row(pool_scale)
    ck_all = cache_k.reshape(DEPTH, nbatch, wbuf, KV_WIDTH)
    cv_all = cache_v.reshape(DEPTH, nbatch, wbuf, KV_WIDTH)
    shift_all = _pad_shift(state_shift)
    hist16_all = jnp.pad(state_pool, ((0, 0), (0, 0), (1, 0), (0, 0)))

    x_p = x_prompt.reshape(n_prompt, D_MODEL)
    x_s = x_sample.reshape(n_sample, D_MODEL)
    kv_p, kv_s, wkv_p, wkv_s = [], [], [], []
    shift_p, shift_s, pool_p, pool_s = [], [], [], []
    proj = _inproj(x_p, g_all, w_in_all, 0, n_total, 0)
    for l in range(DEPTH):
        proj = _inproj(x_s, g_all, w_in_all, l, n_total, n_prompt, carrier=proj)

        oa, *kv_p = _attn_prompt(proj, attn_sinks, qg_all, kg_all, l, kv_p, batch, seq)
        oa, *kv_s = _attn_sample(proj, n_prompt, oa, ck_all, cv_all, attn_sinks, qg_all, kg_all,
                                 l, kv_s, nbatch, tdec)
        ob, *wkv_p = _rwkv_prompt(proj, prm, l, wkv_p, batch, seq)
        ob, *wkv_s = _rwkv_sample(proj, n_prompt, ob, shift_all, state_wkv, prm, l, wkv_s,
                                  nbatch, tdec)
        oc_s = _pool_sample(proj, n_prompt, hist16_all, wbd_all, scale_all, l, nbatch, tdec,
                            PAST_LEN)
        if l + 1 < DEPTH:
            x_p, proj_next = _outproj_prompt(x_p, oa, ob, proj, wbd_all, scale_all, w_out_all, l,
                                             seq, next_in=(g_all, w_in_all, n_total))
        else:
            x_p = _outproj_prompt(x_p, oa, ob, proj, wbd_all, scale_all, w_out_all, l, seq)
        x_s = _outproj_sample(x_s, oa, ob, oc_s, w_out_all, l, n_prompt)

        shift_p.append(jnp.stack([_shift_row(proj[(b + 1) * seq - 1]) for b in range(batch)]))
        shift_s.append(_shift_row(proj[n_prompt + tdec - 1::tdec]))
        pool_p.append(jnp.stack([proj[(b + 1) * seq - POOL_HIST:(b + 1) * seq,
                                      COL_UC:COL_UC + C_WIDTH] for b in range(batch)]))
        u_s = proj[n_prompt:, COL_UC:COL_UC + C_WIDTH].reshape(nbatch, tdec, C_WIDTH)
        pool_s.append(jnp.concatenate([state_pool[l], u_s], axis=1)[:, -POOL_HIST:])
        if l + 1 < DEPTH:
            proj = proj_next

    heads = lambda a: a.reshape(a.shape[:-1] + (A_KV_HEADS, HEAD_DIM))
    return (x_p.reshape(batch, seq, D_MODEL), x_s.reshape(nbatch, tdec, D_MODEL),
            heads(kv_p[0]), heads(kv_p[1]), wkv_p[0], jnp.stack(shift_p), jnp.stack(pool_p),
            heads(kv_s[0]), heads(kv_s[1]), wkv_s[0], jnp.stack(shift_s), jnp.stack(pool_s))
```

```python
import functools
import math

import jax
import jax.numpy as jnp
from jax import lax
from jax.experimental import pallas as pl
from jax.experimental.pallas import tpu as pltpu

F32 = jnp.float32
BF16 = jnp.bfloat16

D_MODEL = 1024
DEPTH = 4
HEAD_DIM = 64
A_HEADS = 6
A_KV_HEADS = 2
A_GROUP = A_HEADS // A_KV_HEADS
A_WIDTH = A_HEADS * HEAD_DIM
KV_WIDTH = A_KV_HEADS * HEAD_DIM
WINDOW = 128
B_HEADS = 6
B_WIDTH = B_HEADS * HEAD_DIM
LORA = 32
GN_EPS = 6.4e-4
SHIFT_WIDTH = 3 * B_WIDTH + 2 * LORA
C_WIDTH = 256
POOL_HIST = 15
NORM_EPS = 1e-6
PAST_LEN = 8192

LANES = 128
TILE = 64
RWKV_PROMPT_ROWS = 512
ATTN_PROMPT_ROWS = 512
RWKV_SAMPLE_ROWS = 256
N_PAIRS = B_HEADS // 2

COL_Q, COL_GA, COL_R, COL_KB, COL_VB, COL_GB = 0, 384, 768, 1152, 1536, 1920
COL_K, COL_V, COL_UC, COL_GC, COL_WLAL = 2304, 2432, 2560, 2816, 3072
IN_PAD = 3200
SHIFT_PAD = 3 * B_WIDTH + LANES

VMEM_LIMIT = 48 * 1024 * 1024
VMEM_LIMIT_FUSED = 58 * 1024 * 1024


def _dot(a, b, prec=None):
    return jnp.dot(a, b, preferred_element_type=F32, precision=prec)


def _dot_nt(a, b, prec=None):
    return lax.dot_general(a, b, (((1,), (1,)), ((), ())),
                           preferred_element_type=F32, precision=prec)


def _dot_tn(a, b, prec=None):
    return lax.dot_general(a, b, (((0,), (0,)), ((), ())),
                           preferred_element_type=F32, precision=prec)


def _sigmoid(x):
    return 1.0 / (1.0 + jnp.exp(-x))


def _silu(x):
    return x * _sigmoid(x)


def _lane_half_mask(rows=1):
    lane = lax.broadcasted_iota(jnp.int32, (rows, LANES), 1)
    return lane < HEAD_DIM


def _head_sums(x):
    lo = _lane_half_mask()
    s0 = jnp.sum(jnp.where(lo, x, 0.0), axis=-1, keepdims=True)
    s1 = jnp.sum(jnp.where(lo, 0.0, x), axis=-1, keepdims=True)
    return jnp.where(lo, s0, s1)


def _head_sums_wide(x):
    n = x.shape[1] // LANES
    return jnp.concatenate(
        [_head_sums(x[:, j * LANES:(j + 1) * LANES]) for j in range(n)], axis=1)


def _head_rms(x, g):
    ms = _head_sums_wide(x * x) * (1.0 / HEAD_DIM)
    return x * lax.rsqrt(ms + NORM_EPS) * g


_W_SEGMENTS = ((0, 384, COL_Q), (384, 128, COL_K), (512, 128, COL_V), (640, 384, COL_GA),
               (1024, 3 * B_WIDTH, COL_R), (2176, 2 * LORA, COL_WLAL), (2240, 384, COL_GB),
               (2624, 256, COL_UC), (2880, 256, COL_GC))
IN_WIDTH = 3136


def _regroup_w_in(wt_ref, wb):
    @pl.when(pl.program_id(0) == 0)
    def _():
        for src, n, dst in _W_SEGMENTS:
            wb[dst:dst + n, :] = wt_ref[src:src + n, :].astype(BF16)
        pad0 = COL_WLAL + 2 * LORA
        wb[pad0:IN_PAD, :] = jnp.zeros((IN_PAD - pad0, D_MODEL), BF16)


def _norm_project(x, g, wb):
    ms = jnp.mean(x * x, axis=-1, keepdims=True)
    h = x * lax.rsqrt(ms + NORM_EPS) * g
    return _dot_nt(h.astype(BF16), wb[...])


def _inproj_kernel(x_ref, g_ref, wt_ref, o_ref, wb):
    _regroup_w_in(wt_ref, wb)
    o_ref[...] = _norm_project(x_ref[...], g_ref[...], wb)


def _skip(n, fn):
    def wrapped(*refs, **kw):
        return fn(*refs[n:], **kw)
    return wrapped


_CARRIER = pl.BlockSpec(memory_space=pl.ANY)


def _params(ndims):
    return pltpu.CompilerParams(dimension_semantics=("arbitrary",) * ndims,
                                vmem_limit_bytes=VMEM_LIMIT)


def _inproj(x, g_all, wt_all, l, n_total, row0, carrier=None, tm=512):
    n = x.shape[0]
    rb = row0 // tm
    car = [] if carrier is None else [carrier]
    return pl.pallas_call(
        _skip(len(car), _inproj_kernel),
        grid=(n // tm,),
        in_specs=[_CARRIER] * len(car)
        + [pl.BlockSpec((tm, D_MODEL), lambda i: (i, 0)),
           pl.BlockSpec((None, 1, D_MODEL), lambda i: (l, 0, 0)),
           pl.BlockSpec((None, IN_WIDTH, D_MODEL), lambda i: (l, 0, 0),
                        pipeline_mode=pl.Buffered(1))],
        out_specs=pl.BlockSpec((tm, IN_PAD), lambda i: (rb + i, 0)),
        out_shape=jax.ShapeDtypeStruct((n_total, IN_PAD), F32),
        scratch_shapes=[pltpu.VMEM((IN_PAD, D_MODEL), BF16)],
        input_output_aliases={0: 0} if car else {},
        compiler_params=_params(1),
        name="inproj",
    )(*car, x, g_all, wt_all)


def _outproj_kernel(x_ref, oa_ref, ob_ref, oc_ref, wa_ref, wb_ref, wc_ref, o_ref):
    acc = _dot(oa_ref[...], wa_ref[...])
    acc += _dot(ob_ref[...], wb_ref[...])
    acc += _dot(oc_ref[...], wc_ref[...])
    o_ref[...] = x_ref[...] + acc


def _outproj_pool_rows(x_ref, oa_ref, ob_ref, u_ref, halo_ref, g_ref, wp_ref, sc_ref,
                       wa_ref, wb_ref, wc_ref, tiles_per_seq):
    it = lax.rem(pl.program_id(0), tiles_per_seq)
    oc = _pool_prompt_rows(u_ref[...], halo_ref[...], g_ref[...], wp_ref[...], sc_ref[...], it)
    acc = _dot(oa_ref[...], wa_ref[...])
    acc += _dot(ob_ref[...], wb_ref[...])
    acc += _dot(oc, wc_ref[...])
    return x_ref[...] + acc


def _outproj_pool_kernel(*refs, tiles_per_seq):
    *ins, o_ref = refs
    o_ref[...] = _outproj_pool_rows(*ins, tiles_per_seq)


def _outproj_inproj_kernel(*refs, tiles_per_seq):
    *ins, ng_ref, wt_ref, xo_ref, po_ref, wb = refs
    _regroup_w_in(wt_ref, wb)
    x_new = _outproj_pool_rows(*ins, tiles_per_seq)
    xo_ref[...] = x_new
    po_ref[...] = _norm_project(x_new, ng_ref[...], wb)


def _w_out_specs(l):
    return [pl.BlockSpec((None, A_WIDTH, D_MODEL), lambda i: (l, 0, 0)),
            pl.BlockSpec((None, B_WIDTH, D_MODEL), lambda i: (l, 1, 0)),
            pl.BlockSpec((None, C_WIDTH, D_MODEL), lambda i: (l, 3, 0))]


def _outproj_prompt(x, oa, ob, proj, wp_all, sc_all, w_all, l, seq, next_in=None, tm=512):
    n = x.shape[0]
    halo_row = lambda i: jnp.maximum(i * (tm // 16) - 1, 0)
    in_specs = [pl.BlockSpec((tm, D_MODEL), lambda i: (i, 0)),
                pl.BlockSpec((tm, A_WIDTH), lambda i: (i, 0)),
                pl.BlockSpec((tm, B_WIDTH), lambda i: (i, 0)),
                pl.BlockSpec((tm, C_WIDTH), lambda i: (i, COL_UC // C_WIDTH)),
                pl.BlockSpec((16, C_WIDTH), lambda i: (halo_row(i), COL_UC // C_WIDTH)),
                pl.BlockSpec((tm, C_WIDTH), lambda i: (i, COL_GC // C_WIDTH)),
                pl.BlockSpec((None, C_WIDTH, C_WIDTH), lambda i: (l, 0, 0)),
                pl.BlockSpec((None, 1, C_WIDTH), lambda i: (l, 0, 0))] + _w_out_specs(l)
    args = [x, oa, ob, proj, proj, proj, wp_all, sc_all, w_all, w_all, w_all]
    x_spec = pl.BlockSpec((tm, D_MODEL), lambda i: (i, 0))
    x_shape = jax.ShapeDtypeStruct((n, D_MODEL), F32)
    if next_in is None:
        return pl.pallas_call(
            functools.partial(_outproj_pool_kernel, tiles_per_seq=seq // tm),
            grid=(n // tm,), in_specs=in_specs, out_specs=x_spec, out_shape=x_shape,
            compiler_params=_params(1), name="outproj_prompt")(*args)
    g_all, wt_all, n_total = next_in
    return pl.pallas_call(
        functools.partial(_outproj_inproj_kernel, tiles_per_seq=seq // tm),
        grid=(n // tm,),
        in_specs=in_specs
        + [pl.BlockSpec((None, 1, D_MODEL), lambda i: (l + 1, 0, 0)),
           pl.BlockSpec((None, IN_WIDTH, D_MODEL), lambda i: (l + 1, 0, 0),
                        pipeline_mode=pl.Buffered(1))],
        out_specs=[x_spec, pl.BlockSpec((tm, IN_PAD), lambda i: (i, 0))],
        out_shape=[x_shape, jax.ShapeDtypeStruct((n_total, IN_PAD), F32)],
        scratch_shapes=[pltpu.VMEM((IN_PAD, D_MODEL), BF16)],
        compiler_params=pltpu.CompilerParams(dimension_semantics=("arbitrary",),
                                             vmem_limit_bytes=VMEM_LIMIT_FUSED),
        name="outproj_inproj",
    )(*args, g_all, wt_all)


def _outproj_sample(x, oa, ob, oc, w_all, l, row0, tm=512):
    n = x.shape[0]
    rb = row0 // tm
    return pl.pallas_call(
        _outproj_kernel,
        grid=(n // tm,),
        in_specs=[pl.BlockSpec((tm, D_MODEL), lambda i: (i, 0)),
                  pl.BlockSpec((tm, A_WIDTH), lambda i: (rb + i, 0)),
                  pl.BlockSpec((tm, B_WIDTH), lambda i: (rb + i, 0)),
                  pl.BlockSpec((tm, C_WIDTH), lambda i: (i, 0))] + _w_out_specs(l),
        out_specs=pl.BlockSpec((tm, D_MODEL), lambda i: (i, 0)),
        out_shape=jax.ShapeDtypeStruct((n, D_MODEL), F32),
        compiler_params=_params(1),
        name="outproj_sample",
    )(x, oa, ob, oc, w_all, w_all, w_all)


def _attn_prompt_kernel(sink_ref, q_ref, ga_ref, k_ref, v_ref, qg_ref, kg_ref,
                        o_ref, kl_ref, vl_ref, kprev, vprev, *, layer):
    i = pl.program_id(1)

    @pl.when(i == 0)
    def _():
        kprev[...] = jnp.zeros_like(kprev)
        vprev[...] = jnp.zeros_like(vprev)

    nblk = q_ref.shape[0] // WINDOW
    k = _head_rms(k_ref[...], kg_ref[...])
    v = v_ref[...]
    qn = (_head_rms(q_ref[...], qg_ref[...]) * (HEAD_DIM ** -0.5)).astype(BF16)
    kc = jnp.concatenate([kprev[...], k], axis=0).astype(BF16)
    vc = jnp.concatenate([vprev[...], v], axis=0).astype(BF16)

    rows = A_GROUP * WINDOW
    r = lax.broadcasted_iota(jnp.int32, (rows, 2 * WINDOW), 0)
    j = lax.broadcasted_iota(jnp.int32, (rows, 2 * WINDOW), 1)
    rel = (r & (WINDOW - 1)) + WINDOW - j
    band = (rel >= 0) & (rel < WINDOW)
    band_first = band & ((j >= WINDOW) | (i > 0))
    rcol = lax.broadcasted_iota(jnp.int32, (rows, 1), 0)

    probs = [dict(b=b, n=n) for b in range(nblk) for n in range(A_KV_HEADS)]
    for p in probs:
        b, n = p["b"], p["n"]
        heads = [A_GROUP * n + g for g in range(A_GROUP)]
        qs = jnp.concatenate([qn[b * WINDOW:(b + 1) * WINDOW, HEAD_DIM * h:HEAD_DIM * (h + 1)]
                              for h in heads], axis=0)
        keys = slice(b * WINDOW, (b + 2) * WINDOW)
        p["s"] = _dot_nt(qs, kc[keys, HEAD_DIM * n:HEAD_DIM * (n + 1)])
        p["v"] = vc[keys, HEAD_DIM * n:HEAD_DIM * (n + 1)]
        sink_col = jnp.full((rows, 1), sink_ref[layer, heads[-1]], F32)
        for g in reversed(range(A_GROUP - 1)):
            sink_col = jnp.where(rcol < (g + 1) * WINDOW, sink_ref[layer, heads[g]], sink_col)
        p["sink"] = sink_col
    for p in probs:
        sm = jnp.where(band_first if p["b"] == 0 else band, p["s"], -1e30)
        m = jnp.maximum(jnp.max(sm, axis=-1, keepdims=True), p["sink"])
        e = jnp.exp(sm - m)
        p["den"] = jnp.sum(e, axis=-1, keepdims=True) + jnp.exp(p["sink"] - m)
        p["e"] = e.astype(BF16)
    for p in probs:
        p["o"] = _dot(p["e"], p["v"]) / p["den"]
    o_rows = []
    for b in range(nblk):
        outs = []
        for p in probs[b * A_KV_HEADS:(b + 1) * A_KV_HEADS]:
            outs += [p["o"][g * WINDOW:(g + 1) * WINDOW] for g in range(A_GROUP)]
        o_rows.append(jnp.concatenate(outs, axis=1))
    o_all = jnp.concatenate(o_rows, axis=0) if nblk > 1 else o_rows[0]
    o_ref[...] = (o_all * _silu(ga_ref[...])).astype(BF16)

    last = slice((nblk - 1) * WINDOW, nblk * WINDOW)
    kprev[...] = k[last]
    vprev[...] = v[last]
    kl_ref[...] = k[last]
    vl_ref[...] = v[last]


def _attn_prompt(proj, sinks, qg_all, kg_all, l, carriers, batch, seq):
    tm = ATTN_PROMPT_ROWS
    nb = seq // tm
    n_total = proj.shape[0]
    cb = lambda c, w: c // w
    row = lambda b, i: b * nb + i
    ncar = len(carriers)
    return pl.pallas_call(
        _skip(ncar, functools.partial(_attn_prompt_kernel, layer=l)),
        grid=(batch, nb),
        in_specs=[_CARRIER] * ncar
        + [pl.BlockSpec(memory_space=pltpu.SMEM),
           pl.BlockSpec((tm, A_WIDTH), lambda b, i: (row(b, i), cb(COL_Q, A_WIDTH))),
           pl.BlockSpec((tm, A_WIDTH), lambda b, i: (row(b, i), cb(COL_GA, A_WIDTH))),
           pl.BlockSpec((tm, KV_WIDTH), lambda b, i: (row(b, i), cb(COL_K, KV_WIDTH))),
           pl.BlockSpec((tm, KV_WIDTH), lambda b, i: (row(b, i), cb(COL_V, KV_WIDTH))),
           pl.BlockSpec((None, 1, A_WIDTH), lambda b, i: (l, 0, 0)),
           pl.BlockSpec((None, 1, KV_WIDTH), lambda b, i: (l, 0, 0))],
        out_specs=[pl.BlockSpec((tm, A_WIDTH), lambda b, i: (row(b, i), 0)),
                   pl.BlockSpec((None, None, WINDOW, KV_WIDTH), lambda b, i: (l, b, 0, 0)),
                   pl.BlockSpec((None, None, WINDOW, KV_WIDTH), lambda b, i: (l, b, 0, 0))],
        out_shape=[jax.ShapeDtypeStruct((n_total, A_WIDTH), BF16),
                   jax.ShapeDtypeStruct((DEPTH, batch, WINDOW, KV_WIDTH), F32),
                   jax.ShapeDtypeStruct((DEPTH, batch, WINDOW, KV_WIDTH), F32)],
        scratch_shapes=[pltpu.VMEM((WINDOW, KV_WIDTH), F32),
                        pltpu.VMEM((WINDOW, KV_WIDTH), F32)],
        input_output_aliases={c: 1 + c for c in range(ncar)},
        compiler_params=_params(2),
        name="attn_prompt",
    )(*carriers, sinks, proj, proj, proj, proj, qg_all, kg_all)


def _attn_sample_kernel(sink_ref, q_ref, ga_ref, k_ref, v_ref, ck_ref, cv_ref, qg_ref, kg_ref,
                        o_ref, nk_ref, nv_ref, *, nseq, tdec, layer):
    k = _head_rms(k_ref[...], kg_ref[...])
    v = v_ref[...]
    qn = _head_rms(q_ref[...], qg_ref[...])
    ck = ck_ref[...]
    cv = cv_ref[...]
    k3 = k.reshape(nseq, tdec, KV_WIDTH)
    v3 = v.reshape(nseq, tdec, KV_WIDTH)
    kc = jnp.concatenate([ck, k3], axis=1)
    vc = jnp.concatenate([cv, v3], axis=1)
    nk_ref[...] = kc[:, tdec:, :]
    nv_ref[...] = vc[:, tdec:, :]
    kcb = kc.astype(BF16)
    vcb = vc.astype(BF16)

    rows = A_GROUP * tdec
    keys = WINDOW + tdec
    r = lax.broadcasted_iota(jnp.int32, (1, rows, keys), 1)
    j = lax.broadcasted_iota(jnp.int32, (1, rows, keys), 2)
    rel = WINDOW + (r & (tdec - 1)) - j
    ok = (rel >= 0) & (rel < WINDOW)
    rcol = lax.broadcasted_iota(jnp.int32, (1, rows, 1), 1)

    outs = []
    for n in range(A_KV_HEADS):
        heads = [A_GROUP * n + g for g in range(A_GROUP)]
        qs = jnp.concatenate(
            [qn[:, HEAD_DIM * h:HEAD_DIM * (h + 1)].reshape(nseq, tdec, HEAD_DIM) for h in heads],
            axis=1).astype(BF16)
        kn = kcb[:, :, HEAD_DIM * n:HEAD_DIM * (n + 1)]
        vn = vcb[:, :, HEAD_DIM * n:HEAD_DIM * (n + 1)]
        s = jnp.einsum("bqd,bkd->bqk", qs, kn, preferred_element_type=F32) * (HEAD_DIM ** -0.5)
        sink_col = jnp.full((1, rows, 1), sink_ref[layer, heads[-1]], F32)
        for g in reversed(range(A_GROUP - 1)):
            sink_col = jnp.where(rcol < (g + 1) * tdec, sink_ref[layer, heads[g]], sink_col)
        sm = jnp.where(ok, s, -1e30)
        m = jnp.maximum(jnp.max(sm, axis=-1, keepdims=True), sink_col)
        p = jnp.where(ok, jnp.exp(sm - m), 0.0)
        den = jnp.sum(p, axis=-1, keepdims=True) + jnp.exp(sink_col - m)
        o = jnp.einsum("bqk,bkd->bqd", p.astype(BF16), vn, preferred_element_type=F32) / den
        outs += [o[:, g * tdec:(g + 1) * tdec, :].reshape(nseq * tdec, HEAD_DIM)
                 for g in range(A_GROUP)]
    o_all = jnp.concatenate(outs, axis=1)
    o_ref[...] = (o_all * _silu(ga_ref[...])).astype(BF16)


def _attn_sample(proj, row0, oa, cache_k, cache_v, sinks, qg_all, kg_all, l, carriers,
                 nbatch, tdec, nseq=32):
    tm = nseq * tdec
    rb = row0 // tm
    cb = lambda c, w: c // w
    car = [oa] + list(carriers)
    cache_spec = pl.BlockSpec((None, nseq, WINDOW, KV_WIDTH), lambda s: (l, s, 0, 0))
    return pl.pallas_call(
        _skip(len(car), functools.partial(_attn_sample_kernel, nseq=nseq, tdec=tdec, layer=l)),
        grid=(nbatch // nseq,),
        in_specs=[_CARRIER] * len(car)
        + [pl.BlockSpec(memory_space=pltpu.SMEM),
           pl.BlockSpec((tm, A_WIDTH), lambda s: (rb + s, cb(COL_Q, A_WIDTH))),
           pl.BlockSpec((tm, A_WIDTH), lambda s: (rb + s, cb(COL_GA, A_WIDTH))),
           pl.BlockSpec((tm, KV_WIDTH), lambda s: (rb + s, cb(COL_K, KV_WIDTH))),
           pl.BlockSpec((tm, KV_WIDTH), lambda s: (rb + s, cb(COL_V, KV_WIDTH))),
           cache_spec, cache_spec,
           pl.BlockSpec((None, 1, A_WIDTH), lambda s: (l, 0, 0)),
           pl.BlockSpec((None, 1, KV_WIDTH), lambda s: (l, 0, 0))],
        out_specs=[pl.BlockSpec((tm, A_WIDTH), lambda s: (rb + s, 0)), cache_spec, cache_spec],
        out_shape=[jax.ShapeDtypeStruct(oa.shape, BF16),
                   jax.ShapeDtypeStruct((DEPTH, nbatch, WINDOW, KV_WIDTH), F32),
                   jax.ShapeDtypeStruct((DEPTH, nbatch, WINDOW, KV_WIDTH), F32)],
        input_output_aliases={c: c for c in range(len(car))},
        compiler_params=_params(1),
        name="attn_sample",
    )(*car, sinks, proj, proj, proj, proj, cache_k, cache_v, qg_all, kg_all)


def _stack_heads(x):
    lo = jnp.where(_lane_half_mask(), 1.0, 0.0).astype(x.dtype)
    return jnp.concatenate([x * lo, x * (1 - lo)], axis=0)


def _cat_rows(parts, rows):
    if (rows.stop - rows.start) % 16 == 0:
        return jnp.concatenate([x[rows] for x in parts], axis=0)
    return jnp.concatenate([x.astype(F32)[rows] for x in parts], axis=0).astype(BF16)


def _split3(x):
    hi = x.astype(BF16)
    r1 = x - hi.astype(F32)
    mid = r1.astype(BF16)
    lo = (r1 - mid.astype(F32)).astype(BF16)
    return hi, mid, lo


def _rwkv_masks(blk):
    lb = blk.bit_length() - 1
    t = lax.broadcasted_iota(jnp.int32, (TILE, LANES), 0)
    s = lax.broadcasted_iota(jnp.int32, (TILE, LANES), 1) & (TILE - 1)
    same = (t >> lb) == (s >> lb)
    levels = [((t >> (l + 1)) == (s >> (l + 1))) & (((t >> l) & 1) == 1) & (((s >> l) & 1) == 0)
              for l in range(lb)]
    tt = lax.broadcasted_iota(jnp.int32, (2 * TILE, 3 * TILE), 0)
    ss = lax.broadcasted_iota(jnp.int32, (2 * TILE, 3 * TILE), 1)
    ti = tt & (TILE - 1)
    si = jnp.where(ss >= 2 * TILE, ss - 2 * TILE, jnp.where(ss >= TILE, ss - TILE, ss))
    sel = ((ti >> lb) == (si >> lb)) & ((tt >= TILE) | (si <= ti))
    r2 = lax.broadcasted_iota(jnp.int32, (LANES, LANES), 0)
    c2 = lax.broadcasted_iota(jnp.int32, (LANES, LANES), 1)
    return dict(lb=lb, strict=same & (s < t), incl=same & (s <= t),
                eye=jnp.where(s == t, 1.0, 0.0), levels=levels,
                cumsel=jnp.where(sel, 1.0, 0.0).astype(BF16),
                same_head=(r2 >> 6) == (c2 >> 6))


def _rwkv_tile(xr, xk, xv, xw, gate, prm, states, blk):
    nchunk = xr.shape[0] // TILE
    nseq = TILE // blk
    mk = _rwkv_masks(blk)
    lw = _dot(jnp.tanh(xw).astype(BF16), prm["w2"])
    la = _dot(xw.astype(BF16), prm["a2"])
    ld = (-math.exp(-0.5) * math.log2(math.e)) * _sigmoid(prm["w0"] + lw)
    a = _sigmoid(prm["a0"] + la)
    kkr = xk * prm["k_k"]
    kk = kkr * lax.rsqrt(jnp.maximum(_head_sums_wide(kkr * kkr), 1e-24))
    kmod = xk * (1.0 + (a - 1.0) * prm["k_a"])
    ka = kk * a

    probs = []
    for ch in range(nchunk):
        rows = slice(ch * TILE, (ch + 1) * TILE)
        cum = _dot(mk["cumsel"], jnp.concatenate(_split3(ld[rows]), axis=0))
        c, cl = cum[:TILE], cum[TILE:]
        e_c = jnp.exp2(-c)
        e_l = jnp.exp2(cl - c)
        full = dict(rt=(xr[rows] * jnp.exp2(c)).astype(BF16),
                    at=(-kk[rows] * jnp.exp2(c - ld[rows])).astype(BF16),
                    bt=(ka[rows] * e_c).astype(BF16), kt=(kmod[rows] * e_c).astype(BF16),
                    bh=(ka[rows] * e_l).astype(BF16), kh=(kmod[rows] * e_l).astype(BF16),
                    v=xv[rows].astype(BF16), decay=jnp.exp2(cl))
        for j in range(N_PAIRS):
            sl = slice(j * LANES, (j + 1) * LANES)
            probs.append(dict(ch=ch, j=j, **{n: x[:, sl] for n, x in full.items()}))

    for p in probs:
        p["v_s"] = _stack_heads(p["v"])
        z_s = jnp.concatenate([_stack_heads(p["bt"]), _stack_heads(p["kt"])], axis=0)
        gram = _dot_nt(jnp.concatenate([p["at"], p["rt"]], axis=0), z_s)
        p["a_ab"] = jnp.where(mk["strict"], gram[:TILE, :LANES], 0.0)
        p["a_ak"] = jnp.where(mk["strict"], gram[:TILE, LANES:], 0.0).astype(BF16)
        p["a_r"] = jnp.concatenate([jnp.where(mk["incl"], gram[TILE:, :LANES], 0.0),
                                    jnp.where(mk["incl"], gram[TILE:, LANES:], 0.0)],
                                   axis=1).astype(BF16)
        p["tinv"] = mk["eye"]
        if mk["lb"] > 0:
            p["tinv"] = p["tinv"] + jnp.where(mk["levels"][0], p["a_ab"], 0.0)

    for l in range(1, mk["lb"]):
        for p in probs:
            p["tb"] = p["tinv"].astype(BF16)
            mid = _stack_heads(jnp.where(mk["levels"][l], p["a_ab"], 0.0).astype(BF16))
            p["half"] = _dot(p["tb"], mid).astype(BF16)
        for p in probs:
            p["tinv"] = p["tinv"] + _dot(p["half"], _stack_heads(p["tb"]))

    for p in probs:
        p["akv"] = _dot(p["a_ak"], p["v_s"]).astype(BF16)
    for p in probs:
        p["wu"] = _dot(p["tinv"].astype(BF16),
                       jnp.concatenate([_stack_heads(p["at"]), _stack_heads(p["akv"])], axis=1))
        p["w"], p["u0"] = p["wu"][:, :LANES].astype(BF16), p["wu"][:, LANES:]

    carried = nseq == 1
    if carried:
        for p in probs:
            p["wtb"] = _dot_tn(p["wu"].astype(BF16), p["bh"])
        for p in probs:
            vtk = _dot_tn(p["v"], p["kh"])
            p["m"] = jnp.where(mk["same_head"], p["wtb"][:LANES], 0.0).astype(BF16)
            p["c"] = jnp.where(mk["same_head"], p["wtb"][LANES:] + vtk, 0.0)

    def read_state(p, sts):
        u_parts, rs_parts = [], []
        for q in range(nseq):
            rows = slice(q * blk, (q + 1) * blk)
            res = _dot_nt(_cat_rows([p["w"], p["rt"]], rows), sts[q].astype(BF16))
            u_parts.append(res[:blk] + p["u0"][rows])
            rs_parts.append(res[blk:])
        p["u"] = (jnp.concatenate(u_parts, axis=0) if nseq > 1 else u_parts[0]).astype(BF16)
        p["rs"] = jnp.concatenate(rs_parts, axis=0) if nseq > 1 else rs_parts[0]

    states = [list(st) for st in states]
    y_rows = []
    for ch in range(nchunk):
        cps = [p for p in probs if p["ch"] == ch]
        first = 0 if carried else ch * nseq
        start = [states[p["j"]][first:first + nseq] for p in cps]
        if carried:
            for p in cps:
                s = states[p["j"]][0]
                states[p["j"]][0] = s * p["decay"][0:1] + _dot(s.astype(BF16), p["m"]) + p["c"]
        for p, sts in zip(cps, start):
            read_state(p, sts)
        if not carried:
            for p in cps:
                for q in range(nseq):
                    rows = slice(q * blk, (q + 1) * blk)
                    upd = _dot_tn(_cat_rows([p["u"], p["v"]], rows),
                                  _cat_rows([p["bh"], p["kh"]], rows))
                    states[p["j"]][first + q] = (
                        states[p["j"]][first + q] * p["decay"][q * blk:q * blk + 1]
                        + jnp.where(mk["same_head"], upd, 0.0))
        y_rows.append(jnp.concatenate(
            [_dot(p["a_r"], jnp.concatenate([_stack_heads(p["u"]), p["v_s"]], axis=0)) + p["rs"]
             for p in cps], axis=1))
    y = jnp.concatenate(y_rows, axis=0) if nchunk > 1 else y_rows[0]

    mean = _head_sums_wide(y) * (1.0 / HEAD_DIM)
    yc = y - mean
    var = _head_sums_wide(yc * yc) * (1.0 / HEAD_DIM)
    yn = yc * lax.rsqrt(var + GN_EPS) * prm["gn_w"] + prm["gn_b"]
    yn = yn + _head_sums_wide(xr * kmod * prm["r_k"]) * xv
    return (yn * _silu(gate)).astype(BF16), states


_RWKV_PARAM_NAMES = ("mu", "w0", "w2", "a0", "a2", "k_k", "k_a", "r_k", "gn_w", "gn_b")


def _rwkv_shift_mix(cur, prev_rows, first_row_mask, mu):
    shifted = jnp.where(first_row_mask, prev_rows, pltpu.roll(cur, 1, axis=0))
    return cur + (shifted - cur) * mu


def _rwkv_load_params(refs):
    prm = {n: r[...] for n, r in zip(_RWKV_PARAM_NAMES, refs)}
    mu = prm.pop("mu")
    return prm, (mu[:, 0:B_WIDTH], mu[:, B_WIDTH:2 * B_WIDTH],
                 mu[:, 2 * B_WIDTH:3 * B_WIDTH], mu[:, 3 * B_WIDTH:])


def _rwkv_prompt_kernel(r_ref, k_ref, v_ref, g_ref, w_ref, *rest):
    prm_refs, (o_ref, s_out_ref, sh_out_ref, state, prev) = (rest[:len(_RWKV_PARAM_NAMES)],
                                                               rest[len(_RWKV_PARAM_NAMES):])
    i = pl.program_id(1)

    @pl.when(i == 0)
    def _():
        state[...] = jnp.zeros_like(state)
        prev[...] = jnp.zeros_like(prev)

    prm, mus = _rwkv_load_params(prm_refs)
    cur = (r_ref[...], k_ref[...], v_ref[...], w_ref[...])
    rows = cur[0].shape[0]
    first = lax.broadcasted_iota(jnp.int32, (8, 1), 0) == 0
    offs = (0, B_WIDTH, 2 * B_WIDTH, 3 * B_WIDTH, SHIFT_PAD)
    mixed = []
    for n, x in enumerate(cur):
        m = x + (pltpu.roll(x, 1, axis=0) - x) * mus[n]
        top = x[0:8]
        top = jnp.where(first, top + (prev[0:1, offs[n]:offs[n + 1]] - top) * mus[n], m[0:8])
        mixed.append(jnp.concatenate([top, m[8:]], axis=0))
        prev[0:1, offs[n]:offs[n + 1]] = x[rows - 1:rows, :]
    states = [[state[j]] for j in range(N_PAIRS)]
    out, new_states = _rwkv_tile(*mixed, g_ref[...], prm, states, TILE)
    o_ref[...] = out
    for j in range(N_PAIRS):
        s = new_states[j][0]
        state[j] = s
        s_out_ref[2 * j] = s[:HEAD_DIM, :HEAD_DIM]
        s_out_ref[2 * j + 1] = s[HEAD_DIM:, HEAD_DIM:]
    sh_out_ref[...] = prev[...]


def _rwkv_sample_kernel(r_ref, k_ref, v_ref, g_ref, w_ref, sh_ref, s_in_ref, *rest, tdec):
    prm_refs, (o_ref, s_out_ref, sh_out_ref) = (rest[:len(_RWKV_PARAM_NAMES)],
                                                rest[len(_RWKV_PARAM_NAMES):])
    rows = r_ref.shape[0]
    nseq = rows // tdec
    prm, mus = _rwkv_load_params(prm_refs)
    cur = (r_ref[...], k_ref[...], v_ref[...], w_ref[...])
    first = (lax.broadcasted_iota(jnp.int32, (rows, 1), 0) & (tdec - 1)) == 0
    offs = (0, B_WIDTH, 2 * B_WIDTH, 3 * B_WIDTH, SHIFT_PAD)
    sh = sh_ref[...]
    mixed = []
    for n, x in enumerate(cur):
        prev_rows = jnp.concatenate(
            [jnp.broadcast_to(sh[q:q + 1, offs[n]:offs[n + 1]], (tdec, x.shape[1]))
             for q in range(nseq)], axis=0)
        mixed.append(_rwkv_shift_mix(x, prev_rows, first, mus[n]))
        for q in range(nseq):
            last = (q + 1) * tdec - 1
            sh_out_ref[q:q + 1, offs[n]:offs[n + 1]] = x[last:last + 1, :]
    zero = jnp.zeros((HEAD_DIM, HEAD_DIM), F32)
    states = []
    for j in range(N_PAIRS):
        pair = []
        for q in range(nseq):
            top = jnp.concatenate([s_in_ref[q, 2 * j], zero], axis=1)
            bot = jnp.concatenate([zero, s_in_ref[q, 2 * j + 1]], axis=1)
            pair.append(jnp.concatenate([top, bot], axis=0))
        states.append(pair)
    out, new_states = _rwkv_tile(*mixed, g_ref[...], prm, states, tdec)
    o_ref[...] = out
    for j in range(N_PAIRS):
        for q in range(nseq):
            s = new_states[j][q]
            s_out_ref[q, 2 * j] = s[:HEAD_DIM, :HEAD_DIM]
            s_out_ref[q, 2 * j + 1] = s[HEAD_DIM:, HEAD_DIM:]


def _rwkv_param_specs(l):
    shapes = {"mu": (None, 1, SHIFT_PAD), "w2": (None, LANES, B_WIDTH), "a2": (None, LANES, B_WIDTH)}
    return [pl.BlockSpec(shapes.get(n, (None, 1, B_WIDTH)), lambda *_: (l, 0, 0))
            for n in _RWKV_PARAM_NAMES]


def _rwkv_col_specs(row_fn, rows):
    cols = (COL_R, COL_KB, COL_VB, COL_GB)
    specs = [pl.BlockSpec((rows, B_WIDTH), (lambda *a, c=c: (row_fn(*a), c // B_WIDTH))) for c in cols]
    specs.append(pl.BlockSpec((rows, LANES), lambda *a: (row_fn(*a), COL_WLAL // LANES)))
    return specs


def _rwkv_prompt(proj, prm, l, carriers, batch, seq, rows=RWKV_PROMPT_ROWS):
    nt = seq // rows
    row_fn = lambda b, i: b * nt + i
    ncar = len(carriers)
    return pl.pallas_call(
        _skip(ncar, _rwkv_prompt_kernel),
        grid=(batch, nt),
        in_specs=[_CARRIER] * ncar + _rwkv_col_specs(row_fn, rows) + _rwkv_param_specs(l),
        out_specs=[pl.BlockSpec((rows, B_WIDTH), lambda b, i: (row_fn(b, i), 0)),
                   pl.BlockSpec((None, None, B_HEADS, HEAD_DIM, HEAD_DIM),
                                lambda b, i: (l, b, 0, 0, 0)),
                   pl.BlockSpec((None, None, 8, SHIFT_PAD), lambda b, i: (l, b, 0, 0))],
        out_shape=[jax.ShapeDtypeStruct((proj.shape[0], B_WIDTH), BF16),
                   jax.ShapeDtypeStruct((DEPTH, batch, B_HEADS, HEAD_DIM, HEAD_DIM), F32),
                   jax.ShapeDtypeStruct((DEPTH, batch, 8, SHIFT_PAD), F32)],
        scratch_shapes=[pltpu.VMEM((N_PAIRS, LANES, LANES), F32),
                        pltpu.VMEM((8, SHIFT_PAD), F32)],
        input_output_aliases={c: 1 + c for c in range(ncar)},
        compiler_params=_params(2),
        name="rwkv_prompt",
    )(*carriers, proj, proj, proj, proj, proj, *[prm[n] for n in _RWKV_PARAM_NAMES])


def _rwkv_sample(proj, row0, ob, shift_in, state_in, prm, l, carriers, nbatch, tdec):
    rows = RWKV_SAMPLE_ROWS
    nseq = rows // tdec
    rb = row0 // rows
    row_fn = lambda s: rb + s
    car = [ob] + list(carriers)
    state_spec = pl.BlockSpec((None, nseq, B_HEADS, HEAD_DIM, HEAD_DIM), lambda s: (l, s, 0, 0, 0))
    shift_spec = pl.BlockSpec((None, nseq, SHIFT_PAD), lambda s: (l, s, 0))
    return pl.pallas_call(
        _skip(len(car), functools.partial(_rwkv_sample_kernel, tdec=tdec)),
        grid=(nbatch // nseq,),
        in_specs=[_CARRIER] * len(car) + _rwkv_col_specs(row_fn, rows)
        + [shift_spec, state_spec] + _rwkv_param_specs(l),
        out_specs=[pl.BlockSpec((rows, B_WIDTH), lambda s: (rb + s, 0)), state_spec, shift_spec],
        out_shape=[jax.ShapeDtypeStruct(ob.shape, BF16),
                   jax.ShapeDtypeStruct((DEPTH, nbatch, B_HEADS, HEAD_DIM, HEAD_DIM), F32),
                   jax.ShapeDtypeStruct((DEPTH, nbatch, SHIFT_PAD), F32)],
        input_output_aliases={c: c for c in range(len(car))},
        compiler_params=_params(1),
        name="rwkv_sample",
    )(*car, proj, proj, proj, proj, proj, shift_in, state_in,
      *[prm[n] for n in _RWKV_PARAM_NAMES])


def _window_sums(xe):
    s2 = xe + pltpu.roll(xe, 1, axis=0)
    s4 = s2 + pltpu.roll(s2, 2, axis=0)
    s8 = s4 + pltpu.roll(s4, 4, axis=0)
    s16 = s8 + pltpu.roll(s8, 8, axis=0)
    lane = lax.broadcasted_iota(jnp.int32, (1, C_WIDTH), 1)
    return jnp.where(lane < 64, s2, jnp.where(lane < 128, s4, jnp.where(lane < 192, s8, s16)))


def _pool_window_lane():
    lane = lax.broadcasted_iota(jnp.int32, (1, C_WIDTH), 1)
    return jnp.where(lane < 64, 2, jnp.where(lane < 128, 4, jnp.where(lane < 192, 8, 16)))


def _pool_prompt_rows(u, halo, gate, wbd, scale, it):
    tm = u.shape[0]
    halo = jnp.where(it > 0, halo, 0.0)
    sums = _window_sums(jnp.concatenate([halo, u], axis=0))[16:]
    pos = it * tm + lax.broadcasted_iota(jnp.int32, (tm, 1), 0)
    cnt = jnp.minimum(_pool_window_lane(), pos + 1).astype(F32)
    d = sums / cnt - u
    y = _dot(d.astype(BF16), wbd) * scale
    return (y * _silu(gate)).astype(BF16)


def _pool_sample_kernel(u_ref, h_ref, g_ref, w_ref, sc_ref, o_ref, *, nseq, tdec, pos0):
    u = u_ref[...]
    hist = h_ref[...]
    xe = jnp.concatenate([hist, u.reshape(nseq, tdec, C_WIDTH)], axis=1)
    xe = xe.reshape(nseq * (16 + tdec), C_WIDTH)
    sums = _window_sums(xe).reshape(nseq, 16 + tdec, C_WIDTH)[:, 16:, :]
    sums = sums.reshape(nseq * tdec, C_WIDTH)
    t = lax.broadcasted_iota(jnp.int32, (nseq * tdec, 1), 0) & (tdec - 1)
    cnt = jnp.minimum(_pool_window_lane(), pos0 + t + 1).astype(F32)
    d = sums / cnt - u
    y = _dot(d.astype(BF16), w_ref[...]) * sc_ref[...]
    o_ref[...] = (y * _silu(g_ref[...])).astype(BF16)


def _pool_sample(proj, row0, hist16_all, wbd_all, scale_all, l, nbatch, tdec, pos0, nseq=64):
    tm = nseq * tdec
    rb = row0 // tm
    return pl.pallas_call(
        functools.partial(_pool_sample_kernel, nseq=nseq, tdec=tdec, pos0=pos0),
        grid=(nbatch // nseq,),
        in_specs=[pl.BlockSpec((tm, C_WIDTH), lambda s: (rb + s, COL_UC // C_WIDTH)),
                  pl.BlockSpec((None, nseq, 16, C_WIDTH), lambda s: (l, s, 0, 0)),
                  pl.BlockSpec((tm, C_WIDTH), lambda s: (rb + s, COL_GC // C_WIDTH)),
                  pl.BlockSpec((None, C_WIDTH, C_WIDTH), lambda s: (l, 0, 0)),
                  pl.BlockSpec((None, 1, C_WIDTH), lambda s: (l, 0, 0))],
        out_specs=pl.BlockSpec((tm, C_WIDTH), lambda s: (s, 0)),
        out_shape=jax.ShapeDtypeStruct((nbatch * tdec, C_WIDTH), BF16),
        compiler_params=_params(1),
        name="pool_sample",
    )(proj, hist16_all, proj, wbd_all, scale_all)


def _pad_shift(x):
    pad = jnp.zeros(x.shape[:-1] + (LANES - 2 * LORA,), x.dtype)
    return jnp.concatenate([x, pad], axis=-1)


def _block_diag(w):
    n, g, c, _ = w.shape
    eye = jnp.eye(g, dtype=w.dtype)
    return (eye[None, :, None, :, None] * w[:, :, :, None, :]).reshape(n, g * c, g * c)


def kernel(x_prompt, x_sample, cache_k, cache_v, state_wkv, state_shift, state_pool, norm_g, w_in, q_norm_g, k_norm_g, attn_sinks, shift_mu, decay_w0, decay_w2, iclr_a0, iclr_a2, k_k, k_a, r_k, gn_w, gn_b, pool_w, pool_scale, w_out):
    batch, seq, _ = x_prompt.shape
    nbatch, tdec, _ = x_sample.shape
    wbuf = cache_k.shape[2]
    n_prompt, n_sample = batch * seq, nbatch * tdec
    n_total = n_prompt + n_sample

    row = lambda a: a.reshape(DEPTH, 1, -1)
    lora_pad = jnp.zeros((DEPTH, LANES - LORA, B_WIDTH), F32)
    prm = {
        "mu": row(_pad_shift(shift_mu)), "w0": row(decay_w0), "a0": row(iclr_a0),
        "w2": jnp.concatenate([decay_w2, lora_pad], axis=1).astype(BF16),
        "a2": jnp.concatenate([lora_pad[:, :LORA], iclr_a2, lora_pad[:, LORA:]], axis=1).astype(BF16),
        "k_k": row(k_k), "k_a": row(k_a), "r_k": row(r_k), "gn_w": row(gn_w), "gn_b": row(gn_b),
    }
    g_all = row(norm_g)
    qg_all = row(jnp.tile(q_norm_g, (1, A_HEADS)))
    kg_all = row(jnp.tile(k_norm_g, (1, A_KV_HEADS)))
    w_in_all = jnp.swapaxes(w_in, 1, 2)
    w_out_all = w_out.astype(BF16)
    wbd_all = _block_diag(pool_w).astype(BF16)
    scale_all = row(pool_scale)
    ck_all = cache_k.reshape(DEPTH, nbatch, wbuf, KV_WIDTH)
    cv_all = cache_v.reshape(DEPTH, nbatch, wbuf, KV_WIDTH)
    shift_all = _pad_shift(state_shift)
    hist16_all = jnp.pad(state_pool, ((0, 0), (0, 0), (1, 0), (0, 0)))

    x_p = x_prompt.reshape(n_prompt, D_MODEL)
    x_s = x_sample.reshape(n_sample, D_MODEL)
    kv_p, kv_s, wkv_p, wkv_s = [], [], [], []
    pool_p, pool_s = [], []
    proj = _inproj(x_p, g_all, w_in_all, 0, n_total, 0)
    for l in range(DEPTH):
        proj = _inproj(x_s, g_all, w_in_all, l, n_total, n_prompt, carrier=proj)

        oa, *kv_p = _attn_prompt(proj, attn_sinks, qg_all, kg_all, l, kv_p, batch, seq)
        oa, *kv_s = _attn_sample(proj, n_prompt, oa, ck_all, cv_all, attn_sinks, qg_all, kg_all,
                                 l, kv_s, nbatch, tdec)
        ob, *wkv_p = _rwkv_prompt(proj, prm, l, wkv_p, batch, seq)
        ob, *wkv_s = _rwkv_sample(proj, n_prompt, ob, shift_all, state_wkv, prm, l, wkv_s,
                                  nbatch, tdec)
        oc_s = _pool_sample(proj, n_prompt, hist16_all, wbd_all, scale_all, l, nbatch, tdec,
                            PAST_LEN)
        if l + 1 < DEPTH:
            x_p, proj_next = _outproj_prompt(x_p, oa, ob, proj, wbd_all, scale_all, w_out_all, l,
                                             seq, next_in=(g_all, w_in_all, n_total))
        else:
            x_p = _outproj_prompt(x_p, oa, ob, proj, wbd_all, scale_all, w_out_all, l, seq)
        x_s = _outproj_sample(x_s, oa, ob, oc_s, w_out_all, l, n_prompt)

        pool_p.append(jnp.stack([proj[(b + 1) * seq - POOL_HIST:(b + 1) * seq,
                                      COL_UC:COL_UC + C_WIDTH] for b in range(batch)]))
        u_s = proj[n_prompt:, COL_UC:COL_UC + C_WIDTH].reshape(nbatch, tdec, C_WIDTH)
        pool_s.append(jnp.concatenate([state_pool[l], u_s], axis=1)[:, -POOL_HIST:])
        if l + 1 < DEPTH:
            proj = proj_next

    heads = lambda a: a.reshape(a.shape[:-1] + (A_KV_HEADS, HEAD_DIM))
    return (x_p.reshape(batch, seq, D_MODEL), x_s.reshape(nbatch, tdec, D_MODEL),
            heads(kv_p[0]), heads(kv_p[1]), wkv_p[0], wkv_p[1][:, :, 0, :SHIFT_WIDTH],
            jnp.stack(pool_p),
            heads(kv_s[0]), heads(kv_s[1]), wkv_s[0], wkv_s[1][:, :, :SHIFT_WIDTH],
            jnp.stack(pool_s))
```

```python
import functools
import math

import jax
import jax.numpy as jnp
from jax import lax
from jax.experimental import pallas as pl
from jax.experimental.pallas import tpu as pltpu

F32 = jnp.float32
BF16 = jnp.bfloat16

D_MODEL = 1024
DEPTH = 4
HEAD_DIM = 64
A_HEADS = 6
A_KV_HEADS = 2
A_GROUP = A_HEADS // A_KV_HEADS
A_WIDTH = A_HEADS * HEAD_DIM
KV_WIDTH = A_KV_HEADS * HEAD_DIM
WINDOW = 128
B_HEADS = 6
B_WIDTH = B_HEADS * HEAD_DIM
LORA = 32
GN_EPS = 6.4e-4
SHIFT_WIDTH = 3 * B_WIDTH + 2 * LORA
C_WIDTH = 256
POOL_HIST = 15
NORM_EPS = 1e-6
PAST_LEN = 8192

LANES = 128
TILE = 64
RWKV_PROMPT_ROWS = 512
ATTN_PROMPT_ROWS = 1024
RWKV_SAMPLE_ROWS = 256
N_PAIRS = B_HEADS // 2

COL_Q, COL_GA, COL_R, COL_KB, COL_VB, COL_GB = 0, 384, 768, 1152, 1536, 1920
COL_K, COL_V, COL_UC, COL_GC, COL_WLAL = 2304, 2432, 2560, 2816, 3072
IN_PAD = 3200
SHIFT_PAD = 3 * B_WIDTH + LANES

VMEM_LIMIT = 48 * 1024 * 1024
VMEM_LIMIT_FUSED = 58 * 1024 * 1024


def _dot(a, b, prec=None):
    return jnp.dot(a, b, preferred_element_type=F32, precision=prec)


def _dot_nt(a, b, prec=None):
    return lax.dot_general(a, b, (((1,), (1,)), ((), ())),
                           preferred_element_type=F32, precision=prec)


def _dot_tn(a, b, prec=None):
    return lax.dot_general(a, b, (((0,), (0,)), ((), ())),
                           preferred_element_type=F32, precision=prec)


def _sigmoid(x):
    return 1.0 / (1.0 + jnp.exp(-x))


def _silu(x):
    return x * _sigmoid(x)


def _lane_half_mask(rows=1):
    lane = lax.broadcasted_iota(jnp.int32, (rows, LANES), 1)
    return lane < HEAD_DIM


def _head_sums(x):
    lo = _lane_half_mask()
    s0 = jnp.sum(jnp.where(lo, x, 0.0), axis=-1, keepdims=True)
    s1 = jnp.sum(jnp.where(lo, 0.0, x), axis=-1, keepdims=True)
    return jnp.where(lo, s0, s1)


def _head_sums_wide(x):
    n = x.shape[1] // LANES
    return jnp.concatenate(
        [_head_sums(x[:, j * LANES:(j + 1) * LANES]) for j in range(n)], axis=1)


def _head_rms(x, g):
    ms = _head_sums_wide(x * x) * (1.0 / HEAD_DIM)
    return x * lax.rsqrt(ms + NORM_EPS) * g


_W_SEGMENTS = ((0, 384, COL_Q), (384, 128, COL_K), (512, 128, COL_V), (640, 384, COL_GA),
               (1024, 3 * B_WIDTH, COL_R), (2176, 2 * LORA, COL_WLAL), (2240, 384, COL_GB),
               (2624, 256, COL_UC), (2880, 256, COL_GC))
IN_WIDTH = 3136


def _regroup_w_in(wt_ref, wb):
    @pl.when(pl.program_id(0) == 0)
    def _():
        for src, n, dst in _W_SEGMENTS:
            wb[dst:dst + n, :] = wt_ref[src:src + n, :].astype(BF16)
        pad0 = COL_WLAL + 2 * LORA
        wb[pad0:IN_PAD, :] = jnp.zeros((IN_PAD - pad0, D_MODEL), BF16)


def _rms_rows(x, g):
    ms = jnp.mean(x * x, axis=-1, keepdims=True)
    return (x * lax.rsqrt(ms + NORM_EPS) * g).astype(BF16)


def _inproj_kernel(x_ref, g_ref, wt_ref, o_ref, wb):
    _regroup_w_in(wt_ref, wb)
    o_ref[...] = _dot_nt(_rms_rows(x_ref[...], g_ref[...]), wb[...])


def _skip(n, fn):
    def wrapped(*refs, **kw):
        return fn(*refs[n:], **kw)
    return wrapped


_CARRIER = pl.BlockSpec(memory_space=pl.ANY)


def _params(ndims):
    return pltpu.CompilerParams(dimension_semantics=("arbitrary",) * ndims,
                                vmem_limit_bytes=VMEM_LIMIT)


def _inproj(x, g_all, wt_all, l, n_total, row0, carrier=None, tm=512):
    n = x.shape[0]
    rb = row0 // tm
    car = [] if carrier is None else [carrier]
    return pl.pallas_call(
        _skip(len(car), _inproj_kernel),
        grid=(n // tm,),
        in_specs=[_CARRIER] * len(car)
        + [pl.BlockSpec((tm, D_MODEL), lambda i: (i, 0)),
           pl.BlockSpec((None, 1, D_MODEL), lambda i: (l, 0, 0)),
           pl.BlockSpec((None, IN_WIDTH, D_MODEL), lambda i: (l, 0, 0),
                        pipeline_mode=pl.Buffered(1))],
        out_specs=pl.BlockSpec((tm, IN_PAD), lambda i: (rb + i, 0)),
        out_shape=jax.ShapeDtypeStruct((n_total, IN_PAD), F32),
        scratch_shapes=[pltpu.VMEM((IN_PAD, D_MODEL), BF16)],
        input_output_aliases={0: 0} if car else {},
        compiler_params=_params(1),
        name="inproj",
    )(*car, x, g_all, wt_all)


def _outproj_kernel(x_ref, oa_ref, ob_ref, oc_ref, wa_ref, wb_ref, wc_ref, o_ref):
    acc = _dot(oa_ref[...], wa_ref[...])
    acc += _dot(ob_ref[...], wb_ref[...])
    acc += _dot(oc_ref[...], wc_ref[...])
    o_ref[...] = x_ref[...] + acc


def _outproj_pool_rows(x_ref, oa_ref, ob_ref, u_ref, halo_ref, g_ref, wp_ref, sc_ref,
                       wa_ref, wb_ref, wc_ref, tile, tiles_per_seq):
    it = lax.rem(tile, tiles_per_seq)
    oc = _pool_prompt_rows(u_ref[...], halo_ref[...], g_ref[...], wp_ref[...], sc_ref[...], it)
    acc = _dot(oa_ref[...], wa_ref[...])
    acc += _dot(ob_ref[...], wb_ref[...])
    acc += _dot(oc, wc_ref[...])
    return x_ref[...] + acc


def _outproj_pool_kernel(*refs, tiles_per_seq):
    *ins, o_ref = refs
    o_ref[...] = _outproj_pool_rows(*ins, pl.program_id(0), tiles_per_seq)


def _outproj_inproj_kernel(*refs, tiles_per_seq):
    *ins, ng_ref, wt_ref, xo_ref, po_ref, wb = refs
    _regroup_w_in(wt_ref, wb)
    x_new = _outproj_pool_rows(*ins, pl.program_id(0), tiles_per_seq)
    xo_ref[...] = x_new
    po_ref[...] = _dot_nt(_rms_rows(x_new, ng_ref[...]), wb[...])


def _w_out_specs(l):
    return [pl.BlockSpec((None, A_WIDTH, D_MODEL), lambda i: (l, 0, 0)),
            pl.BlockSpec((None, B_WIDTH, D_MODEL), lambda i: (l, 1, 0)),
            pl.BlockSpec((None, C_WIDTH, D_MODEL), lambda i: (l, 3, 0))]


def _outproj_prompt(x, oa, ob, proj, wp_all, sc_all, w_all, l, seq, next_in=None, tm=512):
    n = x.shape[0]
    ntiles = n // tm
    halo_row = lambda i: jnp.maximum(i * (tm // 16) - 1, 0)
    in_specs = [pl.BlockSpec((tm, D_MODEL), lambda i: (i, 0)),
                pl.BlockSpec((tm, A_WIDTH), lambda i: (i, 0)),
                pl.BlockSpec((tm, B_WIDTH), lambda i: (i, 0)),
                pl.BlockSpec((tm, C_WIDTH), lambda i: (i, COL_UC // C_WIDTH)),
                pl.BlockSpec((16, C_WIDTH), lambda i: (halo_row(i), COL_UC // C_WIDTH)),
                pl.BlockSpec((tm, C_WIDTH), lambda i: (i, COL_GC // C_WIDTH)),
                pl.BlockSpec((None, C_WIDTH, C_WIDTH), lambda i: (l, 0, 0)),
                pl.BlockSpec((None, 1, C_WIDTH), lambda i: (l, 0, 0))] + _w_out_specs(l)
    args = [x, oa, ob, proj, proj, proj, wp_all, sc_all, w_all, w_all, w_all]
    x_spec = pl.BlockSpec((tm, D_MODEL), lambda i: (i, 0))
    x_shape = jax.ShapeDtypeStruct((n, D_MODEL), F32)
    if next_in is None:
        return pl.pallas_call(
            functools.partial(_outproj_pool_kernel, tiles_per_seq=seq // tm),
            grid=(ntiles,), in_specs=in_specs, out_specs=x_spec, out_shape=x_shape,
            compiler_params=_params(1), name="outproj_prompt")(*args)
    g_all, wt_all, n_total = next_in
    return pl.pallas_call(
        functools.partial(_outproj_inproj_kernel, tiles_per_seq=seq // tm),
        grid=(ntiles,),
        in_specs=in_specs
        + [pl.BlockSpec((None, 1, D_MODEL), lambda i: (l + 1, 0, 0)),
           pl.BlockSpec((None, IN_WIDTH, D_MODEL), lambda i: (l + 1, 0, 0),
                        pipeline_mode=pl.Buffered(1))],
        out_specs=[x_spec, pl.BlockSpec((tm, IN_PAD), lambda i: (i, 0))],
        out_shape=[x_shape, jax.ShapeDtypeStruct((n_total, IN_PAD), F32)],
        scratch_shapes=[pltpu.VMEM((IN_PAD, D_MODEL), BF16)],
        compiler_params=pltpu.CompilerParams(dimension_semantics=("arbitrary",),
                                             vmem_limit_bytes=VMEM_LIMIT_FUSED),
        name="outproj_inproj",
    )(*args, g_all, wt_all)


def _outproj_sample(x, oa, ob, oc, w_all, l, row0, tm=512):
    n = x.shape[0]
    rb = row0 // tm
    return pl.pallas_call(
        _outproj_kernel,
        grid=(n // tm,),
        in_specs=[pl.BlockSpec((tm, D_MODEL), lambda i: (i, 0)),
                  pl.BlockSpec((tm, A_WIDTH), lambda i: (rb + i, 0)),
                  pl.BlockSpec((tm, B_WIDTH), lambda i: (rb + i, 0)),
                  pl.BlockSpec((tm, C_WIDTH), lambda i: (i, 0))] + _w_out_specs(l),
        out_specs=pl.BlockSpec((tm, D_MODEL), lambda i: (i, 0)),
        out_shape=jax.ShapeDtypeStruct((n, D_MODEL), F32),
        compiler_params=_params(1),
        name="outproj_sample",
    )(x, oa, ob, oc, w_all, w_all, w_all)


def _attn_prompt_kernel(sink_ref, q_ref, ga_ref, k_ref, v_ref, qg_ref, kg_ref,
                        o_ref, kl_ref, vl_ref, kprev, vprev, *, layer):
    i = pl.program_id(1)

    @pl.when(i == 0)
    def _():
        kprev[...] = jnp.zeros_like(kprev)
        vprev[...] = jnp.zeros_like(vprev)

    nblk = q_ref.shape[0] // WINDOW
    k = _head_rms(k_ref[...], kg_ref[...])
    v = v_ref[...]
    qn = (_head_rms(q_ref[...], qg_ref[...]) * (HEAD_DIM ** -0.5)).astype(BF16)
    kc = jnp.concatenate([kprev[...], k], axis=0).astype(BF16)
    vc = jnp.concatenate([vprev[...], v], axis=0).astype(BF16)

    rows = A_GROUP * WINDOW
    r = lax.broadcasted_iota(jnp.int32, (rows, 2 * WINDOW), 0)
    j = lax.broadcasted_iota(jnp.int32, (rows, 2 * WINDOW), 1)
    rel = (r & (WINDOW - 1)) + WINDOW - j
    band = (rel >= 0) & (rel < WINDOW)
    band_first = band & ((j >= WINDOW) | (i > 0))
    rcol = lax.broadcasted_iota(jnp.int32, (rows, 1), 0)

    probs = [dict(b=b, n=n) for b in range(nblk) for n in range(A_KV_HEADS)]
    for p in probs:
        b, n = p["b"], p["n"]
        heads = [A_GROUP * n + g for g in range(A_GROUP)]
        qs = jnp.concatenate([qn[b * WINDOW:(b + 1) * WINDOW, HEAD_DIM * h:HEAD_DIM * (h + 1)]
                              for h in heads], axis=0)
        keys = slice(b * WINDOW, (b + 2) * WINDOW)
        p["s"] = _dot_nt(qs, kc[keys, HEAD_DIM * n:HEAD_DIM * (n + 1)])
        p["v"] = vc[keys, HEAD_DIM * n:HEAD_DIM * (n + 1)]
        sink_col = jnp.full((rows, 1), sink_ref[layer, heads[-1]], F32)
        for g in reversed(range(A_GROUP - 1)):
            sink_col = jnp.where(rcol < (g + 1) * WINDOW, sink_ref[layer, heads[g]], sink_col)
        p["sink"] = sink_col
    for p in probs:
        sm = jnp.where(band_first if p["b"] == 0 else band, p["s"], -1e30)
        m = jnp.maximum(jnp.max(sm, axis=-1, keepdims=True), p["sink"])
        e = jnp.exp(sm - m)
        p["den"] = jnp.sum(e, axis=-1, keepdims=True) + jnp.exp(p["sink"] - m)
        p["e"] = e.astype(BF16)
    for p in probs:
        p["o"] = _dot(p["e"], p["v"]) / p["den"]
    o_rows = []
    for b in range(nblk):
        outs = []
        for p in probs[b * A_KV_HEADS:(b + 1) * A_KV_HEADS]:
            outs += [p["o"][g * WINDOW:(g + 1) * WINDOW] for g in range(A_GROUP)]
        o_rows.append(jnp.concatenate(outs, axis=1))
    o_all = jnp.concatenate(o_rows, axis=0) if nblk > 1 else o_rows[0]
    o_ref[...] = (o_all * _silu(ga_ref[...])).astype(BF16)

    last = slice((nblk - 1) * WINDOW, nblk * WINDOW)
    kprev[...] = k[last]
    vprev[...] = v[last]
    kl_ref[...] = k[last]
    vl_ref[...] = v[last]


def _attn_prompt(proj, sinks, qg_all, kg_all, l, carriers, batch, seq):
    tm = ATTN_PROMPT_ROWS
    nb = seq // tm
    n_total = proj.shape[0]
    cb = lambda c, w: c // w
    row = lambda b, i: b * nb + i
    ncar = len(carriers)
    return pl.pallas_call(
        _skip(ncar, functools.partial(_attn_prompt_kernel, layer=l)),
        grid=(batch, nb),
        in_specs=[_CARRIER] * ncar
        + [pl.BlockSpec(memory_space=pltpu.SMEM),
           pl.BlockSpec((tm, A_WIDTH), lambda b, i: (row(b, i), cb(COL_Q, A_WIDTH))),
           pl.BlockSpec((tm, A_WIDTH), lambda b, i: (row(b, i), cb(COL_GA, A_WIDTH))),
           pl.BlockSpec((tm, KV_WIDTH), lambda b, i: (row(b, i), cb(COL_K, KV_WIDTH))),
           pl.BlockSpec((tm, KV_WIDTH), lambda b, i: (row(b, i), cb(COL_V, KV_WIDTH))),
           pl.BlockSpec((None, 1, A_WIDTH), lambda b, i: (l, 0, 0)),
           pl.BlockSpec((None, 1, KV_WIDTH), lambda b, i: (l, 0, 0))],
        out_specs=[pl.BlockSpec((tm, A_WIDTH), lambda b, i: (row(b, i), 0)),
                   pl.BlockSpec((None, None, WINDOW, KV_WIDTH), lambda b, i: (l, b, 0, 0)),
                   pl.BlockSpec((None, None, WINDOW, KV_WIDTH), lambda b, i: (l, b, 0, 0))],
        out_shape=[jax.ShapeDtypeStruct((n_total, A_WIDTH), BF16),
                   jax.ShapeDtypeStruct((DEPTH, batch, WINDOW, KV_WIDTH), F32),
                   jax.ShapeDtypeStruct((DEPTH, batch, WINDOW, KV_WIDTH), F32)],
        scratch_shapes=[pltpu.VMEM((WINDOW, KV_WIDTH), F32),
                        pltpu.VMEM((WINDOW, KV_WIDTH), F32)],
        input_output_aliases={c: 1 + c for c in range(ncar)},
        compiler_params=_params(2),
        name="attn_prompt",
    )(*carriers, sinks, proj, proj, proj, proj, qg_all, kg_all)


def _attn_sample_kernel(sink_ref, q_ref, ga_ref, k_ref, v_ref, ck_ref, cv_ref, qg_ref, kg_ref,
                        o_ref, nk_ref, nv_ref, *, nseq, tdec, layer):
    k = _head_rms(k_ref[...], kg_ref[...])
    v = v_ref[...]
    qn = _head_rms(q_ref[...], qg_ref[...])
    ck = ck_ref[...]
    cv = cv_ref[...]
    k3 = k.reshape(nseq, tdec, KV_WIDTH)
    v3 = v.reshape(nseq, tdec, KV_WIDTH)
    kc = jnp.concatenate([ck, k3], axis=1)
    vc = jnp.concatenate([cv, v3], axis=1)
    nk_ref[...] = kc[:, tdec:, :]
    nv_ref[...] = vc[:, tdec:, :]
    kcb = kc.astype(BF16)
    vcb = vc.astype(BF16)

    rows = A_GROUP * tdec
    keys = WINDOW + tdec
    r = lax.broadcasted_iota(jnp.int32, (1, rows, keys), 1)
    j = lax.broadcasted_iota(jnp.int32, (1, rows, keys), 2)
    rel = WINDOW + (r & (tdec - 1)) - j
    ok = (rel >= 0) & (rel < WINDOW)
    rcol = lax.broadcasted_iota(jnp.int32, (1, rows, 1), 1)

    outs = []
    for n in range(A_KV_HEADS):
        heads = [A_GROUP * n + g for g in range(A_GROUP)]
        qs = jnp.concatenate(
            [qn[:, HEAD_DIM * h:HEAD_DIM * (h + 1)].reshape(nseq, tdec, HEAD_DIM) for h in heads],
            axis=1).astype(BF16)
        kn = kcb[:, :, HEAD_DIM * n:HEAD_DIM * (n + 1)]
        vn = vcb[:, :, HEAD_DIM * n:HEAD_DIM * (n + 1)]
        s = jnp.einsum("bqd,bkd->bqk", qs, kn, preferred_element_type=F32) * (HEAD_DIM ** -0.5)
        sink_col = jnp.full((1, rows, 1), sink_ref[layer, heads[-1]], F32)
        for g in reversed(range(A_GROUP - 1)):
            sink_col = jnp.where(rcol < (g + 1) * tdec, sink_ref[layer, heads[g]], sink_col)
        sm = jnp.where(ok, s, -1e30)
        m = jnp.maximum(jnp.max(sm, axis=-1, keepdims=True), sink_col)
        p = jnp.where(ok, jnp.exp(sm - m), 0.0)
        den = jnp.sum(p, axis=-1, keepdims=True) + jnp.exp(sink_col - m)
        o = jnp.einsum("bqk,bkd->bqd", p.astype(BF16), vn, preferred_element_type=F32) / den
        outs += [o[:, g * tdec:(g + 1) * tdec, :].reshape(nseq * tdec, HEAD_DIM)
                 for g in range(A_GROUP)]
    o_all = jnp.concatenate(outs, axis=1)
    o_ref[...] = (o_all * _silu(ga_ref[...])).astype(BF16)


def _attn_sample(proj, row0, oa, cache_k, cache_v, sinks, qg_all, kg_all, l, carriers,
                 nbatch, tdec, nseq=32):
    tm = nseq * tdec
    rb = row0 // tm
    cb = lambda c, w: c // w
    car = [oa] + list(carriers)
    cache_spec = pl.BlockSpec((None, nseq, WINDOW, KV_WIDTH), lambda s: (l, s, 0, 0))
    return pl.pallas_call(
        _skip(len(car), functools.partial(_attn_sample_kernel, nseq=nseq, tdec=tdec, layer=l)),
        grid=(nbatch // nseq,),
        in_specs=[_CARRIER] * len(car)
        + [pl.BlockSpec(memory_space=pltpu.SMEM),
           pl.BlockSpec((tm, A_WIDTH), lambda s: (rb + s, cb(COL_Q, A_WIDTH))),
           pl.BlockSpec((tm, A_WIDTH), lambda s: (rb + s, cb(COL_GA, A_WIDTH))),
           pl.BlockSpec((tm, KV_WIDTH), lambda s: (rb + s, cb(COL_K, KV_WIDTH))),
           pl.BlockSpec((tm, KV_WIDTH), lambda s: (rb + s, cb(COL_V, KV_WIDTH))),
           cache_spec, cache_spec,
           pl.BlockSpec((None, 1, A_WIDTH), lambda s: (l, 0, 0)),
           pl.BlockSpec((None, 1, KV_WIDTH), lambda s: (l, 0, 0))],
        out_specs=[pl.BlockSpec((tm, A_WIDTH), lambda s: (rb + s, 0)), cache_spec, cache_spec],
        out_shape=[jax.ShapeDtypeStruct(oa.shape, BF16),
                   jax.ShapeDtypeStruct((DEPTH, nbatch, WINDOW, KV_WIDTH), F32),
                   jax.ShapeDtypeStruct((DEPTH, nbatch, WINDOW, KV_WIDTH), F32)],
        input_output_aliases={c: c for c in range(len(car))},
        compiler_params=_params(1),
        name="attn_sample",
    )(*car, sinks, proj, proj, proj, proj, cache_k, cache_v, qg_all, kg_all)


def _stack_heads(x):
    lo = jnp.where(_lane_half_mask(), 1.0, 0.0).astype(x.dtype)
    return jnp.concatenate([x * lo, x * (1 - lo)], axis=0)


def _cat_rows(parts, rows):
    if (rows.stop - rows.start) % 16 == 0:
        return jnp.concatenate([x[rows] for x in parts], axis=0)
    return jnp.concatenate([x.astype(F32)[rows] for x in parts], axis=0).astype(BF16)


def _split3(x):
    hi = x.astype(BF16)
    r1 = x - hi.astype(F32)
    mid = r1.astype(BF16)
    lo = (r1 - mid.astype(F32)).astype(BF16)
    return hi, mid, lo


def _rwkv_masks(blk):
    lb = blk.bit_length() - 1
    t = lax.broadcasted_iota(jnp.int32, (TILE, LANES), 0)
    s = lax.broadcasted_iota(jnp.int32, (TILE, LANES), 1) & (TILE - 1)
    same = (t >> lb) == (s >> lb)
    levels = [((t >> (l + 1)) == (s >> (l + 1))) & (((t >> l) & 1) == 1) & (((s >> l) & 1) == 0)
              for l in range(lb)]
    tt = lax.broadcasted_iota(jnp.int32, (2 * TILE, 3 * TILE), 0)
    ss = lax.broadcasted_iota(jnp.int32, (2 * TILE, 3 * TILE), 1)
    ti = tt & (TILE - 1)
    si = jnp.where(ss >= 2 * TILE, ss - 2 * TILE, jnp.where(ss >= TILE, ss - TILE, ss))
    sel = ((ti >> lb) == (si >> lb)) & ((tt >= TILE) | (si <= ti))
    r2 = lax.broadcasted_iota(jnp.int32, (LANES, LANES), 0)
    c2 = lax.broadcasted_iota(jnp.int32, (LANES, LANES), 1)
    return dict(lb=lb, strict=same & (s < t), incl=same & (s <= t),
                eye=jnp.where(s == t, 1.0, 0.0), levels=levels,
                cumsel=jnp.where(sel, 1.0, 0.0).astype(BF16),
                same_head=(r2 >> 6) == (c2 >> 6))


def _rwkv_tile(xr, xk, xv, xw, gate, prm, states, blk):
    nchunk = xr.shape[0] // TILE
    nseq = TILE // blk
    mk = _rwkv_masks(blk)
    lw = _dot(jnp.tanh(xw).astype(BF16), prm["w2"])
    la = _dot(xw.astype(BF16), prm["a2"])
    ld = (-math.exp(-0.5) * math.log2(math.e)) * _sigmoid(prm["w0"] + lw)
    a = _sigmoid(prm["a0"] + la)
    kkr = xk * prm["k_k"]
    kk = kkr * lax.rsqrt(jnp.maximum(_head_sums_wide(kkr * kkr), 1e-24))
    kmod = xk * (1.0 + (a - 1.0) * prm["k_a"])
    ka = kk * a

    probs = []
    for ch in range(nchunk):
        rows = slice(ch * TILE, (ch + 1) * TILE)
        cum = _dot(mk["cumsel"], jnp.concatenate(_split3(ld[rows]), axis=0))
        c, cl = cum[:TILE], cum[TILE:]
        e_c = jnp.exp2(-c)
        e_l = jnp.exp2(cl - c)
        full = dict(rt=(xr[rows] * jnp.exp2(c)).astype(BF16),
                    at=(-kk[rows] * jnp.exp2(c - ld[rows])).astype(BF16),
                    bt=(ka[rows] * e_c).astype(BF16), kt=(kmod[rows] * e_c).astype(BF16),
                    bh=(ka[rows] * e_l).astype(BF16), kh=(kmod[rows] * e_l).astype(BF16),
                    v=xv[rows].astype(BF16), decay=jnp.exp2(cl))
        for j in range(N_PAIRS):
            sl = slice(j * LANES, (j + 1) * LANES)
            probs.append(dict(ch=ch, j=j, **{n: x[:, sl] for n, x in full.items()}))

    for p in probs:
        p["v_s"] = _stack_heads(p["v"])
        z_s = jnp.concatenate([_stack_heads(p["bt"]), _stack_heads(p["kt"])], axis=0)
        gram = _dot_nt(jnp.concatenate([p["at"], p["rt"]], axis=0), z_s)
        p["a_ab"] = jnp.where(mk["strict"], gram[:TILE, :LANES], 0.0)
        p["a_ak"] = jnp.where(mk["strict"], gram[:TILE, LANES:], 0.0).astype(BF16)
        p["a_r"] = jnp.concatenate([jnp.where(mk["incl"], gram[TILE:, :LANES], 0.0),
                                    jnp.where(mk["incl"], gram[TILE:, LANES:], 0.0)],
                                   axis=1).astype(BF16)
        p["tinv"] = mk["eye"]
        if mk["lb"] > 0:
            p["tinv"] = p["tinv"] + jnp.where(mk["levels"][0], p["a_ab"], 0.0)

    for l in range(1, mk["lb"]):
        for p in probs:
            p["tb"] = p["tinv"].astype(BF16)
            mid = _stack_heads(jnp.where(mk["levels"][l], p["a_ab"], 0.0).astype(BF16))
            p["half"] = _dot(p["tb"], mid).astype(BF16)
        for p in probs:
            p["tinv"] = p["tinv"] + _dot(p["half"], _stack_heads(p["tb"]))

    for p in probs:
        p["akv"] = _dot(p["a_ak"], p["v_s"]).astype(BF16)
    for p in probs:
        p["wu"] = _dot(p["tinv"].astype(BF16),
                       jnp.concatenate([_stack_heads(p["at"]), _stack_heads(p["akv"])], axis=1))
        p["w"], p["u0"] = p["wu"][:, :LANES].astype(BF16), p["wu"][:, LANES:]

    carried = nseq == 1
    if carried:
        for p in probs:
            p["wtb"] = _dot_tn(p["wu"].astype(BF16), p["bh"])
        for p in probs:
            vtk = _dot_tn(p["v"], p["kh"])
            p["m"] = jnp.where(mk["same_head"], p["wtb"][:LANES], 0.0).astype(BF16)
            p["c"] = jnp.where(mk["same_head"], p["wtb"][LANES:] + vtk, 0.0)

    def read_state(p, sts):
        u_parts, rs_parts = [], []
        for q in range(nseq):
            rows = slice(q * blk, (q + 1) * blk)
            res = _dot_nt(_cat_rows([p["w"], p["rt"]], rows), sts[q].astype(BF16))
            u_parts.append(res[:blk] + p["u0"][rows])
            rs_parts.append(res[blk:])
        p["u"] = (jnp.concatenate(u_parts, axis=0) if nseq > 1 else u_parts[0]).astype(BF16)
        p["rs"] = jnp.concatenate(rs_parts, axis=0) if nseq > 1 else rs_parts[0]

    states = [list(st) for st in states]
    y_rows = []
    for ch in range(nchunk):
        cps = [p for p in probs if p["ch"] == ch]
        first = 0 if carried else ch * nseq
        start = [states[p["j"]][first:first + nseq] for p in cps]
        if carried:
            for p in cps:
                s = states[p["j"]][0]
                states[p["j"]][0] = s * p["decay"][0:1] + _dot(s.astype(BF16), p["m"]) + p["c"]
        for p, sts in zip(cps, start):
            read_state(p, sts)
        if not carried:
            for p in cps:
                for q in range(nseq):
                    rows = slice(q * blk, (q + 1) * blk)
                    upd = _dot_tn(_cat_rows([p["u"], p["v"]], rows),
                                  _cat_rows([p["bh"], p["kh"]], rows))
                    states[p["j"]][first + q] = (
                        states[p["j"]][first + q] * p["decay"][q * blk:q * blk + 1]
                        + jnp.where(mk["same_head"], upd, 0.0))
        y_rows.append(jnp.concatenate(
            [_dot(p["a_r"], jnp.concatenate([_stack_heads(p["u"]), p["v_s"]], axis=0)) + p["rs"]
             for p in cps], axis=1))
    y = jnp.concatenate(y_rows, axis=0) if nchunk > 1 else y_rows[0]

    mean = _head_sums_wide(y) * (1.0 / HEAD_DIM)
    yc = y - mean
    var = _head_sums_wide(yc * yc) * (1.0 / HEAD_DIM)
    yn = yc * lax.rsqrt(var + GN_EPS) * prm["gn_w"] + prm["gn_b"]
    yn = yn + _head_sums_wide(xr * kmod * prm["r_k"]) * xv
    return (yn * _silu(gate)).astype(BF16), states


_RWKV_PARAM_NAMES = ("mu", "w0", "w2", "a0", "a2", "k_k", "k_a", "r_k", "gn_w", "gn_b")


def _rwkv_shift_mix(cur, prev_rows, first_row_mask, mu):
    shifted = jnp.where(first_row_mask, prev_rows, pltpu.roll(cur, 1, axis=0))
    return cur + (shifted - cur) * mu


def _rwkv_load_params(refs):
    prm = {n: r[...] for n, r in zip(_RWKV_PARAM_NAMES, refs)}
    mu = prm.pop("mu")
    return prm, (mu[:, 0:B_WIDTH], mu[:, B_WIDTH:2 * B_WIDTH],
                 mu[:, 2 * B_WIDTH:3 * B_WIDTH], mu[:, 3 * B_WIDTH:])


def _rwkv_prompt_kernel(r_ref, k_ref, v_ref, g_ref, w_ref, *rest):
    prm_refs, (o_ref, s_out_ref, sh_out_ref, state, prev) = (rest[:len(_RWKV_PARAM_NAMES)],
                                                               rest[len(_RWKV_PARAM_NAMES):])
    i = pl.program_id(1)

    @pl.when(i == 0)
    def _():
        state[...] = jnp.zeros_like(state)
        prev[...] = jnp.zeros_like(prev)

    prm, mus = _rwkv_load_params(prm_refs)
    cur = (r_ref[...], k_ref[...], v_ref[...], w_ref[...])
    rows = cur[0].shape[0]
    first = lax.broadcasted_iota(jnp.int32, (8, 1), 0) == 0
    offs = (0, B_WIDTH, 2 * B_WIDTH, 3 * B_WIDTH, SHIFT_PAD)
    mixed = []
    for n, x in enumerate(cur):
        m = x + (pltpu.roll(x, 1, axis=0) - x) * mus[n]
        top = x[0:8]
        top = jnp.where(first, top + (prev[0:1, offs[n]:offs[n + 1]] - top) * mus[n], m[0:8])
        mixed.append(jnp.concatenate([top, m[8:]], axis=0))
        prev[0:1, offs[n]:offs[n + 1]] = x[rows - 1:rows, :]
    states = [[state[j]] for j in range(N_PAIRS)]
    out, new_states = _rwkv_tile(*mixed, g_ref[...], prm, states, TILE)
    o_ref[...] = out
    for j in range(N_PAIRS):
        s = new_states[j][0]
        state[j] = s
        s_out_ref[2 * j] = s[:HEAD_DIM, :HEAD_DIM]
        s_out_ref[2 * j + 1] = s[HEAD_DIM:, HEAD_DIM:]
    sh_out_ref[...] = prev[...]


def _rwkv_sample_kernel(r_ref, k_ref, v_ref, g_ref, w_ref, sh_ref, s_in_ref, *rest, tdec):
    prm_refs, (o_ref, s_out_ref, sh_out_ref) = (rest[:len(_RWKV_PARAM_NAMES)],
                                                rest[len(_RWKV_PARAM_NAMES):])
    rows = r_ref.shape[0]
    nseq = rows // tdec
    prm, mus = _rwkv_load_params(prm_refs)
    cur = (r_ref[...], k_ref[...], v_ref[...], w_ref[...])
    first = (lax.broadcasted_iota(jnp.int32, (rows, 1), 0) & (tdec - 1)) == 0
    offs = (0, B_WIDTH, 2 * B_WIDTH, 3 * B_WIDTH, SHIFT_PAD)
    sh = sh_ref[...]
    mixed = []
    for n, x in enumerate(cur):
        prev_rows = jnp.concatenate(
            [jnp.broadcast_to(sh[q:q + 1, offs[n]:offs[n + 1]], (tdec, x.shape[1]))
             for q in range(nseq)], axis=0)
        mixed.append(_rwkv_shift_mix(x, prev_rows, first, mus[n]))
        for q in range(nseq):
            last = (q + 1) * tdec - 1
            sh_out_ref[q:q + 1, offs[n]:offs[n + 1]] = x[last:last + 1, :]
    zero = jnp.zeros((HEAD_DIM, HEAD_DIM), F32)
    states = []
    for j in range(N_PAIRS):
        pair = []
        for q in range(nseq):
            top = jnp.concatenate([s_in_ref[q, 2 * j], zero], axis=1)
            bot = jnp.concatenate([zero, s_in_ref[q, 2 * j + 1]], axis=1)
            pair.append(jnp.concatenate([top, bot], axis=0))
        states.append(pair)
    out, new_states = _rwkv_tile(*mixed, g_ref[...], prm, states, tdec)
    o_ref[...] = out
    for j in range(N_PAIRS):
        for q in range(nseq):
            s = new_states[j][q]
            s_out_ref[q, 2 * j] = s[:HEAD_DIM, :HEAD_DIM]
            s_out_ref[q, 2 * j + 1] = s[HEAD_DIM:, HEAD_DIM:]


def _rwkv_param_specs(l):
    shapes = {"mu": (None, 1, SHIFT_PAD), "w2": (None, LANES, B_WIDTH), "a2": (None, LANES, B_WIDTH)}
    return [pl.BlockSpec(shapes.get(n, (None, 1, B_WIDTH)), lambda *_: (l, 0, 0))
            for n in _RWKV_PARAM_NAMES]


def _rwkv_col_specs(row_fn, rows):
    cols = (COL_R, COL_KB, COL_VB, COL_GB)
    specs = [pl.BlockSpec((rows, B_WIDTH), (lambda *a, c=c: (row_fn(*a), c // B_WIDTH))) for c in cols]
    specs.append(pl.BlockSpec((rows, LANES), lambda *a: (row_fn(*a), COL_WLAL // LANES)))
    return specs


def _rwkv_prompt(proj, prm, l, carriers, batch, seq, rows=RWKV_PROMPT_ROWS):
    nt = seq // rows
    row_fn = lambda b, i: b * nt + i
    ncar = len(carriers)
    return pl.pallas_call(
        _skip(ncar, _rwkv_prompt_kernel),
        grid=(batch, nt),
        in_specs=[_CARRIER] * ncar + _rwkv_col_specs(row_fn, rows) + _rwkv_param_specs(l),
        out_specs=[pl.BlockSpec((rows, B_WIDTH), lambda b, i: (row_fn(b, i), 0)),
                   pl.BlockSpec((None, None, B_HEADS, HEAD_DIM, HEAD_DIM),
                                lambda b, i: (l, b, 0, 0, 0)),
                   pl.BlockSpec((None, None, 8, SHIFT_PAD), lambda b, i: (l, b, 0, 0))],
        out_shape=[jax.ShapeDtypeStruct((proj.shape[0], B_WIDTH), BF16),
                   jax.ShapeDtypeStruct((DEPTH, batch, B_HEADS, HEAD_DIM, HEAD_DIM), F32),
                   jax.ShapeDtypeStruct((DEPTH, batch, 8, SHIFT_PAD), F32)],
        scratch_shapes=[pltpu.VMEM((N_PAIRS, LANES, LANES), F32),
                        pltpu.VMEM((8, SHIFT_PAD), F32)],
        input_output_aliases={c: 1 + c for c in range(ncar)},
        compiler_params=_params(2),
        name="rwkv_prompt",
    )(*carriers, proj, proj, proj, proj, proj, *[prm[n] for n in _RWKV_PARAM_NAMES])


def _rwkv_sample(proj, row0, ob, shift_in, state_in, prm, l, carriers, nbatch, tdec):
    rows = RWKV_SAMPLE_ROWS
    nseq = rows // tdec
    rb = row0 // rows
    row_fn = lambda s: rb + s
    car = [ob] + list(carriers)
    state_spec = pl.BlockSpec((None, nseq, B_HEADS, HEAD_DIM, HEAD_DIM), lambda s: (l, s, 0, 0, 0))
    shift_spec = pl.BlockSpec((None, nseq, SHIFT_PAD), lambda s: (l, s, 0))
    return pl.pallas_call(
        _skip(len(car), functools.partial(_rwkv_sample_kernel, tdec=tdec)),
        grid=(nbatch // nseq,),
        in_specs=[_CARRIER] * len(car) + _rwkv_col_specs(row_fn, rows)
        + [shift_spec, state_spec] + _rwkv_param_specs(l),
        out_specs=[pl.BlockSpec((rows, B_WIDTH), lambda s: (rb + s, 0)), state_spec, shift_spec],
        out_shape=[jax.ShapeDtypeStruct(ob.shape, BF16),
                   jax.ShapeDtypeStruct((DEPTH, nbatch, B_HEADS, HEAD_DIM, HEAD_DIM), F32),
                   jax.ShapeDtypeStruct((DEPTH, nbatch, SHIFT_PAD), F32)],
        input_output_aliases={c: c for c in range(len(car))},
        compiler_params=_params(1),
        name="rwkv_sample",
    )(*car, proj, proj, proj, proj, proj, shift_in, state_in,
      *[prm[n] for n in _RWKV_PARAM_NAMES])


def _window_sums(xe):
    s2 = xe + pltpu.roll(xe, 1, axis=0)
    s4 = s2 + pltpu.roll(s2, 2, axis=0)
    s8 = s4 + pltpu.roll(s4, 4, axis=0)
    s16 = s8 + pltpu.roll(s8, 8, axis=0)
    lane = lax.broadcasted_iota(jnp.int32, (1, C_WIDTH), 1)
    return jnp.where(lane < 64, s2, jnp.where(lane < 128, s4, jnp.where(lane < 192, s8, s16)))


def _pool_window_lane():
    lane = lax.broadcasted_iota(jnp.int32, (1, C_WIDTH), 1)
    return jnp.where(lane < 64, 2, jnp.where(lane < 128, 4, jnp.where(lane < 192, 8, 16)))


def _pool_prompt_rows(u, halo, gate, wbd, scale, it):
    tm = u.shape[0]
    halo = jnp.where(it > 0, halo, 0.0)
    sums = _window_sums(jnp.concatenate([halo, u], axis=0))[16:]
    pos = it * tm + lax.broadcasted_iota(jnp.int32, (tm, 1), 0)
    cnt = jnp.minimum(_pool_window_lane(), pos + 1).astype(F32)
    d = sums / cnt - u
    y = _dot(d.astype(BF16), wbd) * scale
    return (y * _silu(gate)).astype(BF16)


def _pool_sample_kernel(u_ref, h_ref, g_ref, w_ref, sc_ref, o_ref, *, nseq, tdec, pos0):
    u = u_ref[...]
    hist = h_ref[...]
    xe = jnp.concatenate([hist, u.reshape(nseq, tdec, C_WIDTH)], axis=1)
    xe = xe.reshape(nseq * (16 + tdec), C_WIDTH)
    sums = _window_sums(xe).reshape(nseq, 16 + tdec, C_WIDTH)[:, 16:, :]
    sums = sums.reshape(nseq * tdec, C_WIDTH)
    t = lax.broadcasted_iota(jnp.int32, (nseq * tdec, 1), 0) & (tdec - 1)
    cnt = jnp.minimum(_pool_window_lane(), pos0 + t + 1).astype(F32)
    d = sums / cnt - u
    y = _dot(d.astype(BF16), w_ref[...]) * sc_ref[...]
    o_ref[...] = (y * _silu(g_ref[...])).astype(BF16)


def _pool_sample(proj, row0, hist16_all, wbd_all, scale_all, l, nbatch, tdec, pos0, nseq=64):
    tm = nseq * tdec
    rb = row0 // tm
    return pl.pallas_call(
        functools.partial(_pool_sample_kernel, nseq=nseq, tdec=tdec, pos0=pos0),
        grid=(nbatch // nseq,),
        in_specs=[pl.BlockSpec((tm, C_WIDTH), lambda s: (rb + s, COL_UC // C_WIDTH)),
                  pl.BlockSpec((None, nseq, 16, C_WIDTH), lambda s: (l, s, 0, 0)),
                  pl.BlockSpec((tm, C_WIDTH), lambda s: (rb + s, COL_GC // C_WIDTH)),
                  pl.BlockSpec((None, C_WIDTH, C_WIDTH), lambda s: (l, 0, 0)),
                  pl.BlockSpec((None, 1, C_WIDTH), lambda s: (l, 0, 0))],
        out_specs=pl.BlockSpec((tm, C_WIDTH), lambda s: (s, 0)),
        out_shape=jax.ShapeDtypeStruct((nbatch * tdec, C_WIDTH), BF16),
        compiler_params=_params(1),
        name="pool_sample",
    )(proj, hist16_all, proj, wbd_all, scale_all)


def _pad_shift(x):
    pad = jnp.zeros(x.shape[:-1] + (LANES - 2 * LORA,), x.dtype)
    return jnp.concatenate([x, pad], axis=-1)


def _block_diag(w):
    n, g, c, _ = w.shape
    eye = jnp.eye(g, dtype=w.dtype)
    return (eye[None, :, None, :, None] * w[:, :, :, None, :]).reshape(n, g * c, g * c)


def kernel(x_prompt, x_sample, cache_k, cache_v, state_wkv, state_shift, state_pool, norm_g, w_in, q_norm_g, k_norm_g, attn_sinks, shift_mu, decay_w0, decay_w2, iclr_a0, iclr_a2, k_k, k_a, r_k, gn_w, gn_b, pool_w, pool_scale, w_out):
    batch, seq, _ = x_prompt.shape
    nbatch, tdec, _ = x_sample.shape
    wbuf = cache_k.shape[2]
    n_prompt, n_sample = batch * seq, nbatch * tdec
    n_total = n_prompt + n_sample

    row = lambda a: a.reshape(DEPTH, 1, -1)
    lora_pad = jnp.zeros((DEPTH, LANES - LORA, B_WIDTH), F32)
    prm = {
        "mu": row(_pad_shift(shift_mu)), "w0": row(decay_w0), "a0": row(iclr_a0),
        "w2": jnp.concatenate([decay_w2, lora_pad], axis=1).astype(BF16),
        "a2": jnp.concatenate([lora_pad[:, :LORA], iclr_a2, lora_pad[:, LORA:]], axis=1).astype(BF16),
        "k_k": row(k_k), "k_a": row(k_a), "r_k": row(r_k), "gn_w": row(gn_w), "gn_b": row(gn_b),
    }
    g_all = row(norm_g)
    qg_all = row(jnp.tile(q_norm_g, (1, A_HEADS)))
    kg_all = row(jnp.tile(k_norm_g, (1, A_KV_HEADS)))
    w_in_all = jnp.swapaxes(w_in, 1, 2)
    w_out_all = w_out.astype(BF16)
    wbd_all = _block_diag(pool_w).astype(BF16)
    scale_all = row(pool_scale)
    ck_all = cache_k.reshape(DEPTH, nbatch, wbuf, KV_WIDTH)
    cv_all = cache_v.reshape(DEPTH, nbatch, wbuf, KV_WIDTH)
    shift_all = _pad_shift(state_shift)
    hist16_all = jnp.pad(state_pool, ((0, 0), (0, 0), (1, 0), (0, 0)))

    x_p = x_prompt.reshape(n_prompt, D_MODEL)
    x_s = x_sample.reshape(n_sample, D_MODEL)
    kv_p, kv_s, wkv_p, wkv_s = [], [], [], []
    pool_p, pool_s = [], []
    proj = _inproj(x_p, g_all, w_in_all, 0, n_total, 0)
    for l in range(DEPTH):
        proj = _inproj(x_s, g_all, w_in_all, l, n_total, n_prompt, carrier=proj)

        oa, *kv_p = _attn_prompt(proj, attn_sinks, qg_all, kg_all, l, kv_p, batch, seq)
        oa, *kv_s = _attn_sample(proj, n_prompt, oa, ck_all, cv_all, attn_sinks, qg_all, kg_all,
                                 l, kv_s, nbatch, tdec)
        ob, *wkv_p = _rwkv_prompt(proj, prm, l, wkv_p, batch, seq)
        ob, *wkv_s = _rwkv_sample(proj, n_prompt, ob, shift_all, state_wkv, prm, l, wkv_s,
                                  nbatch, tdec)
        oc_s = _pool_sample(proj, n_prompt, hist16_all, wbd_all, scale_all, l, nbatch, tdec,
                            PAST_LEN)
        if l + 1 < DEPTH:
            x_p, proj_next = _outproj_prompt(x_p, oa, ob, proj, wbd_all, scale_all, w_out_all, l,
                                             seq, next_in=(g_all, w_in_all, n_total))
        else:
            x_p = _outproj_prompt(x_p, oa, ob, proj, wbd_all, scale_all, w_out_all, l, seq)
        x_s = _outproj_sample(x_s, oa, ob, oc_s, w_out_all, l, n_prompt)

        pool_p.append(jnp.stack([proj[(b + 1) * seq - POOL_HIST:(b + 1) * seq,
                                      COL_UC:COL_UC + C_WIDTH] for b in range(batch)]))
        u_s = proj[n_prompt:, COL_UC:COL_UC + C_WIDTH].reshape(nbatch, tdec, C_WIDTH)
        pool_s.append(jnp.concatenate([state_pool[l], u_s], axis=1)[:, -POOL_HIST:])
        if l + 1 < DEPTH:
            proj = proj_next

    heads = lambda a: a.reshape(a.shape[:-1] + (A_KV_HEADS, HEAD_DIM))
    return (x_p.reshape(batch, seq, D_MODEL), x_s.reshape(nbatch, tdec, D_MODEL),
            heads(kv_p[0]), heads(kv_p[1]), wkv_p[0], wkv_p[1][:, :, 0, :SHIFT_WIDTH],
            jnp.stack(pool_p),
            heads(kv_s[0]), heads(kv_s[1]), wkv_s[0], wkv_s[1][:, :, :SHIFT_WIDTH],
            jnp.stack(pool_s))
```

```python
import functools
import math

import jax
import jax.numpy as jnp
from jax import lax
from jax.experimental import pallas as pl
from jax.experimental.pallas import tpu as pltpu

F32 = jnp.float32
BF16 = jnp.bfloat16

D_MODEL = 1024
DEPTH = 4
HEAD_DIM = 64
A_HEADS = 6
A_KV_HEADS = 2
A_GROUP = A_HEADS // A_KV_HEADS
A_WIDTH = A_HEADS * HEAD_DIM
KV_WIDTH = A_KV_HEADS * HEAD_DIM
WINDOW = 128
B_HEADS = 6
B_WIDTH = B_HEADS * HEAD_DIM
LORA = 32
GN_EPS = 6.4e-4
SHIFT_WIDTH = 3 * B_WIDTH + 2 * LORA
C_WIDTH = 256
POOL_HIST = 15
NORM_EPS = 1e-6
PAST_LEN = 8192

LANES = 128
TILE = 64
RWKV_PROMPT_ROWS = 512
ATTN_PROMPT_ROWS = 1024
RWKV_SAMPLE_ROWS = 256
N_PAIRS = B_HEADS // 2

COL_Q, COL_GA, COL_R, COL_KB, COL_VB, COL_GB = 0, 384, 768, 1152, 1536, 1920
COL_K, COL_V, COL_UC, COL_GC, COL_WLAL = 2304, 2432, 2560, 2816, 3072
IN_PAD = 3200
SHIFT_PAD = 3 * B_WIDTH + LANES

VMEM_LIMIT = 48 * 1024 * 1024
VMEM_LIMIT_FUSED = 58 * 1024 * 1024


def _dot(a, b, prec=None):
    return jnp.dot(a, b, preferred_element_type=F32, precision=prec)


def _dot_nt(a, b, prec=None):
    return lax.dot_general(a, b, (((1,), (1,)), ((), ())),
                           preferred_element_type=F32, precision=prec)


def _dot_tn(a, b, prec=None):
    return lax.dot_general(a, b, (((0,), (0,)), ((), ())),
                           preferred_element_type=F32, precision=prec)


def _sigmoid(x):
    return 1.0 / (1.0 + jnp.exp(-x))


def _silu(x):
    return x * _sigmoid(x)


def _lane_half_mask(rows=1):
    lane = lax.broadcasted_iota(jnp.int32, (rows, LANES), 1)
    return lane < HEAD_DIM


def _head_sums(x):
    lo = _lane_half_mask()
    s0 = jnp.sum(jnp.where(lo, x, 0.0), axis=-1, keepdims=True)
    s1 = jnp.sum(jnp.where(lo, 0.0, x), axis=-1, keepdims=True)
    return jnp.where(lo, s0, s1)


def _head_sums_wide(x):
    n = x.shape[1] // LANES
    return jnp.concatenate(
        [_head_sums(x[:, j * LANES:(j + 1) * LANES]) for j in range(n)], axis=1)


def _head_rms(x, g):
    ms = _head_sums_wide(x * x) * (1.0 / HEAD_DIM)
    return x * lax.rsqrt(ms + NORM_EPS) * g


_W_SEGMENTS = ((0, 384, COL_Q), (384, 128, COL_K), (512, 128, COL_V), (640, 384, COL_GA),
               (1024, 3 * B_WIDTH, COL_R), (2176, 2 * LORA, COL_WLAL), (2240, 384, COL_GB),
               (2624, 256, COL_UC), (2880, 256, COL_GC))
IN_WIDTH = 3136


def _regroup_w_in(wt_ref, wb):
    @pl.when(pl.program_id(0) == 0)
    def _():
        for src, n, dst in _W_SEGMENTS:
            wb[dst:dst + n, :] = wt_ref[src:src + n, :].astype(BF16)
        pad0 = COL_WLAL + 2 * LORA
        wb[pad0:IN_PAD, :] = jnp.zeros((IN_PAD - pad0, D_MODEL), BF16)


def _rms_rows(x, g):
    ms = jnp.mean(x * x, axis=-1, keepdims=True)
    return (x * lax.rsqrt(ms + NORM_EPS) * g).astype(BF16)


def _inproj_kernel(x_ref, g_ref, wt_ref, o_ref, wb):
    _regroup_w_in(wt_ref, wb)
    o_ref[...] = _dot_nt(_rms_rows(x_ref[...], g_ref[...]), wb[...])


def _skip(n, fn):
    def wrapped(*refs, **kw):
        return fn(*refs[n:], **kw)
    return wrapped


_CARRIER = pl.BlockSpec(memory_space=pl.ANY)


def _params(ndims):
    return pltpu.CompilerParams(dimension_semantics=("arbitrary",) * ndims,
                                vmem_limit_bytes=VMEM_LIMIT)


def _inproj(x, g_all, wt_all, l, n_total, row0, carrier=None, tm=512):
    n = x.shape[0]
    rb = row0 // tm
    car = [] if carrier is None else [carrier]
    return pl.pallas_call(
        _skip(len(car), _inproj_kernel),
        grid=(n // tm,),
        in_specs=[_CARRIER] * len(car)
        + [pl.BlockSpec((tm, D_MODEL), lambda i: (i, 0)),
           pl.BlockSpec((None, 1, D_MODEL), lambda i: (l, 0, 0)),
           pl.BlockSpec((None, IN_WIDTH, D_MODEL), lambda i: (l, 0, 0),
                        pipeline_mode=pl.Buffered(1))],
        out_specs=pl.BlockSpec((tm, IN_PAD), lambda i: (rb + i, 0)),
        out_shape=jax.ShapeDtypeStruct((n_total, IN_PAD), F32),
        scratch_shapes=[pltpu.VMEM((IN_PAD, D_MODEL), BF16)],
        input_output_aliases={0: 0} if car else {},
        compiler_params=_params(1),
        name="inproj",
    )(*car, x, g_all, wt_all)


def _outproj_kernel(x_ref, oa_ref, ob_ref, oc_ref, wa_ref, wb_ref, wc_ref, o_ref):
    acc = _dot(oa_ref[...], wa_ref[...])
    acc += _dot(ob_ref[...], wb_ref[...])
    acc += _dot(oc_ref[...], wc_ref[...])
    o_ref[...] = x_ref[...] + acc


def _outproj_pool_rows(x_ref, oa_ref, ob_ref, u_ref, halo_ref, g_ref, wp_ref, sc_ref,
                       wa_ref, wb_ref, wc_ref, tile, tiles_per_seq):
    it = lax.rem(tile, tiles_per_seq)
    oc = _pool_prompt_rows(u_ref[...], halo_ref[...], g_ref[...], wp_ref[...], sc_ref[...], it)
    acc = _dot(oa_ref[...], wa_ref[...])
    acc += _dot(ob_ref[...], wb_ref[...])
    acc += _dot(oc, wc_ref[...])
    return x_ref[...] + acc


def _outproj_pool_kernel(*refs, tiles_per_seq):
    *ins, o_ref = refs
    o_ref[...] = _outproj_pool_rows(*ins, pl.program_id(0), tiles_per_seq)


def _outproj_inproj_kernel(*refs, tiles_per_seq, ntiles):
    *ins, xs_ref, ng_ref, wt_ref, xo_ref, po_ref, wb = refs
    i = pl.program_id(0)
    _regroup_w_in(wt_ref, wb)

    @pl.when(i < ntiles)
    def _():
        x_new = _outproj_pool_rows(*ins, i, tiles_per_seq)
        xo_ref[...] = x_new
        po_ref[...] = _dot_nt(_rms_rows(x_new, ng_ref[...]), wb[...])

    @pl.when(i >= ntiles)
    def _():
        po_ref[...] = _dot_nt(_rms_rows(xs_ref[...], ng_ref[...]), wb[...])


def _w_out_specs(l):
    return [pl.BlockSpec((None, A_WIDTH, D_MODEL), lambda i: (l, 0, 0)),
            pl.BlockSpec((None, B_WIDTH, D_MODEL), lambda i: (l, 1, 0)),
            pl.BlockSpec((None, C_WIDTH, D_MODEL), lambda i: (l, 3, 0))]


def _outproj_prompt(x, oa, ob, proj, wp_all, sc_all, w_all, l, seq, next_in=None, tm=512):
    n = x.shape[0]
    ntiles = n // tm
    tile = lambda i: jnp.minimum(i, ntiles - 1)
    halo_row = lambda i: jnp.maximum(tile(i) * (tm // 16) - 1, 0)
    in_specs = [pl.BlockSpec((tm, D_MODEL), lambda i: (tile(i), 0)),
                pl.BlockSpec((tm, A_WIDTH), lambda i: (tile(i), 0)),
                pl.BlockSpec((tm, B_WIDTH), lambda i: (tile(i), 0)),
                pl.BlockSpec((tm, C_WIDTH), lambda i: (tile(i), COL_UC // C_WIDTH)),
                pl.BlockSpec((16, C_WIDTH), lambda i: (halo_row(i), COL_UC // C_WIDTH)),
                pl.BlockSpec((tm, C_WIDTH), lambda i: (tile(i), COL_GC // C_WIDTH)),
                pl.BlockSpec((None, C_WIDTH, C_WIDTH), lambda i: (l, 0, 0)),
                pl.BlockSpec((None, 1, C_WIDTH), lambda i: (l, 0, 0))] + _w_out_specs(l)
    args = [x, oa, ob, proj, proj, proj, wp_all, sc_all, w_all, w_all, w_all]
    x_spec = pl.BlockSpec((tm, D_MODEL), lambda i: (tile(i), 0))
    x_shape = jax.ShapeDtypeStruct((n, D_MODEL), F32)
    if next_in is None:
        return pl.pallas_call(
            functools.partial(_outproj_pool_kernel, tiles_per_seq=seq // tm),
            grid=(ntiles,), in_specs=in_specs, out_specs=x_spec, out_shape=x_shape,
            compiler_params=_params(1), name="outproj_prompt")(*args)
    g_all, wt_all, x_dec = next_in
    dtiles = x_dec.shape[0] // tm
    return pl.pallas_call(
        functools.partial(_outproj_inproj_kernel, tiles_per_seq=seq // tm, ntiles=ntiles),
        grid=(ntiles + dtiles,),
        in_specs=in_specs
        + [pl.BlockSpec((tm, D_MODEL), lambda i: (jnp.maximum(i - ntiles, 0), 0)),
           pl.BlockSpec((None, 1, D_MODEL), lambda i: (l + 1, 0, 0)),
           pl.BlockSpec((None, IN_WIDTH, D_MODEL), lambda i: (l + 1, 0, 0),
                        pipeline_mode=pl.Buffered(1))],
        out_specs=[x_spec, pl.BlockSpec((tm, IN_PAD), lambda i: (i, 0))],
        out_shape=[x_shape, jax.ShapeDtypeStruct((n + x_dec.shape[0], IN_PAD), F32)],
        scratch_shapes=[pltpu.VMEM((IN_PAD, D_MODEL), BF16)],
        compiler_params=pltpu.CompilerParams(dimension_semantics=("arbitrary",),
                                             vmem_limit_bytes=VMEM_LIMIT_FUSED),
        name="outproj_inproj",
    )(*args, x_dec, g_all, wt_all)


def _outproj_sample(x, oa, ob, oc, w_all, l, row0, tm=512):
    n = x.shape[0]
    rb = row0 // tm
    return pl.pallas_call(
        _outproj_kernel,
        grid=(n // tm,),
        in_specs=[pl.BlockSpec((tm, D_MODEL), lambda i: (i, 0)),
                  pl.BlockSpec((tm, A_WIDTH), lambda i: (rb + i, 0)),
                  pl.BlockSpec((tm, B_WIDTH), lambda i: (rb + i, 0)),
                  pl.BlockSpec((tm, C_WIDTH), lambda i: (i, 0))] + _w_out_specs(l),
        out_specs=pl.BlockSpec((tm, D_MODEL), lambda i: (i, 0)),
        out_shape=jax.ShapeDtypeStruct((n, D_MODEL), F32),
        compiler_params=_params(1),
        name="outproj_sample",
    )(x, oa, ob, oc, w_all, w_all, w_all)


def _attn_prompt_kernel(sink_ref, q_ref, ga_ref, k_ref, v_ref, qg_ref, kg_ref,
                        o_ref, kl_ref, vl_ref, kprev, vprev, *, layer):
    i = pl.program_id(1)

    @pl.when(i == 0)
    def _():
        kprev[...] = jnp.zeros_like(kprev)
        vprev[...] = jnp.zeros_like(vprev)

    nblk = q_ref.shape[0] // WINDOW
    k = _head_rms(k_ref[...], kg_ref[...])
    v = v_ref[...]
    qn = (_head_rms(q_ref[...], qg_ref[...]) * (HEAD_DIM ** -0.5)).astype(BF16)
    kc = jnp.concatenate([kprev[...], k], axis=0).astype(BF16)
    vc = jnp.concatenate([vprev[...], v], axis=0).astype(BF16)

    rows = A_GROUP * WINDOW
    r = lax.broadcasted_iota(jnp.int32, (rows, 2 * WINDOW), 0)
    j = lax.broadcasted_iota(jnp.int32, (rows, 2 * WINDOW), 1)
    rel = (r & (WINDOW - 1)) + WINDOW - j
    band = (rel >= 0) & (rel < WINDOW)
    band_first = band & ((j >= WINDOW) | (i > 0))
    rcol = lax.broadcasted_iota(jnp.int32, (rows, 1), 0)

    probs = [dict(b=b, n=n) for b in range(nblk) for n in range(A_KV_HEADS)]
    for p in probs:
        b, n = p["b"], p["n"]
        heads = [A_GROUP * n + g for g in range(A_GROUP)]
        qs = jnp.concatenate([qn[b * WINDOW:(b + 1) * WINDOW, HEAD_DIM * h:HEAD_DIM * (h + 1)]
                              for h in heads], axis=0)
        keys = slice(b * WINDOW, (b + 2) * WINDOW)
        p["s"] = _dot_nt(qs, kc[keys, HEAD_DIM * n:HEAD_DIM * (n + 1)])
        p["v"] = vc[keys, HEAD_DIM * n:HEAD_DIM * (n + 1)]
        sink_col = jnp.full((rows, 1), sink_ref[layer, heads[-1]], F32)
        for g in reversed(range(A_GROUP - 1)):
            sink_col = jnp.where(rcol < (g + 1) * WINDOW, sink_ref[layer, heads[g]], sink_col)
        p["sink"] = sink_col
    for p in probs:
        sm = jnp.where(band_first if p["b"] == 0 else band, p["s"], -1e30)
        m = jnp.maximum(jnp.max(sm, axis=-1, keepdims=True), p["sink"])
        e = jnp.exp(sm - m)
        p["den"] = jnp.sum(e, axis=-1, keepdims=True) + jnp.exp(p["sink"] - m)
        p["e"] = e.astype(BF16)
    for p in probs:
        p["o"] = _dot(p["e"], p["v"]) / p["den"]
    o_rows = []
    for b in range(nblk):
        outs = []
        for p in probs[b * A_KV_HEADS:(b + 1) * A_KV_HEADS]:
            outs += [p["o"][g * WINDOW:(g + 1) * WINDOW] for g in range(A_GROUP)]
        o_rows.append(jnp.concatenate(outs, axis=1))
    o_all = jnp.concatenate(o_rows, axis=0) if nblk > 1 else o_rows[0]
    o_ref[...] = (o_all * _silu(ga_ref[...])).astype(BF16)

    last = slice((nblk - 1) * WINDOW, nblk * WINDOW)
    kprev[...] = k[last]
    vprev[...] = v[last]
    kl_ref[...] = k[last]
    vl_ref[...] = v[last]


def _attn_prompt(proj, sinks, qg_all, kg_all, l, carriers, batch, seq):
    tm = ATTN_PROMPT_ROWS
    nb = seq // tm
    n_total = proj.shape[0]
    cb = lambda c, w: c // w
    row = lambda b, i: b * nb + i
    ncar = len(carriers)
    return pl.pallas_call(
        _skip(ncar, functools.partial(_attn_prompt_kernel, layer=l)),
        grid=(batch, nb),
        in_specs=[_CARRIER] * ncar
        + [pl.BlockSpec(memory_space=pltpu.SMEM),
           pl.BlockSpec((tm, A_WIDTH), lambda b, i: (row(b, i), cb(COL_Q, A_WIDTH))),
           pl.BlockSpec((tm, A_WIDTH), lambda b, i: (row(b, i), cb(COL_GA, A_WIDTH))),
           pl.BlockSpec((tm, KV_WIDTH), lambda b, i: (row(b, i), cb(COL_K, KV_WIDTH))),
           pl.BlockSpec((tm, KV_WIDTH), lambda b, i: (row(b, i), cb(COL_V, KV_WIDTH))),
           pl.BlockSpec((None, 1, A_WIDTH), lambda b, i: (l, 0, 0)),
           pl.BlockSpec((None, 1, KV_WIDTH), lambda b, i: (l, 0, 0))],
        out_specs=[pl.BlockSpec((tm, A_WIDTH), lambda b, i: (row(b, i), 0)),
                   pl.BlockSpec((None, None, WINDOW, KV_WIDTH), lambda b, i: (l, b, 0, 0)),
                   pl.BlockSpec((None, None, WINDOW, KV_WIDTH), lambda b, i: (l, b, 0, 0))],
        out_shape=[jax.ShapeDtypeStruct((n_total, A_WIDTH), BF16),
                   jax.ShapeDtypeStruct((DEPTH, batch, WINDOW, KV_WIDTH), F32),
                   jax.ShapeDtypeStruct((DEPTH, batch, WINDOW, KV_WIDTH), F32)],
        scratch_shapes=[pltpu.VMEM((WINDOW, KV_WIDTH), F32),
                        pltpu.VMEM((WINDOW, KV_WIDTH), F32)],
        input_output_aliases={c: 1 + c for c in range(ncar)},
        compiler_params=_params(2),
        name="attn_prompt",
    )(*carriers, sinks, proj, proj, proj, proj, qg_all, kg_all)


def _attn_sample_kernel(sink_ref, q_ref, ga_ref, k_ref, v_ref, ck_ref, cv_ref, qg_ref, kg_ref,
                        o_ref, nk_ref, nv_ref, *, nseq, tdec, layer):
    k = _head_rms(k_ref[...], kg_ref[...])
    v = v_ref[...]
    qn = _head_rms(q_ref[...], qg_ref[...])
    ck = ck_ref[...]
    cv = cv_ref[...]
    k3 = k.reshape(nseq, tdec, KV_WIDTH)
    v3 = v.reshape(nseq, tdec, KV_WIDTH)
    kc = jnp.concatenate([ck, k3], axis=1)
    vc = jnp.concatenate([cv, v3], axis=1)
    nk_ref[...] = kc[:, tdec:, :]
    nv_ref[...] = vc[:, tdec:, :]
    kcb = kc.astype(BF16)
    vcb = vc.astype(BF16)

    rows = A_GROUP * tdec
    keys = WINDOW + tdec
    r = lax.broadcasted_iota(jnp.int32, (1, rows, keys), 1)
    j = lax.broadcasted_iota(jnp.int32, (1, rows, keys), 2)
    rel = WINDOW + (r & (tdec - 1)) - j
    ok = (rel >= 0) & (rel < WINDOW)
    rcol = lax.broadcasted_iota(jnp.int32, (1, rows, 1), 1)

    outs = []
    for n in range(A_KV_HEADS):
        heads = [A_GROUP * n + g for g in range(A_GROUP)]
        qs = jnp.concatenate(
            [qn[:, HEAD_DIM * h:HEAD_DIM * (h + 1)].reshape(nseq, tdec, HEAD_DIM) for h in heads],
            axis=1).astype(BF16)
        kn = kcb[:, :, HEAD_DIM * n:HEAD_DIM * (n + 1)]
        vn = vcb[:, :, HEAD_DIM * n:HEAD_DIM * (n + 1)]
        s = jnp.einsum("bqd,bkd->bqk", qs, kn, preferred_element_type=F32) * (HEAD_DIM ** -0.5)
        sink_col = jnp.full((1, rows, 1), sink_ref[layer, heads[-1]], F32)
        for g in reversed(range(A_GROUP - 1)):
            sink_col = jnp.where(rcol < (g + 1) * tdec, sink_ref[layer, heads[g]], sink_col)
        sm = jnp.where(ok, s, -1e30)
        m = jnp.maximum(jnp.max(sm, axis=-1, keepdims=True), sink_col)
        p = jnp.where(ok, jnp.exp(sm - m), 0.0)
        den = jnp.sum(p, axis=-1, keepdims=True) + jnp.exp(sink_col - m)
        o = jnp.einsum("bqk,bkd->bqd", p.astype(BF16), vn, preferred_element_type=F32) / den
        outs += [o[:, g * tdec:(g + 1) * tdec, :].reshape(nseq * tdec, HEAD_DIM)
                 for g in range(A_GROUP)]
    o_all = jnp.concatenate(outs, axis=1)
    o_ref[...] = (o_all * _silu(ga_ref[...])).astype(BF16)


def _attn_sample(proj, row0, oa, cache_k, cache_v, sinks, qg_all, kg_all, l, carriers,
                 nbatch, tdec, nseq=32):
    tm = nseq * tdec
    rb = row0 // tm
    cb = lambda c, w: c // w
    car = [oa] + list(carriers)
    cache_spec = pl.BlockSpec((None, nseq, WINDOW, KV_WIDTH), lambda s: (l, s, 0, 0))
    return pl.pallas_call(
        _skip(len(car), functools.partial(_attn_sample_kernel, nseq=nseq, tdec=tdec, layer=l)),
        grid=(nbatch // nseq,),
        in_specs=[_CARRIER] * len(car)
        + [pl.BlockSpec(memory_space=pltpu.SMEM),
           pl.BlockSpec((tm, A_WIDTH), lambda s: (rb + s, cb(COL_Q, A_WIDTH))),
           pl.BlockSpec((tm, A_WIDTH), lambda s: (rb + s, cb(COL_GA, A_WIDTH))),
           pl.BlockSpec((tm, KV_WIDTH), lambda s: (rb + s, cb(COL_K, KV_WIDTH))),
           pl.BlockSpec((tm, KV_WIDTH), lambda s: (rb + s, cb(COL_V, KV_WIDTH))),
           cache_spec, cache_spec,
           pl.BlockSpec((None, 1, A_WIDTH), lambda s: (l, 0, 0)),
           pl.BlockSpec((None, 1, KV_WIDTH), lambda s: (l, 0, 0))],
        out_specs=[pl.BlockSpec((tm, A_WIDTH), lambda s: (rb + s, 0)), cache_spec, cache_spec],
        out_shape=[jax.ShapeDtypeStruct(oa.shape, BF16),
                   jax.ShapeDtypeStruct((DEPTH, nbatch, WINDOW, KV_WIDTH), F32),
                   jax.ShapeDtypeStruct((DEPTH, nbatch, WINDOW, KV_WIDTH), F32)],
        input_output_aliases={c: c for c in range(len(car))},
        compiler_params=_params(1),
        name="attn_sample",
    )(*car, sinks, proj, proj, proj, proj, cache_k, cache_v, qg_all, kg_all)


def _stack_heads(x):
    lo = jnp.where(_lane_half_mask(), 1.0, 0.0).astype(x.dtype)
    return jnp.concatenate([x * lo, x * (1 - lo)], axis=0)


def _cat_rows(parts, rows):
    if (rows.stop - rows.start) % 16 == 0:
        return jnp.concatenate([x[rows] for x in parts], axis=0)
    return jnp.concatenate([x.astype(F32)[rows] for x in parts], axis=0).astype(BF16)


def _split3(x):
    hi = x.astype(BF16)
    r1 = x - hi.astype(F32)
    mid = r1.astype(BF16)
    lo = (r1 - mid.astype(F32)).astype(BF16)
    return hi, mid, lo


def _rwkv_masks(blk):
    lb = blk.bit_length() - 1
    t = lax.broadcasted_iota(jnp.int32, (TILE, LANES), 0)
    s = lax.broadcasted_iota(jnp.int32, (TILE, LANES), 1) & (TILE - 1)
    same = (t >> lb) == (s >> lb)
    levels = [((t >> (l + 1)) == (s >> (l + 1))) & (((t >> l) & 1) == 1) & (((s >> l) & 1) == 0)
              for l in range(lb)]
    tt = lax.broadcasted_iota(jnp.int32, (2 * TILE, 3 * TILE), 0)
    ss = lax.broadcasted_iota(jnp.int32, (2 * TILE, 3 * TILE), 1)
    ti = tt & (TILE - 1)
    si = jnp.where(ss >= 2 * TILE, ss - 2 * TILE, jnp.where(ss >= TILE, ss - TILE, ss))
    sel = ((ti >> lb) == (si >> lb)) & ((tt >= TILE) | (si <= ti))
    r2 = lax.broadcasted_iota(jnp.int32, (LANES, LANES), 0)
    c2 = lax.broadcasted_iota(jnp.int32, (LANES, LANES), 1)
    return dict(lb=lb, strict=same & (s < t), incl=same & (s <= t),
                eye=jnp.where(s == t, 1.0, 0.0), levels=levels,
                cumsel=jnp.where(sel, 1.0, 0.0).astype(BF16),
                same_head=(r2 >> 6) == (c2 >> 6))


def _rwkv_tile(xr, xk, xv, xw, gate, prm, states, blk):
    nchunk = xr.shape[0] // TILE
    nseq = TILE // blk
    mk = _rwkv_masks(blk)
    lw = _dot(jnp.tanh(xw).astype(BF16), prm["w2"])
    la = _dot(xw.astype(BF16), prm["a2"])
    ld = (-math.exp(-0.5) * math.log2(math.e)) * _sigmoid(prm["w0"] + lw)
    a = _sigmoid(prm["a0"] + la)
    kkr = xk * prm["k_k"]
    kk = kkr * lax.rsqrt(jnp.maximum(_head_sums_wide(kkr * kkr), 1e-24))
    kmod = xk * (1.0 + (a - 1.0) * prm["k_a"])
    ka = kk * a

    probs = []
    for ch in range(nchunk):
        rows = slice(ch * TILE, (ch + 1) * TILE)
        cum = _dot(mk["cumsel"], jnp.concatenate(_split3(ld[rows]), axis=0))
        c, cl = cum[:TILE], cum[TILE:]
        e_c = jnp.exp2(-c)
        e_l = jnp.exp2(cl - c)
        full = dict(rt=(xr[rows] * jnp.exp2(c)).astype(BF16),
                    at=(-kk[rows] * jnp.exp2(c - ld[rows])).astype(BF16),
                    bt=(ka[rows] * e_c).astype(BF16), kt=(kmod[rows] * e_c).astype(BF16),
                    bh=(ka[rows] * e_l).astype(BF16), kh=(kmod[rows] * e_l).astype(BF16),
                    v=xv[rows].astype(BF16), decay=jnp.exp2(cl))
        for j in range(N_PAIRS):
            sl = slice(j * LANES, (j + 1) * LANES)
            probs.append(dict(ch=ch, j=j, **{n: x[:, sl] for n, x in full.items()}))

    for p in probs:
        p["v_s"] = _stack_heads(p["v"])
        z_s = jnp.concatenate([_stack_heads(p["bt"]), _stack_heads(p["kt"])], axis=0)
        gram = _dot_nt(jnp.concatenate([p["at"], p["rt"]], axis=0), z_s)
        p["a_ab"] = jnp.where(mk["strict"], gram[:TILE, :LANES], 0.0)
        p["a_ak"] = jnp.where(mk["strict"], gram[:TILE, LANES:], 0.0).astype(BF16)
        p["a_r"] = jnp.concatenate([jnp.where(mk["incl"], gram[TILE:, :LANES], 0.0),
                                    jnp.where(mk["incl"], gram[TILE:, LANES:], 0.0)],
                                   axis=1).astype(BF16)
        p["tinv"] = mk["eye"]
        if mk["lb"] > 0:
            p["tinv"] = p["tinv"] + jnp.where(mk["levels"][0], p["a_ab"], 0.0)

    for l in range(1, mk["lb"]):
        for p in probs:
            p["tb"] = p["tinv"].astype(BF16)
            mid = _stack_heads(jnp.where(mk["levels"][l], p["a_ab"], 0.0).astype(BF16))
            p["half"] = _dot(p["tb"], mid).astype(BF16)
        for p in probs:
            p["tinv"] = p["tinv"] + _dot(p["half"], _stack_heads(p["tb"]))

    for p in probs:
        p["akv"] = _dot(p["a_ak"], p["v_s"]).astype(BF16)
    for p in probs:
        p["wu"] = _dot(p["tinv"].astype(BF16),
                       jnp.concatenate([_stack_heads(p["at"]), _stack_heads(p["akv"])], axis=1))
        p["w"], p["u0"] = p["wu"][:, :LANES].astype(BF16), p["wu"][:, LANES:]

    carried = nseq == 1
    if carried:
        for p in probs:
            p["wtb"] = _dot_tn(p["wu"].astype(BF16), p["bh"])
        for p in probs:
            vtk = _dot_tn(p["v"], p["kh"])
            p["m"] = jnp.where(mk["same_head"], p["wtb"][:LANES], 0.0).astype(BF16)
            p["c"] = jnp.where(mk["same_head"], p["wtb"][LANES:] + vtk, 0.0)

    def read_state(p, sts):
        u_parts, rs_parts = [], []
        for q in range(nseq):
            rows = slice(q * blk, (q + 1) * blk)
            res = _dot_nt(_cat_rows([p["w"], p["rt"]], rows), sts[q].astype(BF16))
            u_parts.append(res[:blk] + p["u0"][rows])
            rs_parts.append(res[blk:])
        p["u"] = (jnp.concatenate(u_parts, axis=0) if nseq > 1 else u_parts[0]).astype(BF16)
        p["rs"] = jnp.concatenate(rs_parts, axis=0) if nseq > 1 else rs_parts[0]

    states = [list(st) for st in states]
    y_rows = []
    for ch in range(nchunk):
        cps = [p for p in probs if p["ch"] == ch]
        first = 0 if carried else ch * nseq
        start = [states[p["j"]][first:first + nseq] for p in cps]
        if carried:
            for p in cps:
                s = states[p["j"]][0]
                states[p["j"]][0] = s * p["decay"][0:1] + _dot(s.astype(BF16), p["m"]) + p["c"]
        for p, sts in zip(cps, start):
            read_state(p, sts)
        if not carried:
            for p in cps:
                for q in range(nseq):
                    rows = slice(q * blk, (q + 1) * blk)
                    upd = _dot_tn(_cat_rows([p["u"], p["v"]], rows),
                                  _cat_rows([p["bh"], p["kh"]], rows))
                    states[p["j"]][first + q] = (
                        states[p["j"]][first + q] * p["decay"][q * blk:q * blk + 1]
                        + jnp.where(mk["same_head"], upd, 0.0))
        y_rows.append(jnp.concatenate(
            [_dot(p["a_r"], jnp.concatenate([_stack_heads(p["u"]), p["v_s"]], axis=0)) + p["rs"]
             for p in cps], axis=1))
    y = jnp.concatenate(y_rows, axis=0) if nchunk > 1 else y_rows[0]

    mean = _head_sums_wide(y) * (1.0 / HEAD_DIM)
    yc = y - mean
    var = _head_sums_wide(yc * yc) * (1.0 / HEAD_DIM)
    yn = yc * lax.rsqrt(var + GN_EPS) * prm["gn_w"] + prm["gn_b"]
    yn = yn + _head_sums_wide(xr * kmod * prm["r_k"]) * xv
    return (yn * _silu(gate)).astype(BF16), states


_RWKV_PARAM_NAMES = ("mu", "w0", "w2", "a0", "a2", "k_k", "k_a", "r_k", "gn_w", "gn_b")


def _rwkv_shift_mix(cur, prev_rows, first_row_mask, mu):
    shifted = jnp.where(first_row_mask, prev_rows, pltpu.roll(cur, 1, axis=0))
    return cur + (shifted - cur) * mu


def _rwkv_load_params(refs):
    prm = {n: r[...] for n, r in zip(_RWKV_PARAM_NAMES, refs)}
    mu = prm.pop("mu")
    return prm, (mu[:, 0:B_WIDTH], mu[:, B_WIDTH:2 * B_WIDTH],
                 mu[:, 2 * B_WIDTH:3 * B_WIDTH], mu[:, 3 * B_WIDTH:])


def _rwkv_prompt_kernel(r_ref, k_ref, v_ref, g_ref, w_ref, *rest):
    prm_refs, (o_ref, s_out_ref, sh_out_ref, state, prev) = (rest[:len(_RWKV_PARAM_NAMES)],
                                                               rest[len(_RWKV_PARAM_NAMES):])
    i = pl.program_id(1)

    @pl.when(i == 0)
    def _():
        state[...] = jnp.zeros_like(state)
        prev[...] = jnp.zeros_like(prev)

    prm, mus = _rwkv_load_params(prm_refs)
    cur = (r_ref[...], k_ref[...], v_ref[...], w_ref[...])
    rows = cur[0].shape[0]
    first = lax.broadcasted_iota(jnp.int32, (8, 1), 0) == 0
    offs = (0, B_WIDTH, 2 * B_WIDTH, 3 * B_WIDTH, SHIFT_PAD)
    mixed = []
    for n, x in enumerate(cur):
        m = x + (pltpu.roll(x, 1, axis=0) - x) * mus[n]
        top = x[0:8]
        top = jnp.where(first, top + (prev[0:1, offs[n]:offs[n + 1]] - top) * mus[n], m[0:8])
        mixed.append(jnp.concatenate([top, m[8:]], axis=0))
        prev[0:1, offs[n]:offs[n + 1]] = x[rows - 1:rows, :]
    states = [[state[j]] for j in range(N_PAIRS)]
    out, new_states = _rwkv_tile(*mixed, g_ref[...], prm, states, TILE)
    o_ref[...] = out
    for j in range(N_PAIRS):
        s = new_states[j][0]
        state[j] = s
        s_out_ref[2 * j] = s[:HEAD_DIM, :HEAD_DIM]
        s_out_ref[2 * j + 1] = s[HEAD_DIM:, HEAD_DIM:]
    sh_out_ref[...] = prev[...]


def _rwkv_sample_kernel(r_ref, k_ref, v_ref, g_ref, w_ref, sh_ref, s_in_ref, *rest, tdec):
    prm_refs, (o_ref, s_out_ref, sh_out_ref) = (rest[:len(_RWKV_PARAM_NAMES)],
                                                rest[len(_RWKV_PARAM_NAMES):])
    rows = r_ref.shape[0]
    nseq = rows // tdec
    prm, mus = _rwkv_load_params(prm_refs)
    cur = (r_ref[...], k_ref[...], v_ref[...], w_ref[...])
    first = (lax.broadcasted_iota(jnp.int32, (rows, 1), 0) & (tdec - 1)) == 0
    offs = (0, B_WIDTH, 2 * B_WIDTH, 3 * B_WIDTH, SHIFT_PAD)
    sh = sh_ref[...]
    mixed = []
    for n, x in enumerate(cur):
        prev_rows = jnp.concatenate(
            [jnp.broadcast_to(sh[q:q + 1, offs[n]:offs[n + 1]], (tdec, x.shape[1]))
             for q in range(nseq)], axis=0)
        mixed.append(_rwkv_shift_mix(x, prev_rows, first, mus[n]))
        for q in range(nseq):
            last = (q + 1) * tdec - 1
            sh_out_ref[q:q + 1, offs[n]:offs[n + 1]] = x[last:last + 1, :]
    zero = jnp.zeros((HEAD_DIM, HEAD_DIM), F32)
    states = []
    for j in range(N_PAIRS):
        pair = []
        for q in range(nseq):
            top = jnp.concatenate([s_in_ref[q, 2 * j], zero], axis=1)
            bot = jnp.concatenate([zero, s_in_ref[q, 2 * j + 1]], axis=1)
            pair.append(jnp.concatenate([top, bot], axis=0))
        states.append(pair)
    out, new_states = _rwkv_tile(*mixed, g_ref[...], prm, states, tdec)
    o_ref[...] = out
    for j in range(N_PAIRS):
        for q in range(nseq):
            s = new_states[j][q]
            s_out_ref[q, 2 * j] = s[:HEAD_DIM, :HEAD_DIM]
            s_out_ref[q, 2 * j + 1] = s[HEAD_DIM:, HEAD_DIM:]


def _rwkv_param_specs(l):
    shapes = {"mu": (None, 1, SHIFT_PAD), "w2": (None, LANES, B_WIDTH), "a2": (None, LANES, B_WIDTH)}
    return [pl.BlockSpec(shapes.get(n, (None, 1, B_WIDTH)), lambda *_: (l, 0, 0))
            for n in _RWKV_PARAM_NAMES]


def _rwkv_col_specs(row_fn, rows):
    cols = (COL_R, COL_KB, COL_VB, COL_GB)
    specs = [pl.BlockSpec((rows, B_WIDTH), (lambda *a, c=c: (row_fn(*a), c // B_WIDTH))) for c in cols]
    specs.append(pl.BlockSpec((rows, LANES), lambda *a: (row_fn(*a), COL_WLAL // LANES)))
    return specs


def _rwkv_prompt(proj, prm, l, carriers, batch, seq, rows=RWKV_PROMPT_ROWS):
    nt = seq // rows
    row_fn = lambda b, i: b * nt + i
    ncar = len(carriers)
    return pl.pallas_call(
        _skip(ncar, _rwkv_prompt_kernel),
        grid=(batch, nt),
        in_specs=[_CARRIER] * ncar + _rwkv_col_specs(row_fn, rows) + _rwkv_param_specs(l),
        out_specs=[pl.BlockSpec((rows, B_WIDTH), lambda b, i: (row_fn(b, i), 0)),
                   pl.BlockSpec((None, None, B_HEADS, HEAD_DIM, HEAD_DIM),
                                lambda b, i: (l, b, 0, 0, 0)),
                   pl.BlockSpec((None, None, 8, SHIFT_PAD), lambda b, i: (l, b, 0, 0))],
        out_shape=[jax.ShapeDtypeStruct((proj.shape[0], B_WIDTH), BF16),
                   jax.ShapeDtypeStruct((DEPTH, batch, B_HEADS, HEAD_DIM, HEAD_DIM), F32),
                   jax.ShapeDtypeStruct((DEPTH, batch, 8, SHIFT_PAD), F32)],
        scratch_shapes=[pltpu.VMEM((N_PAIRS, LANES, LANES), F32),
                        pltpu.VMEM((8, SHIFT_PAD), F32)],
        input_output_aliases={c: 1 + c for c in range(ncar)},
        compiler_params=_params(2),
        name="rwkv_prompt",
    )(*carriers, proj, proj, proj, proj, proj, *[prm[n] for n in _RWKV_PARAM_NAMES])


def _rwkv_sample(proj, row0, ob, shift_in, state_in, prm, l, carriers, nbatch, tdec):
    rows = RWKV_SAMPLE_ROWS
    nseq = rows // tdec
    rb = row0 // rows
    row_fn = lambda s: rb + s
    car = [ob] + list(carriers)
    state_spec = pl.BlockSpec((None, nseq, B_HEADS, HEAD_DIM, HEAD_DIM), lambda s: (l, s, 0, 0, 0))
    shift_spec = pl.BlockSpec((None, nseq, SHIFT_PAD), lambda s: (l, s, 0))
    return pl.pallas_call(
        _skip(len(car), functools.partial(_rwkv_sample_kernel, tdec=tdec)),
        grid=(nbatch // nseq,),
        in_specs=[_CARRIER] * len(car) + _rwkv_col_specs(row_fn, rows)
        + [shift_spec, state_spec] + _rwkv_param_specs(l),
        out_specs=[pl.BlockSpec((rows, B_WIDTH), lambda s: (rb + s, 0)), state_spec, shift_spec],
        out_shape=[jax.ShapeDtypeStruct(ob.shape, BF16),
                   jax.ShapeDtypeStruct((DEPTH, nbatch, B_HEADS, HEAD_DIM, HEAD_DIM), F32),
                   jax.ShapeDtypeStruct((DEPTH, nbatch, SHIFT_PAD), F32)],
        input_output_aliases={c: c for c in range(len(car))},
        compiler_params=_params(1),
        name="rwkv_sample",
    )(*car, proj, proj, proj, proj, proj, shift_in, state_in,
      *[prm[n] for n in _RWKV_PARAM_NAMES])


def _window_sums(xe):
    s2 = xe + pltpu.roll(xe, 1, axis=0)
    s4 = s2 + pltpu.roll(s2, 2, axis=0)
    s8 = s4 + pltpu.roll(s4, 4, axis=0)
    s16 = s8 + pltpu.roll(s8, 8, axis=0)
    lane = lax.broadcasted_iota(jnp.int32, (1, C_WIDTH), 1)
    return jnp.where(lane < 64, s2, jnp.where(lane < 128, s4, jnp.where(lane < 192, s8, s16)))


def _pool_window_lane():
    lane = lax.broadcasted_iota(jnp.int32, (1, C_WIDTH), 1)
    return jnp.where(lane < 64, 2, jnp.where(lane < 128, 4, jnp.where(lane < 192, 8, 16)))


def _pool_prompt_rows(u, halo, gate, wbd, scale, it):
    tm = u.shape[0]
    halo = jnp.where(it > 0, halo, 0.0)
    sums = _window_sums(jnp.concatenate([halo, u], axis=0))[16:]
    pos = it * tm + lax.broadcasted_iota(jnp.int32, (tm, 1), 0)
    cnt = jnp.minimum(_pool_window_lane(), pos + 1).astype(F32)
    d = sums / cnt - u
    y = _dot(d.astype(BF16), wbd) * scale
    return (y * _silu(gate)).astype(BF16)


def _pool_sample_kernel(u_ref, h_ref, g_ref, w_ref, sc_ref, o_ref, *, nseq, tdec, pos0):
    u = u_ref[...]
    hist = h_ref[...]
    xe = jnp.concatenate([hist, u.reshape(nseq, tdec, C_WIDTH)], axis=1)
    xe = xe.reshape(nseq * (16 + tdec), C_WIDTH)
    sums = _window_sums(xe).reshape(nseq, 16 + tdec, C_WIDTH)[:, 16:, :]
    sums = sums.reshape(nseq * tdec, C_WIDTH)
    t = lax.broadcasted_iota(jnp.int32, (nseq * tdec, 1), 0) & (tdec - 1)
    cnt = jnp.minimum(_pool_window_lane(), pos0 + t + 1).astype(F32)
    d = sums / cnt - u
    y = _dot(d.astype(BF16), w_ref[...]) * sc_ref[...]
    o_ref[...] = (y * _silu(g_ref[...])).astype(BF16)


def _pool_sample(proj, row0, hist16_all, wbd_all, scale_all, l, nbatch, tdec, pos0, nseq=64):
    tm = nseq * tdec
    rb = row0 // tm
    return pl.pallas_call(
        functools.partial(_pool_sample_kernel, nseq=nseq, tdec=tdec, pos0=pos0),
        grid=(nbatch // nseq,),
        in_specs=[pl.BlockSpec((tm, C_WIDTH), lambda s: (rb + s, COL_UC // C_WIDTH)),
                  pl.BlockSpec((None, nseq, 16, C_WIDTH), lambda s: (l, s, 0, 0)),
                  pl.BlockSpec((tm, C_WIDTH), lambda s: (rb + s, COL_GC // C_WIDTH)),
                  pl.BlockSpec((None, C_WIDTH, C_WIDTH), lambda s: (l, 0, 0)),
                  pl.BlockSpec((None, 1, C_WIDTH), lambda s: (l, 0, 0))],
        out_specs=pl.BlockSpec((tm, C_WIDTH), lambda s: (s, 0)),
        out_shape=jax.ShapeDtypeStruct((nbatch * tdec, C_WIDTH), BF16),
        compiler_params=_params(1),
        name="pool_sample",
    )(proj, hist16_all, proj, wbd_all, scale_all)


def _pad_shift(x):
    pad = jnp.zeros(x.shape[:-1] + (LANES - 2 * LORA,), x.dtype)
    return jnp.concatenate([x, pad], axis=-1)


def _block_diag(w):
    n, g, c, _ = w.shape
    eye = jnp.eye(g, dtype=w.dtype)
    return (eye[None, :, None, :, None] * w[:, :, :, None, :]).reshape(n, g * c, g * c)


def kernel(x_prompt, x_sample, cache_k, cache_v, state_wkv, state_shift, state_pool, norm_g, w_in, q_norm_g, k_norm_g, attn_sinks, shift_mu, decay_w0, decay_w2, iclr_a0, iclr_a2, k_k, k_a, r_k, gn_w, gn_b, pool_w, pool_scale, w_out):
    batch, seq, _ = x_prompt.shape
    nbatch, tdec, _ = x_sample.shape
    wbuf = cache_k.shape[2]
    n_prompt, n_sample = batch * seq, nbatch * tdec
    n_total = n_prompt + n_sample

    row = lambda a: a.reshape(DEPTH, 1, -1)
    lora_pad = jnp.zeros((DEPTH, LANES - LORA, B_WIDTH), F32)
    prm = {
        "mu": row(_pad_shift(shift_mu)), "w0": row(decay_w0), "a0": row(iclr_a0),
        "w2": jnp.concatenate([decay_w2, lora_pad], axis=1).astype(BF16),
        "a2": jnp.concatenate([lora_pad[:, :LORA], iclr_a2, lora_pad[:, LORA:]], axis=1).astype(BF16),
        "k_k": row(k_k), "k_a": row(k_a), "r_k": row(r_k), "gn_w": row(gn_w), "gn_b": row(gn_b),
    }
    g_all = row(norm_g)
    qg_all = row(jnp.tile(q_norm_g, (1, A_HEADS)))
    kg_all = row(jnp.tile(k_norm_g, (1, A_KV_HEADS)))
    w_in_all = jnp.swapaxes(w_in, 1, 2)
    w_out_all = w_out.astype(BF16)
    wbd_all = _block_diag(pool_w).astype(BF16)
    scale_all = row(pool_scale)
    ck_all = cache_k.reshape(DEPTH, nbatch, wbuf, KV_WIDTH)
    cv_all = cache_v.reshape(DEPTH, nbatch, wbuf, KV_WIDTH)
    shift_all = _pad_shift(state_shift)
    hist16_all = jnp.pad(state_pool, ((0, 0), (0, 0), (1, 0), (0, 0)))

    x_p = x_prompt.reshape(n_prompt, D_MODEL)
    x_s = x_sample.reshape(n_sample, D_MODEL)
    kv_p, kv_s, wkv_p, wkv_s = [], [], [], []
    pool_p, pool_s = [], []
    proj = _inproj(x_p, g_all, w_in_all, 0, n_total, 0)
    proj = _inproj(x_s, g_all, w_in_all, 0, n_total, n_prompt, carrier=proj)
    for l in range(DEPTH):

        oa, *kv_p = _attn_prompt(proj, attn_sinks, qg_all, kg_all, l, kv_p, batch, seq)
        oa, *kv_s = _attn_sample(proj, n_prompt, oa, ck_all, cv_all, attn_sinks, qg_all, kg_all,
                                 l, kv_s, nbatch, tdec)
        ob, *wkv_p = _rwkv_prompt(proj, prm, l, wkv_p, batch, seq)
        ob, *wkv_s = _rwkv_sample(proj, n_prompt, ob, shift_all, state_wkv, prm, l, wkv_s,
                                  nbatch, tdec)
        oc_s = _pool_sample(proj, n_prompt, hist16_all, wbd_all, scale_all, l, nbatch, tdec,
                            PAST_LEN)
        x_s = _outproj_sample(x_s, oa, ob, oc_s, w_out_all, l, n_prompt)
        if l + 1 < DEPTH:
            x_p, proj_next = _outproj_prompt(x_p, oa, ob, proj, wbd_all, scale_all, w_out_all, l,
                                             seq, next_in=(g_all, w_in_all, x_s))
        else:
            x_p = _outproj_prompt(x_p, oa, ob, proj, wbd_all, scale_all, w_out_all, l, seq)

        pool_p.append(jnp.stack([proj[(b + 1) * seq - POOL_HIST:(b + 1) * seq,
                                      COL_UC:COL_UC + C_WIDTH] for b in range(batch)]))
        u_s = proj[n_prompt:, COL_UC:COL_UC + C_WIDTH].reshape(nbatch, tdec, C_WIDTH)
        pool_s.append(jnp.concatenate([state_pool[l], u_s], axis=1)[:, -POOL_HIST:])
        if l + 1 < DEPTH:
            proj = proj_next

    heads = lambda a: a.reshape(a.shape[:-1] + (A_KV_HEADS, HEAD_DIM))
    return (x_p.reshape(batch, seq, D_MODEL), x_s.reshape(nbatch, tdec, D_MODEL),
            heads(kv_p[0]), heads(kv_p[1]), wkv_p[0], wkv_p[1][:, :, 0, :SHIFT_WIDTH],
            jnp.stack(pool_p),
            heads(kv_s[0]), heads(kv_s[1]), wkv_s[0], wkv_s[1][:, :, :SHIFT_WIDTH],
            jnp.stack(pool_s))
```
